```python
import math
import functools
import jax
import jax.numpy as jnp
from jax import lax
import numpy as np

D_MODEL = 2048
BATCH = 1
SEQ = 8192
DEPTH = 1
DEC_BATCH = 128
DEC_SEQ = 1
PAST_LEN = 2048
PAGE_SIZE = 128

D_MIX = D_MODEL
D_SSM = D_MIX // 2
D_ATTN = D_MIX - D_SSM
HEAD_DIM = 64
SSM_HEADS = D_SSM // HEAD_DIM
SSM_GROUPS = 4
SSM_STATE = 128
CONV_W = 4
SSD_CHUNK = 128
XBC_DIM = D_SSM + 2 * SSM_GROUPS * SSM_STATE
ATTN_HEADS = D_ATTN // HEAD_DIM
DILATED_BRANCHES = ((128, 1), (512, 4), (2048, 16))
W_MAX = max(w for w, _ in DILATED_BRANCHES)
MEM_LEN = 256
MEM_HEADS = 4
MEM_HEAD_DIM = 128
D_MEM = MEM_HEADS * MEM_HEAD_DIM
D_FF = (8 * D_MODEL + 3 * 256 - 1) // (3 * 256) * 256
IN_SPLITS = (D_SSM, D_SSM + XBC_DIM, D_SSM + XBC_DIM + SSM_HEADS,
             D_SSM + XBC_DIM + SSM_HEADS + D_ATTN, D_SSM + XBC_DIM + SSM_HEADS + 2 * D_ATTN)
D_IN_PROJ = D_SSM + XBC_DIM + SSM_HEADS + 3 * D_ATTN
RMS_EPS = 1e-5

kernel_name = 'hybrid_ssd_dilated_attn_decode_step'


def rmsnorm(x, g):
    xf = x.astype(jnp.float32)
    y = xf * lax.rsqrt(jnp.mean(xf * xf, axis=-1, keepdims=True) + RMS_EPS)
    return (y * g.astype(jnp.float32)).astype(x.dtype)


def causal_conv(u, buf, w, b):
    T = u.shape[1]
    up = jnp.concatenate([buf.astype(u.dtype), u], axis=1)
    y = b + sum(up[:, j:j + T] * w[j] for j in range(CONV_W))
    return jax.nn.silu(y), up[:, T:]


def ssd_scan(xs, dt, A, Bm, Cm, h0):
    Bsz, T, H, P = xs.shape
    L = SSD_CHUNK if T >= SSD_CHUNK else T
    Tp = -(-T // L) * L
    nc = Tp // L
    rep = H // Bm.shape[2]

    def chunks(t):
        t = t.astype(jnp.float32)
        t = jnp.pad(t, [(0, 0), (0, Tp - T)] + [(0, 0)] * (t.ndim - 2))
        return t.reshape((Bsz, nc, L) + t.shape[2:])

    x = chunks(xs)
    dtc = chunks(dt)
    Bh = jnp.repeat(chunks(Bm), rep, axis=3)
    Ch = jnp.repeat(chunks(Cm), rep, axis=3)
    acum = jnp.cumsum(dtc * A, axis=2)
    causal = jnp.tril(jnp.ones((L, L), dtype=bool))
    seg = acum[:, :, :, None, :] - acum[:, :, None, :, :]
    decay = jnp.exp(jnp.where(causal[None, None, :, :, None], seg, -jnp.inf))
    w_intra = jnp.einsum('bclhn,bcshn->bclsh', Ch, Bh) * decay * dtc[:, :, None, :, :]
    y_diag = jnp.einsum('bclsh,bcshp->bclhp', w_intra, x)
    decay_to_end = jnp.exp(acum[:, :, -1:, :] - acum)
    chunk_states = jnp.einsum('bclhn,bclh,bclhp->bchpn', Bh, decay_to_end * dtc, x)
    chunk_decay = jnp.exp(acum[:, :, -1, :])

    def step(h, inp):
        st, dec = inp
        return dec[:, :, None, None] * h + st, h

    h_final, h_prev = lax.scan(step, h0.astype(jnp.float32),
                               (jnp.moveaxis(chunk_states, 1, 0), jnp.moveaxis(chunk_decay, 1, 0)))
    h_prev = jnp.moveaxis(h_prev, 0, 1)
    y_off = jnp.einsum('bclhn,bchpn,bclh->bclhp', Ch, h_prev, jnp.exp(acum))
    y = (y_diag + y_off).reshape(Bsz, Tp, H, P)[:, :T]
    return y, h_final


def ssd_mixer(z, xbc, dt_raw, conv_buf, h0, conv_w, conv_b, dt_bias, a_log, d_skip, norm_g):
    Bsz, T, _ = xbc.shape
    xbc_c, conv_new = causal_conv(xbc, conv_buf, conv_w, conv_b)
    xs, Bm, Cm = jnp.split(xbc_c, [D_SSM, D_SSM + SSM_GROUPS * SSM_STATE], axis=-1)
    xs = xs.reshape(Bsz, T, SSM_HEADS, HEAD_DIM)
    Bm = Bm.reshape(Bsz, T, SSM_GROUPS, SSM_STATE)
    Cm = Cm.reshape(Bsz, T, SSM_GROUPS, SSM_STATE)
    dt = jax.nn.softplus(dt_raw.astype(jnp.float32) + dt_bias.astype(jnp.float32))
    A = -jnp.exp(a_log.astype(jnp.float32))
    y, h_new = ssd_scan(xs, dt, A, Bm, Cm, h0)
    y = y + d_skip.astype(jnp.float32)[:, None] * xs.astype(jnp.float32)
    y = y.reshape(Bsz, T, D_SSM) * jax.nn.silu(z.astype(jnp.float32))
    yg = y.reshape(Bsz, T, SSM_GROUPS, D_SSM // SSM_GROUPS)
    yg = yg * lax.rsqrt(jnp.mean(yg * yg, axis=-1, keepdims=True) + RMS_EPS)
    y = yg.reshape(Bsz, T, D_SSM) * norm_g.astype(jnp.float32)
    return y.astype(z.dtype), conv_new, h_new.astype(h0.dtype)


def dilated_branch_blocked(q, k, v, window, dil):
    Bsz, S, H, Dh = q.shape
    blk = window // dil
    unit = blk * dil
    s_pad = -(-S // unit) * unit
    nb = s_pad // unit

    def to_blocks(t):
        t = jnp.pad(t, ((0, 0), (0, s_pad - S), (0, 0), (0, 0)))
        t = t.reshape(Bsz, s_pad // dil, dil, H, Dh).transpose(0, 2, 1, 3, 4)
        return t.reshape(Bsz, dil, nb, blk, H, Dh)

    def with_prev(t):
        prev = jnp.pad(t[:, :, :-1], ((0, 0), (0, 0), (1, 0), (0, 0), (0, 0), (0, 0)))
        return jnp.concatenate([prev, t], axis=3)

    qb = to_blocks(q)
    kb = with_prev(to_blocks(k))
    vb = with_prev(to_blocks(v))
    s = jnp.einsum('brnqhd,brnkhd->brnhqk', qb, kb, preferred_element_type=jnp.float32) * (Dh ** -0.5)
    qi = jnp.arange(blk)[:, None]
    ki = jnp.arange(2 * blk)[None, :]
    dist = qi + blk - ki
    band = (dist >= 0) & (dist <= blk)
    mask = band[None] & ((jnp.arange(nb)[:, None, None] > 0) | (ki >= blk)[None])
    s = jnp.where(mask[None, None, :, None], s, -jnp.inf)
    m = jnp.max(s, axis=-1, keepdims=True)
    p = jnp.exp(s - m)
    l = jnp.sum(p, axis=-1, keepdims=True)
    o = jnp.einsum('brnhqk,brnkhd->brnqhd', p.astype(vb.dtype), vb, preferred_element_type=jnp.float32)
    o = o / jnp.moveaxis(l, 3, 4)
    lse = jnp.moveaxis((m + jnp.log(l))[..., 0], 3, 4)

    def from_blocks(t):
        t = t.reshape((Bsz, dil, s_pad // dil) + t.shape[4:])
        t = jnp.swapaxes(t, 1, 2)
        return t.reshape((Bsz, s_pad) + t.shape[3:])[:, :S]

    return from_blocks(o), from_blocks(lse)


def dilated_branch_gather(q, k_all, v_all, window, dil):
    Bsz, T, H, Dh = q.shape
    L = k_all.shape[1]
    nw = window // dil
    q_idx = (L - T) + jnp.arange(T)
    idx = q_idx[:, None] - dil * jnp.arange(nw + 1)[None, :]
    valid = idx >= 0
    idx_c = jnp.maximum(idx, 0).reshape(-1)
    kg = jnp.take(k_all, idx_c, axis=1).reshape(Bsz, T, nw + 1, H, Dh)
    vg = jnp.take(v_all, idx_c, axis=1).reshape(Bsz, T, nw + 1, H, Dh)
    s = jnp.einsum('bthd,btjhd->bthj', q, kg, preferred_element_type=jnp.float32) * (Dh ** -0.5)
    s = jnp.where(valid[None, :, None, :], s, -jnp.inf)
    m = jnp.max(s, axis=-1, keepdims=True)
    p = jnp.exp(s - m)
    l = jnp.sum(p, axis=-1, keepdims=True)
    o = jnp.einsum('bthj,btjhd->bthd', p.astype(vg.dtype), vg, preferred_element_type=jnp.float32) / l
    return o, (m + jnp.log(l))[..., 0]


def merge_branches(outs, lses):
    alpha = jax.nn.softmax(jnp.stack(lses, axis=0), axis=0)
    return jnp.einsum('nbth,nbthd->bthd', alpha, jnp.stack(outs, axis=0))


def dilated_attention_prompt(q, k, v):
    res = [dilated_branch_blocked(q, k, v, w, d) for w, d in DILATED_BRANCHES]
    return merge_branches([r[0] for r in res], [r[1] for r in res])


def dilated_attention_sample(q, k, v, past_k, past_v):
    k_all = jnp.concatenate([past_k.astype(k.dtype), k], axis=1)
    v_all = jnp.concatenate([past_v.astype(v.dtype), v], axis=1)
    res = [dilated_branch_gather(q, k_all, v_all, w, d) for w, d in DILATED_BRANCHES]
    return merge_branches([r[0] for r in res], [r[1] for r in res])


def mem_kv(mem, g, w_mk, w_mv):
    Bsz = mem.shape[0]
    m = rmsnorm(mem, g)
    k = (m @ w_mk).reshape(Bsz, MEM_LEN, MEM_HEADS, MEM_HEAD_DIM)
    v = (m @ w_mv).reshape(Bsz, MEM_LEN, MEM_HEADS, MEM_HEAD_DIM)
    return k, v


def mem_attend(h, w_mq, w_mo, mk, mv):
    Bsz, T, _ = h.shape
    q = (h @ w_mq).reshape(Bsz, T, MEM_HEADS, MEM_HEAD_DIM)
    s = jnp.einsum('bthd,bmhd->bhtm', q, mk.astype(q.dtype), preferred_element_type=jnp.float32) * (MEM_HEAD_DIM ** -0.5)
    p = jax.nn.softmax(s, axis=-1)
    o = jnp.einsum('bhtm,bmhd->bthd', p.astype(q.dtype), mv.astype(q.dtype))
    return o.reshape(Bsz, T, D_MEM) @ w_mo


def layer(x, attend, conv_buf, h0, mem_k, mem_v, ln1_g, w_in, conv_w, conv_b, dt_bias, a_log, d_skip,
          ssm_norm_g, w_out, ln_mem_g, w_mq, w_mo, ln2_g, w_gate, w_up, w_down):
    Bsz, T, _ = x.shape
    h = rmsnorm(x, ln1_g)
    z, xbc, dt_raw, q, k, v = jnp.split(h @ w_in, IN_SPLITS, axis=-1)
    y_ssm, conv_new, h_new = ssd_mixer(z, xbc, dt_raw, conv_buf, h0, conv_w, conv_b, dt_bias, a_log, d_skip, ssm_norm_g)
    q = q.reshape(Bsz, T, ATTN_HEADS, HEAD_DIM)
    k = k.reshape(Bsz, T, ATTN_HEADS, HEAD_DIM)
    v = v.reshape(Bsz, T, ATTN_HEADS, HEAD_DIM)
    y_att = attend(q, k, v).reshape(Bsz, T, D_ATTN).astype(x.dtype)
    x = x + jnp.concatenate([y_ssm, y_att], axis=-1) @ w_out
    x = x + mem_attend(rmsnorm(x, ln_mem_g), w_mq, w_mo, mem_k, mem_v)
    h = rmsnorm(x, ln2_g)
    x = x + (jax.nn.silu(h @ w_gate) * (h @ w_up)) @ w_down
    return x, k, v, conv_new, h_new


def setup_inputs(seed: int = 0) -> dict:
    key = jax.random.key(seed)
    ks = jax.random.split(key, 32)
    f32 = jnp.float32

    def nrm(k, shape, scale):
        return jax.random.normal(k, shape, f32) * scale

    def gain(k, shape):
        return 1.0 + 0.02 * jax.random.normal(k, shape, f32)

    w_buf = min(W_MAX, PAST_LEN)
    dt0 = jnp.exp(jax.random.uniform(ks[13], (DEPTH, SSM_HEADS), f32, math.log(1e-3), math.log(1e-1)))
    return {
        'x_prompt': nrm(ks[0], (BATCH, SEQ, D_MODEL), 1.0),
        'x_sample': nrm(ks[1], (DEC_BATCH, DEC_SEQ, D_MODEL), 1.0),
        'cache_win_k': nrm(ks[2], (DEPTH, DEC_BATCH, w_buf, ATTN_HEADS, HEAD_DIM), 1.0),
        'cache_win_v': nrm(ks[3], (DEPTH, DEC_BATCH, w_buf, ATTN_HEADS, HEAD_DIM), 1.0),
        'state_conv': nrm(ks[4], (DEPTH, DEC_BATCH, CONV_W - 1, XBC_DIM), 1.0),
        'state_ssm': nrm(ks[5], (DEPTH, DEC_BATCH, SSM_HEADS, HEAD_DIM, SSM_STATE), 0.1),
        'cache_mem_k': nrm(ks[6], (DEPTH, DEC_BATCH, MEM_LEN, MEM_HEADS, MEM_HEAD_DIM), 1.0),
        'cache_mem_v': nrm(ks[7], (DEPTH, DEC_BATCH, MEM_LEN, MEM_HEADS, MEM_HEAD_DIM), 1.0),
        'mem_prompt': nrm(ks[8], (BATCH, MEM_LEN, D_MODEL), 1.0),
        'ln1_g': gain(ks[9], (DEPTH, D_MODEL)),
        'w_in': nrm(ks[10], (DEPTH, D_MODEL, D_IN_PROJ), D_MODEL ** -0.5),
        'conv_w': nrm(ks[11], (DEPTH, CONV_W, XBC_DIM), CONV_W ** -0.5),
        'conv_b': nrm(ks[12], (DEPTH, XBC_DIM), 0.02),
        'dt_bias': dt0 + jnp.log(-jnp.expm1(-dt0)),
        'a_log': jnp.log(jax.random.uniform(ks[14], (DEPTH, SSM_HEADS), f32, 1.0, 16.0)),
        'd_skip': gain(ks[15], (DEPTH, SSM_HEADS)),
        'ssm_norm_g': gain(ks[16], (DEPTH, D_SSM)),
        'w_out': nrm(ks[17], (DEPTH, D_MIX, D_MODEL), D_MIX ** -0.5),
        'ln_mem_g': gain(ks[18], (DEPTH, D_MODEL)),
        'mem_norm_g': gain(ks[19], (DEPTH, D_MODEL)),
        'w_mq': nrm(ks[20], (DEPTH, D_MODEL, D_MEM), D_MODEL ** -0.5),
        'w_mk': nrm(ks[21], (DEPTH, D_MODEL, D_MEM), D_MODEL ** -0.5),
        'w_mv': nrm(ks[22], (DEPTH, D_MODEL, D_MEM), D_MODEL ** -0.5),
        'w_mo': nrm(ks[23], (DEPTH, D_MEM, D_MODEL), D_MEM ** -0.5),
        'ln2_g': gain(ks[24], (DEPTH, D_MODEL)),
        'w_gate': nrm(ks[25], (DEPTH, D_MODEL, D_FF), D_MODEL ** -0.5),
        'w_up': nrm(ks[26], (DEPTH, D_MODEL, D_FF), D_MODEL ** -0.5),
        'w_down': nrm(ks[27], (DEPTH, D_FF, D_MODEL), D_FF ** -0.5),
        'ln_f_g': gain(ks[28], (D_MODEL,)),
    }


def reference(x_prompt, x_sample, cache_win_k, cache_win_v, state_conv, state_ssm, cache_mem_k, cache_mem_v,
              mem_prompt, ln1_g, w_in, conv_w, conv_b, dt_bias, a_log, d_skip, ssm_norm_g, w_out, ln_mem_g,
              mem_norm_g, w_mq, w_mk, w_mv, w_mo, ln2_g, w_gate, w_up, w_down, ln_f_g):
    Bp, S, _ = x_prompt.shape
    w_keep = min(W_MAX, S)
    zero_conv = jnp.zeros((Bp, CONV_W - 1, XBC_DIM), x_prompt.dtype)
    zero_h = jnp.zeros((Bp, SSM_HEADS, HEAD_DIM, SSM_STATE), jnp.float32)
    xp, xs = x_prompt, x_sample
    pk, pv, pconv, pssm, pmk, pmv, sk, sv, sconv, sssm = ([] for _ in range(10))
    for l in range(DEPTH):
        lw = (ln1_g[l], w_in[l], conv_w[l], conv_b[l], dt_bias[l], a_log[l], d_skip[l], ssm_norm_g[l],
              w_out[l], ln_mem_g[l], w_mq[l], w_mo[l], ln2_g[l], w_gate[l], w_up[l], w_down[l])
        mk, mv = mem_kv(mem_prompt, mem_norm_g[l], w_mk[l], w_mv[l])
        xp, k_p, v_p, c_p, h_p = layer(xp, dilated_attention_prompt, zero_conv, zero_h, mk, mv, *lw)
        pk.append(k_p[:, S - w_keep:])
        pv.append(v_p[:, S - w_keep:])
        pconv.append(c_p)
        pssm.append(h_p)
        pmk.append(mk)
        pmv.append(mv)
        attend_s = functools.partial(dilated_attention_sample, past_k=cache_win_k[l], past_v=cache_win_v[l])
        xs, k_s, v_s, c_s, h_s = layer(xs, attend_s, state_conv[l], state_ssm[l], cache_mem_k[l], cache_mem_v[l], *lw)
        sk.append(k_s)
        sv.append(v_s)
        sconv.append(c_s)
        sssm.append(h_s)
    y_prompt = rmsnorm(xp, ln_f_g)
    y_sample = rmsnorm(xs, ln_f_g)
    win_k_prompt = jnp.stack(pk, axis=0)
    win_v_prompt = jnp.stack(pv, axis=0)
    conv_prompt = jnp.stack(pconv, axis=0)
    ssm_prompt = jnp.stack(pssm, axis=0)
    mem_k_prompt = jnp.stack(pmk, axis=0)
    mem_v_prompt = jnp.stack(pmv, axis=0)
    win_k_sample = jnp.stack(sk, axis=0)
    win_v_sample = jnp.stack(sv, axis=0)
    conv_sample = jnp.stack(sconv, axis=0)
    ssm_sample = jnp.stack(sssm, axis=0)
    return (y_prompt, y_sample, win_k_prompt, win_v_prompt, conv_prompt, ssm_prompt, mem_k_prompt, mem_v_prompt,
            win_k_sample, win_v_sample, conv_sample, ssm_sample)
```

```python
import functools

import jax
import jax.numpy as jnp
from jax import lax
from jax.experimental import pallas as pl
from jax.experimental.pallas import tpu as pltpu

F32 = jnp.float32
BF16 = jnp.bfloat16

D_MODEL = 2048
D_SSM = 1024
D_ATTN = 1024
HEAD_DIM = 64
SSM_HEADS = 16
SSM_GROUPS = 4
SSM_STATE = 128
CONV_W = 4
SSD_CHUNK = 128
XBC_DIM = D_SSM + 2 * SSM_GROUPS * SSM_STATE
ATTN_HEADS = 16
DILATED_BRANCHES = ((128, 1), (512, 4), (2048, 16))
MEM_LEN = 256
MEM_HEADS = 4
MEM_HEAD_DIM = 128
D_MEM = MEM_HEADS * MEM_HEAD_DIM
RMS_EPS = 1e-5
NEG = -1e30

LANES = 128
SUBLANES = 8
VMEM_BYTES_V7X = 64 * 1024 * 1024
VMEM_LIMIT = 56 * 1024 * 1024

COL_Z = 0
COL_XBC = D_SSM
COL_Q = D_SSM + XBC_DIM
COL_K = COL_Q + D_ATTN
COL_V = COL_K + D_ATTN
D_PROJ = COL_V + D_ATTN


def _params(sem):
    return pltpu.CompilerParams(dimension_semantics=sem, vmem_limit_bytes=VMEM_LIMIT)


def _rms(x, g):
    ms = jnp.mean(x * x, axis=-1, keepdims=True)
    return x * lax.rsqrt(ms + RMS_EPS) * g


def _silu(x):
    return x * (1.0 / (1.0 + jnp.exp(-x)))


def _softplus(x):
    return jnp.maximum(x, 0.0) + jnp.log1p(jnp.exp(-jnp.abs(x)))


def _split3(x):
    hi = x.astype(BF16)
    r1 = x - hi.astype(F32)
    mid = r1.astype(BF16)
    lo = (r1 - mid.astype(F32)).astype(BF16)
    return hi, mid, lo


def _dot(a, b):
    return jnp.dot(a, b, preferred_element_type=F32)


def _rms_matmul_kernel(x_ref, g_ref, w_ref, o_ref, hn_ref):
    @pl.when(pl.program_id(1) == 0)
    def _():
        hn_ref[...] = _rms(x_ref[...], g_ref[...]).astype(BF16)

    o_ref[...] = _dot(hn_ref[...], w_ref[...])


def _rms_matmul_dt_kernel(x_ref, g_ref, w_ref, wdt_ref, o_ref, dt_ref, hn_ref):
    @pl.when(pl.program_id(1) == 0)
    def _():
        hn = _rms(x_ref[...], g_ref[...])
        hi = hn.astype(BF16)
        lo = (hn - hi.astype(F32)).astype(BF16)
        hn_ref[...] = hi
        dt_ref[...] = (_dot(hi, wdt_ref[0]) + _dot(lo, wdt_ref[0])
                       + _dot(hi, wdt_ref[1]) + _dot(lo, wdt_ref[1]) + _dot(hi, wdt_ref[2]))

    o_ref[...] = _dot(hn_ref[...], w_ref[...])


def rms_matmul(x, g, w, *, tm, tn, wdt=None):
    m, k = x.shape
    n = w.shape[1]
    grid = (m // tm, n // tn)
    in_specs = [
        pl.BlockSpec((tm, k), lambda i, j: (i, 0)),
        pl.BlockSpec((1, k), lambda i, j: (0, 0)),
        pl.BlockSpec((k, tn), lambda i, j: (0, j)),
    ]
    o_spec = pl.BlockSpec((tm, tn), lambda i, j: (i, j))
    scratch = [pltpu.VMEM((tm, k), BF16)]
    if wdt is None:
        return pl.pallas_call(
            _rms_matmul_kernel, grid=grid, in_specs=in_specs, out_specs=o_spec,
            out_shape=jax.ShapeDtypeStruct((m, n), F32), scratch_shapes=scratch,
            compiler_params=_params(("parallel", "arbitrary")), name="rms_matmul",
        )(x, g.reshape(1, k), w)
    in_specs.append(pl.BlockSpec((3, k, LANES), lambda i, j: (0, 0, 0)))
    return pl.pallas_call(
        _rms_matmul_dt_kernel, grid=grid, in_specs=in_specs,
        out_specs=[o_spec, pl.BlockSpec((tm, LANES), lambda i, j: (i, 0))],
        out_shape=[jax.ShapeDtypeStruct((m, n), F32), jax.ShapeDtypeStruct((m, LANES), F32)],
        scratch_shapes=scratch,
        compiler_params=_params(("parallel", "arbitrary")), name="rms_matmul_dt",
    )(x, g.reshape(1, k), w, wdt)


def _mm_res_kernel(n_pairs, *refs):
    res_ref = refs[2 * n_pairs]
    o_ref = refs[2 * n_pairs + 1]
    acc = res_ref[...]
    for i in range(n_pairs):
        acc = acc + _dot(refs[2 * i][...].astype(BF16), refs[2 * i + 1][...])
    o_ref[...] = acc


def matmul_residual(pairs, res, *, tm, tn):
    m, n = res.shape
    grid = (m // tm, n // tn)
    in_specs, args = [], []
    for a, w, kblk in pairs:
        k = a.shape[1]
        in_specs.append(pl.BlockSpec((tm, k), lambda i, j: (i, 0)))
        in_specs.append(pl.BlockSpec((k, tn), lambda i, j, kblk=kblk: (kblk, j)))
        args += [a, w]
    in_specs.append(pl.BlockSpec((tm, tn), lambda i, j: (i, j)))
    args.append(res)
    return pl.pallas_call(
        functools.partial(_mm_res_kernel, len(pairs)), grid=grid, in_specs=in_specs,
        out_specs=pl.BlockSpec((tm, tn), lambda i, j: (i, j)),
        out_shape=jax.ShapeDtypeStruct((m, n), F32),
        compiler_params=_params(("parallel", "parallel")), name="matmul_residual",
    )(*args)


def _ffn_kernel(x_ref, g2_ref, wg_ref, wu_ref, wd_ref, gf_ref, o_ref, hn_ref, acc_ref):
    f = pl.program_id(1)

    @pl.when(f == 0)
    def _():
        hn_ref[...] = _rms(x_ref[...], g2_ref[...]).astype(BF16)
        acc_ref[...] = jnp.zeros_like(acc_ref)

    hn = hn_ref[...]
    act = _silu(_dot(hn, wg_ref[...])) * _dot(hn, wu_ref[...])
    acc_ref[...] += _dot(act.astype(BF16), wd_ref[...])

    @pl.when(f == pl.num_programs(1) - 1)
    def _():
        o_ref[...] = _rms(x_ref[...] + acc_ref[...], gf_ref[...])


def ffn_final(x, g2, wg, wu, wd, gf, *, tm, tf):
    m, d = x.shape
    dff = wg.shape[1]
    grid = (m // tm, dff // tf)
    return pl.pallas_call(
        _ffn_kernel, grid=grid,
        in_specs=[
            pl.BlockSpec((tm, d), lambda i, f: (i, 0)),
            pl.BlockSpec((1, d), lambda i, f: (0, 0)),
            pl.BlockSpec((d, tf), lambda i, f: (0, f)),
            pl.BlockSpec((d, tf), lambda i, f: (0, f)),
            pl.BlockSpec((tf, d), lambda i, f: (f, 0)),
            pl.BlockSpec((1, d), lambda i, f: (0, 0)),
        ],
        out_specs=pl.BlockSpec((tm, d), lambda i, f: (i, 0)),
        out_shape=jax.ShapeDtypeStruct((m, d), F32),
        scratch_shapes=[pltpu.VMEM((tm, d), BF16), pltpu.VMEM((tm, d), F32)],
        compiler_params=_params(("parallel", "arbitrary")), name="ffn_final",
    )(x, g2.reshape(1, d), wg, wu, wd, gf.reshape(1, d))


def _mem_attn_kernel(q_ref, k_ref, v_ref, o_ref):
    scale = MEM_HEAD_DIM ** -0.5
    for h in range(MEM_HEADS):
        sl = slice(h * MEM_HEAD_DIM, (h + 1) * MEM_HEAD_DIM)
        q = q_ref[:, sl].astype(BF16)
        k = k_ref[:, sl].astype(BF16)
        v = v_ref[:, sl].astype(BF16)
        s = lax.dot_general(q, k, (((1,), (1,)), ((), ())), preferred_element_type=F32) * scale
        m = jnp.max(s, axis=-1, keepdims=True)
        p = jnp.exp(s - m)
        l = jnp.sum(p, axis=-1, keepdims=True)
        p = p / l
        o_ref[:, sl] = _dot(p.astype(BF16), v).astype(o_ref.dtype)


def mem_attention(q, mkv, *, tm):
    m = q.shape[0]
    return pl.pallas_call(
        _mem_attn_kernel, grid=(m // tm,),
        in_specs=[
            pl.BlockSpec((tm, D_MEM), lambda i: (i, 0)),
            pl.BlockSpec((MEM_LEN, D_MEM), lambda i: (0, 0)),
            pl.BlockSpec((MEM_LEN, D_MEM), lambda i: (0, 1)),
        ],
        out_specs=pl.BlockSpec((tm, D_MEM), lambda i: (i, 0)),
        out_shape=jax.ShapeDtypeStruct((m, D_MEM), BF16),
        compiler_params=_params(("parallel",)), name="mem_attention",
    )(q, mkv, mkv)


def _head_cols(x, g, width):
    rows = x.shape[0]
    parts = [jnp.broadcast_to(x[:, 4 * g + i:4 * g + i + 1], (rows, width)) for i in range(4)]
    return jnp.concatenate(parts, axis=1)


def _ssd_prompt_kernel(z_ref, xs_ref, bc_ref, dt_ref, cw_ref, cb_ref, dtb_ref, alog_ref, dskip_ref, ng_ref,
                       y_ref, hfin_ref, convo_ref, cbuf, st_ref):
    c = pl.program_id(0)
    L = SSD_CHUNK
    P = HEAD_DIM
    GW = 4 * P

    @pl.when(c == 0)
    def _():
        cbuf[0:SUBLANES, :] = jnp.zeros((SUBLANES, XBC_DIM), F32)
        st_ref[...] = jnp.zeros_like(st_ref)

    cbuf[SUBLANES:SUBLANES + L, 0:D_SSM] = xs_ref[...]
    cbuf[SUBLANES:SUBLANES + L, D_SSM:XBC_DIM] = bc_ref[...]
    taps = [cbuf[pl.ds(SUBLANES - (CONV_W - 1) + j, L), :] * cw_ref[j:j + 1, :] for j in range(CONV_W)]
    conv = cb_ref[...] + (((taps[0] + taps[1]) + taps[2]) + taps[3])

    @pl.when(c == pl.num_programs(0) - 1)
    def _():
        convo_ref[...] = cbuf[pl.ds(L + SUBLANES - (CONV_W - 1), CONV_W - 1), :]

    cbuf[0:SUBLANES, :] = cbuf[L:L + SUBLANES, :]

    act = _silu(conv)
    xs = act[:, 0:D_SSM]
    dt = _softplus(dt_ref[...] + dtb_ref[...])
    a = dt * (-jnp.exp(alog_ref[...]))
    ri = lax.broadcasted_iota(jnp.int32, (L, L), 0)
    ci = lax.broadcasted_iota(jnp.int32, (L, L), 1)
    causal = ri >= ci
    acum = jnp.dot(causal.astype(F32), a, precision=lax.Precision.HIGHEST, preferred_element_type=F32)
    acum_t = acum.T
    dt_t = dt.T
    last = acum[L - 1:L, :]
    ea = jnp.exp(acum)
    wend = jnp.exp(last - acum) * dt
    cdec = jnp.exp(last)
    lane = lax.broadcasted_iota(jnp.int32, (L, GW), 1)

    ys = []
    for g in range(SSM_GROUPS):
        bg = act[:, D_SSM + g * SSM_STATE:D_SSM + (g + 1) * SSM_STATE]
        cg = act[:, D_SSM + SSM_GROUPS * SSM_STATE + g * SSM_STATE:
                 D_SSM + SSM_GROUPS * SSM_STATE + (g + 1) * SSM_STATE]
        bgb = bg.astype(BF16)
        cgb = cg.astype(BF16)
        cb = lax.dot_general(cgb, bgb, (((1,), (1,)), ((), ())), preferred_element_type=F32)
        xg = xs[:, g * GW:(g + 1) * GW]
        yd = jnp.zeros((L, GW), F32)
        for i in range(4):
            h = 4 * g + i
            seg = acum[:, h:h + 1] - acum_t[h:h + 1, :]
            decay = jnp.exp(jnp.where(causal, seg, NEG))
            w = (cb * decay * dt_t[h:h + 1, :]).astype(BF16)
            xm = jnp.where((lane >= i * P) & (lane < (i + 1) * P), xg, 0.0).astype(BF16)
            yd = yd + _dot(w, xm)
        st = st_ref[g]
        yoff = _dot(cgb, st.astype(BF16)) * _head_cols(ea, g, P)
        ys.append(yd + yoff)
        xw = (xg * _head_cols(wend, g, P)).astype(BF16)
        new = _dot(bg.T.astype(BF16), xw)
        cd = jnp.concatenate([jnp.broadcast_to(cdec[:, 4 * g + i:4 * g + i + 1], (1, P)) for i in range(4)], axis=1)
        st_ref[g] = st * cd + new

    y = jnp.concatenate(ys, axis=1) + dskip_ref[...] * xs
    y = y * _silu(z_ref[...])
    outs = []
    for g in range(SSM_GROUPS):
        yg = y[:, g * GW:(g + 1) * GW]
        outs.append(yg * lax.rsqrt(jnp.mean(yg * yg, axis=-1, keepdims=True) + RMS_EPS))
    y_ref[...] = (jnp.concatenate(outs, axis=1) * ng_ref[...]).astype(y_ref.dtype)

    @pl.when(c == pl.num_programs(0) - 1)
    def _():
        for g in range(SSM_GROUPS):
            stg = st_ref[g]
            for i in range(4):
                hfin_ref[4 * g + i] = stg[:, i * P:(i + 1) * P].T


def _pad_lanes(v):
    return jnp.pad(v.astype(F32), (0, LANES - v.shape[0])).reshape(1, LANES)


def ssd_prompt(proj, dt_raw, conv_w, conv_b, dt_bias, a_log, d_skip, norm_g):
    s = proj.shape[0]
    L = SSD_CHUNK
    nblk = D_SSM // 1024
    col = lambda off: off // 1024
    const = lambda shape: pl.BlockSpec(shape, lambda c: (0,) * len(shape))
    return pl.pallas_call(
        _ssd_prompt_kernel, grid=(s // L,),
        in_specs=[
            pl.BlockSpec((L, D_SSM), lambda c: (c, col(COL_Z))),
            pl.BlockSpec((L, D_SSM), lambda c: (c, col(COL_XBC))),
            pl.BlockSpec((L, XBC_DIM - D_SSM), lambda c: (c, col(COL_XBC + D_SSM))),
            pl.BlockSpec((L, LANES), lambda c: (c, 0)),
            const((CONV_W, XBC_DIM)), const((1, XBC_DIM)), const((1, LANES)), const((1, LANES)),
            const((1, D_SSM)), const((1, D_SSM)),
        ],
        out_specs=[
            pl.BlockSpec((L, D_SSM), lambda c: (c, 0)),
            const((SSM_HEADS, HEAD_DIM, SSM_STATE)),
            const((CONV_W - 1, XBC_DIM)),
        ],
        out_shape=[
            jax.ShapeDtypeStruct((s, D_SSM), BF16),
            jax.ShapeDtypeStruct((SSM_HEADS, HEAD_DIM, SSM_STATE), F32),
            jax.ShapeDtypeStruct((CONV_W - 1, XBC_DIM), F32),
        ],
        scratch_shapes=[
            pltpu.VMEM((L + SUBLANES, XBC_DIM), F32),
            pltpu.VMEM((SSM_GROUPS, SSM_STATE, 4 * HEAD_DIM), F32),
        ],
        compiler_params=_params(("arbitrary",)), name="ssd_prompt",
    )(proj, proj, proj, dt_raw, conv_w, conv_b.reshape(1, XBC_DIM), _pad_lanes(dt_bias), _pad_lanes(a_log),
      jnp.repeat(d_skip, HEAD_DIM).reshape(1, D_SSM), norm_g.reshape(1, D_SSM))


DIL_BLK = 128


def _dil_kernel(q_ref, k_ref, v_ref, o_ref, acc_ref, m_ref, l_ref):
    s_len = q_ref.shape[0]
    B = DIL_BLK
    scale = HEAD_DIM ** -0.5
    lane = lax.broadcasted_iota(jnp.int32, (B, LANES), 1)
    head_a = lane < HEAD_DIM
    qi = lax.broadcasted_iota(jnp.int32, (2 * B, 2 * B), 0) & (B - 1)
    ki = lax.broadcasted_iota(jnp.int32, (2 * B, 2 * B), 1)
    band = (ki >= qi) & (ki <= qi + B)

    def comb(x):
        return jnp.where(head_a, jnp.broadcast_to(x[:B], (B, LANES)), jnp.broadcast_to(x[B:], (B, LANES)))

    def rows(ref, start, d):
        if d == 1:
            return ref[pl.ds(start, B), :]
        return ref[pl.ds(start, B, stride=d), :]

    def store_rows(ref, start, d, val):
        if d == 1:
            ref[pl.ds(start, B), :] = val
        else:
            ref[pl.ds(start, B, stride=d), :] = val

    for w, d in DILATED_BRANCHES:
        assert w // d == B
        first_branch = d == DILATED_BRANCHES[0][1]

        def body(idx, carry, d=d, first_branch=first_branch):
            n = idx // d
            r = idx % d
            start = n * (B * d) + r
            prev = jnp.maximum(n - 1, 0) * (B * d) + r
            q = rows(q_ref, start, d)
            qs = jnp.concatenate([jnp.where(head_a, q, 0.0), jnp.where(head_a, 0.0, q)], axis=0).astype(BF16)
            kk = jnp.concatenate([rows(k_ref, prev, d), rows(k_ref, start, d)], axis=0).astype(BF16)
            vv = jnp.concatenate([rows(v_ref, prev, d), rows(v_ref, start, d)], axis=0).astype(BF16)
            s = lax.dot_general(qs, kk, (((1,), (1,)), ((), ())), preferred_element_type=F32) * scale
            first_key = jnp.where(n > 0, 0, B)
            s = jnp.where(band & (ki >= first_key), s, NEG)
            m_row = jnp.max(s, axis=-1, keepdims=True)
            if first_branch:
                m_new = m_row
            else:
                m_b = rows(m_ref, start, d)
                m_old = jnp.concatenate(
                    [jnp.max(jnp.where(head_a, m_b, NEG), axis=-1, keepdims=True),
                     jnp.max(jnp.where(head_a, NEG, m_b), axis=-1, keepdims=True)], axis=0)
                m_new = jnp.maximum(m_old, m_row)
            p = jnp.exp(s - m_new)
            l_row = jnp.sum(p, axis=-1, keepdims=True)
            pv = _dot(p.astype(BF16), vv)
            pv = jnp.where(head_a, pv[:B], pv[B:])
            if first_branch:
                acc, l_b = pv, comb(l_row)
            else:
                alpha = comb(jnp.exp(m_old - m_new))
                acc = rows(acc_ref, start, d) * alpha + pv
                l_b = rows(l_ref, start, d) * alpha + comb(l_row)
            store_rows(acc_ref, start, d, acc)
            store_rows(l_ref, start, d, l_b)
            store_rows(m_ref, start, d, comb(m_new))
            return carry

        lax.fori_loop(0, s_len // B, body, 0)

    def fin(n, carry):
        sl = pl.ds(pl.multiple_of(n * B, B), B)
        o_ref[sl, :] = (acc_ref[sl, :] / l_ref[sl, :]).astype(o_ref.dtype)
        return carry

    lax.fori_loop(0, s_len // B, fin, 0)


def dilated_prompt(proj):
    s = proj.shape[0]
    n_pairs = D_ATTN // LANES
    spec = lambda off: pl.BlockSpec((s, LANES), lambda h: (0, off // LANES + h))
    return pl.pallas_call(
        _dil_kernel, grid=(n_pairs,),
        in_specs=[spec(COL_Q), spec(COL_K), spec(COL_V)],
        out_specs=pl.BlockSpec((s, LANES), lambda h: (0, h)),
        out_shape=jax.ShapeDtypeStruct((s, D_ATTN), BF16),
        scratch_shapes=[pltpu.VMEM((s, LANES), F32)] * 3,
        compiler_params=_params(("parallel",)), name="dilated_prompt",
    )(proj, proj, proj)


SSD_SAMPLE_BB = 8


def _head_expand(x, seg_t):
    hi, mid, lo = _split3(x)
    return _dot(hi, seg_t) + _dot(mid, seg_t) + _dot(lo, seg_t)


def _ssd_sample_kernel(z_ref, xs_ref, bc_ref, dt_ref, sc_ref, st_ref, cw_ref, cb_ref, dtb_ref, alog_ref,
                       dskip_ref, ng_ref, segt_ref,
                       y_ref, ho_ref, convo_ref,
                       xs_s, t2_s, dec_s, b_s, c_s, col_s, yoff_s):
    step = pl.program_id(0)
    nb = xs_ref.shape[0]
    BB = SSD_SAMPLE_BB
    GW = 4 * HEAD_DIM
    HP = SSM_HEADS * HEAD_DIM

    @pl.when(step == 0)
    def _():
        xnew = jnp.concatenate([xs_ref[...], bc_ref[...]], axis=1)
        rows = [sc_ref[:, j * XBC_DIM:(j + 1) * XBC_DIM] for j in range(CONV_W - 1)] + [xnew]
        taps = [rows[j] * cw_ref[j:j + 1, :] for j in range(CONV_W)]
        act = _silu(cb_ref[...] + (((taps[0] + taps[1]) + taps[2]) + taps[3]))
        for j in range(CONV_W - 1):
            convo_ref[:, j * XBC_DIM:(j + 1) * XBC_DIM] = rows[j + 1]
        xs = act[:, 0:D_SSM]
        bm = act[:, D_SSM:D_SSM + SSM_GROUPS * SSM_STATE]
        cm = act[:, D_SSM + SSM_GROUPS * SSM_STATE:]
        dt = _softplus(dt_ref[...] + dtb_ref[...])
        dec = jnp.exp(dt * (-jnp.exp(alog_ref[...])))
        seg_t = segt_ref[...]
        u = _head_expand(dt, seg_t) * xs
        dec_e = _head_expand(dec, seg_t)
        xs_s[...] = xs
        dec_s[...] = dec_e
        b_s[...] = bm
        c_s[...] = cm
        for g in range(SSM_GROUPS):
            sl = slice(g * SSM_STATE, (g + 1) * SSM_STATE)
            bc = jnp.sum(bm[:, sl] * cm[:, sl], axis=-1, keepdims=True)
            t2_s[:, g * GW:(g + 1) * GW] = u[:, g * GW:(g + 1) * GW] * bc
        for j in range(HP // LANES):
            for src, base in ((u, 0), (dec_e, HP)):
                t = src[:, j * LANES:(j + 1) * LANES].T
                for k, part in enumerate(_split3(t)):
                    col_s[base + j * LANES:base + (j + 1) * LANES, k * nb:(k + 1) * nb] = part

    def per_seq(i, carry):
        b = step * BB + i
        ridx = lax.broadcasted_iota(jnp.int32, (3 * nb, SSM_STATE), 0)
        sel = ((ridx == b) | (ridx == b + nb) | (ridx == b + 2 * nb)).astype(BF16)
        cols = _dot(col_s[...], sel)
        h0 = st_ref[i].reshape(HP, SSM_STATE)
        brow = b_s[pl.ds(b, 1), :]
        c8 = c_s[pl.ds(pl.multiple_of(step * BB, BB), BB), :]
        pick = lax.broadcasted_iota(jnp.int32, (BB, GW), 0) == i
        for g in range(SSM_GROUPS):
            rs = slice(g * GW, (g + 1) * GW)
            ls = slice(g * SSM_STATE, (g + 1) * SSM_STATE)
            h0g = h0[rs]
            hn = cols[HP + g * GW:HP + (g + 1) * GW] * h0g + cols[rs] * brow[:, ls]
            ho_ref[i, 4 * g:4 * g + 4] = hn.reshape(4, HEAD_DIM, SSM_STATE)
            yo = lax.dot_general(c8[:, ls].astype(BF16), h0g.astype(BF16), (((1,), (1,)), ((), ())),
                                 preferred_element_type=F32)
            yoff_s[pl.ds(b, 1), rs] = jnp.sum(jnp.where(pick, yo, 0.0), axis=0, keepdims=True)
        return carry

    lax.fori_loop(0, BB, per_seq, 0)

    @pl.when(step == pl.num_programs(0) - 1)
    def _():
        xs = xs_s[...]
        y = dec_s[...] * yoff_s[...] + t2_s[...] + dskip_ref[...] * xs
        y = y * _silu(z_ref[...])
        for g in range(SSM_GROUPS):
            yg = y[:, g * GW:(g + 1) * GW]
            yg = yg * lax.rsqrt(jnp.mean(yg * yg, axis=-1, keepdims=True) + RMS_EPS)
            y_ref[:, g * GW:(g + 1) * GW] = (yg * ng_ref[:, g * GW:(g + 1) * GW]).astype(y_ref.dtype)


def _seg_ones(n_heads, width):
    c = jnp.arange(n_heads * width)[:, None] // width
    seg = (c == jnp.arange(LANES)[None, :]).astype(BF16)
    return seg, seg.T


def ssd_sample(proj, dt_raw, state_conv, state_ssm, conv_w, conv_b, dt_bias, a_log, d_skip, norm_g):
    nb = proj.shape[0]
    BB = SSD_SAMPLE_BB
    HP = SSM_HEADS * HEAD_DIM
    col = lambda off: off // 1024
    const = lambda shape: pl.BlockSpec(shape, lambda s: (0,) * len(shape))
    _, seg_t = _seg_ones(SSM_HEADS, HEAD_DIM)
    st_spec = pl.BlockSpec((BB, SSM_HEADS, HEAD_DIM, SSM_STATE), lambda s: (s, 0, 0, 0))
    y, h_new, conv_new = pl.pallas_call(
        _ssd_sample_kernel, grid=(nb // BB,),
        in_specs=[
            pl.BlockSpec((nb, D_SSM), lambda s: (0, col(COL_Z))),
            pl.BlockSpec((nb, D_SSM), lambda s: (0, col(COL_XBC))),
            pl.BlockSpec((nb, XBC_DIM - D_SSM), lambda s: (0, col(COL_XBC + D_SSM))),
            const((nb, LANES)), const((nb, (CONV_W - 1) * XBC_DIM)), st_spec,
            const((CONV_W, XBC_DIM)), const((1, XBC_DIM)), const((1, LANES)), const((1, LANES)),
            const((1, D_SSM)), const((1, D_SSM)), const((LANES, HP)),
        ],
        out_specs=[const((nb, D_SSM)), st_spec, const((nb, (CONV_W - 1) * XBC_DIM))],
        out_shape=[
            jax.ShapeDtypeStruct((nb, D_SSM), BF16),
            jax.ShapeDtypeStruct(state_ssm.shape, F32),
            jax.ShapeDtypeStruct((nb, (CONV_W - 1) * XBC_DIM), F32),
        ],
        scratch_shapes=[
            pltpu.VMEM((nb, HP), F32), pltpu.VMEM((nb, HP), F32), pltpu.VMEM((nb, HP), F32),
            pltpu.VMEM((nb, SSM_GROUPS * SSM_STATE), F32), pltpu.VMEM((nb, SSM_GROUPS * SSM_STATE), F32),
            pltpu.VMEM((2 * HP, 3 * nb), BF16), pltpu.VMEM((nb, HP), F32),
        ],
        compiler_params=_params(("arbitrary",)), name="ssd_sample",
    )(proj, proj, proj, dt_raw, state_conv.reshape(nb, (CONV_W - 1) * XBC_DIM), state_ssm, conv_w,
      conv_b.reshape(1, XBC_DIM), _pad_lanes(dt_bias), _pad_lanes(a_log),
      jnp.repeat(d_skip, HEAD_DIM).reshape(1, D_SSM), norm_g.reshape(1, D_SSM), seg_t)
    return y, h_new, conv_new.reshape(nb, CONV_W - 1, XBC_DIM)


def _sq_attn_kernel(n_branch, has_new, scale, *refs):
    q_ref = refs[0]
    pos = 1
    if has_new:
        kn_ref, vn_ref = refs[1], refs[2]
        pos = 3
    kv_refs = refs[pos:pos + 2 * n_branch]
    seg_ref, segt_ref, o_ref = refs[pos + 2 * n_branch:pos + 2 * n_branch + 3]
    b = pl.program_id(0)
    seg = seg_ref[...]
    seg_t = segt_ref[...]
    q = q_ref[pl.ds(b, 1), :]

    def scores(krows):
        return _dot((krows * q).astype(BF16), seg) * scale

    ss = [scores(kv_refs[2 * n][...]) for n in range(n_branch)]
    m = functools.reduce(jnp.maximum, [jnp.max(s, axis=0, keepdims=True) for s in ss])
    if has_new:
        kn = jnp.broadcast_to(kn_ref[pl.ds(b, 1), :], (SUBLANES, q.shape[1]))
        s_new = scores(kn)[0:1]
        m = jnp.maximum(m, s_new)
    l = jnp.zeros_like(m)
    acc = jnp.zeros((1, q.shape[1]), F32)
    for n in range(n_branch):
        p = jnp.exp(ss[n] - m)
        l = l + jnp.sum(p, axis=0, keepdims=True)
        acc = acc + jnp.sum(_dot(p.astype(BF16), seg_t) * kv_refs[2 * n + 1][...], axis=0, keepdims=True)
    if has_new:
        p_new = jnp.exp(s_new - m)
        l = l + n_branch * p_new
        pe = _dot(jnp.broadcast_to(p_new, (SUBLANES, LANES)).astype(BF16), seg_t)[0:1]
        acc = acc + n_branch * (pe * vn_ref[pl.ds(b, 1), :])
    l_e = _head_expand(jnp.broadcast_to(l, (SUBLANES, LANES)), seg_t)[0:1]
    o_ref[pl.ds(b, 1), :] = acc / l_e


def _sq_attention(q_arr, q_spec, new_kv, branches, n_heads, width, scale, name):
    nb = q_arr.shape[0]
    hd = n_heads * width
    seg, seg_t = _seg_ones(n_heads, width)
    args, in_specs = [q_arr], [q_spec]
    if new_kv is not None:
        for arr, spec in new_kv:
            args.append(arr)
            in_specs.append(spec)
    for arr, spec in branches:
        args.append(arr)
        in_specs.append(spec)
    args += [seg, seg_t]
    in_specs += [pl.BlockSpec((hd, LANES), lambda b: (0, 0)), pl.BlockSpec((LANES, hd), lambda b: (0, 0))]
    return pl.pallas_call(
        functools.partial(_sq_attn_kernel, len(branches) // 2, new_kv is not None, scale),
        grid=(nb,), in_specs=in_specs,
        out_specs=pl.BlockSpec((nb, hd), lambda b: (0, 0)),
        out_shape=jax.ShapeDtypeStruct((nb, hd), F32),
        compiler_params=_params(("arbitrary",)), name=name,
    )(*args)


def dilated_sample(proj, cache_k, cache_v):
    nb, w_buf = cache_k.shape[0], cache_k.shape[1]
    col = lambda off: pl.BlockSpec((nb, D_ATTN), lambda b: (0, off // D_ATTN))
    branches = []
    for w, d in DILATED_BRANCHES:
        keys = w // d
        assert w_buf % d == 0 and (w_buf // d) % keys == 0
        blk = (w_buf // d) // keys - 1
        for cache in (cache_k, cache_v):
            branches.append((cache.reshape(nb, w_buf // d, d * D_ATTN),
                             pl.BlockSpec((None, keys, D_ATTN), lambda b, blk=blk: (b, blk, 0))))
    return _sq_attention(proj, col(COL_Q), [(proj, col(COL_K)), (proj, col(COL_V))], branches,
                         ATTN_HEADS, HEAD_DIM, HEAD_DIM ** -0.5, "dilated_sample")


def mem_attention_sample(q, cache_mk, cache_mv):
    nb = q.shape[0]
    branches = [(c.reshape(nb, MEM_LEN, D_MEM), pl.BlockSpec((None, MEM_LEN, D_MEM), lambda b: (b, 0, 0)))
                for c in (cache_mk, cache_mv)]
    return _sq_attention(q, pl.BlockSpec((nb, D_MEM), lambda b: (0, 0)), None, branches,
                         MEM_HEADS, MEM_HEAD_DIM, MEM_HEAD_DIM ** -0.5, "mem_attention_sample")


def _layer_tail(x, y_ssm, y_att, mem_attend, lw, gf, *, tm, tf):
    half = D_SSM
    x1 = matmul_residual([(y_ssm, lw["w_out"], 0), (y_att, lw["w_out"], 1)], x, tm=tm, tn=1024)
    assert y_ssm.shape[1] == half and y_att.shape[1] == half
    hq = rms_matmul(x1, lw["ln_mem_g"], lw["w_mq"], tm=tm, tn=D_MEM)
    x2 = matmul_residual([(mem_attend(hq), lw["w_mo"], 0)], x1, tm=tm, tn=1024)
    return ffn_final(x2, lw["ln2_g"], lw["w_gate"], lw["w_up"], lw["w_down"], gf, tm=min(tm, 512), tf=tf)


def kernel(x_prompt, x_sample, cache_win_k, cache_win_v, state_conv, state_ssm, cache_mem_k, cache_mem_v, mem_prompt, ln1_g, w_in, conv_w, conv_b, dt_bias, a_log, d_skip, ssm_norm_g, w_out, ln_mem_g, mem_norm_g, w_mq, w_mk, w_mv, w_mo, ln2_g, w_gate, w_up, w_down, ln_f_g):
    bp, s, d = x_prompt.shape
    nb, ts, _ = x_sample.shape
    depth = w_in.shape[0]
    assert bp == 1 and ts == 1 and depth == 1
    w_keep = min(DILATED_BRANCHES[-1][0], s)
    l = 0
    dt_lo = D_SSM + XBC_DIM
    dt_hi = dt_lo + SSM_HEADS
    w_main = jnp.concatenate([w_in[l][:, :dt_lo], w_in[l][:, dt_hi:]], axis=1).astype(BF16)
    w_dt = jnp.stack(_split3(jnp.pad(w_in[l][:, dt_lo:dt_hi], ((0, 0), (0, LANES - SSM_HEADS)))))
    lw = dict(w_out=w_out[l].astype(BF16), ln_mem_g=ln_mem_g[l], w_mq=w_mq[l].astype(BF16),
              w_mo=w_mo[l].astype(BF16), ln2_g=ln2_g[l], w_gate=w_gate[l].astype(BF16),
              w_up=w_up[l].astype(BF16), w_down=w_down[l].astype(BF16))
    w_mkv = jnp.concatenate([w_mk[l], w_mv[l]], axis=1).astype(BF16)
    ssm_w = (conv_w[l], conv_b[l], dt_bias[l], a_log[l], d_skip[l], ssm_norm_g[l])

    xp = x_prompt.reshape(s, d)
    mkv = rms_matmul(mem_prompt.reshape(MEM_LEN, d), mem_norm_g[l], w_mkv, tm=MEM_LEN, tn=2 * D_MEM)
    proj, dt_raw = rms_matmul(xp, ln1_g[l], w_main, tm=1024, tn=1024, wdt=w_dt)
    y_ssm, h_p, conv_p = ssd_prompt(proj, dt_raw, *ssm_w)
    y_att = dilated_prompt(proj)
    y_prompt = _layer_tail(xp, y_ssm, y_att, lambda hq: mem_attention(hq, mkv, tm=1024), lw, ln_f_g,
                           tm=1024, tf=512)

    xs = x_sample.reshape(nb, d)
    proj_s, dt_s = rms_matmul(xs, ln1_g[l], w_main, tm=nb, tn=1024, wdt=w_dt)
    y_ssm_s, h_s, conv_s = ssd_sample(proj_s, dt_s, state_conv[l], state_ssm[l], *ssm_w)
    y_att_s = dilated_sample(proj_s, cache_win_k[l], cache_win_v[l])
    y_sample = _layer_tail(xs, y_ssm_s, y_att_s,
                           lambda hq: mem_attention_sample(hq, cache_mem_k[l], cache_mem_v[l]), lw, ln_f_g,
                           tm=nb, tf=512)

    heads = lambda t: t.reshape(t.shape[0], ATTN_HEADS, HEAD_DIM)
    return (
        y_prompt.reshape(bp, s, d),
        y_sample.reshape(nb, ts, d),
        heads(proj[s - w_keep:, COL_K:COL_V])[None, None],
        heads(proj[s - w_keep:, COL_V:])[None, None],
        conv_p[None, None],
        h_p[None, None],
        mkv[:, :D_MEM].reshape(1, 1, MEM_LEN, MEM_HEADS, MEM_HEAD_DIM),
        mkv[:, D_MEM:].reshape(1, 1, MEM_LEN, MEM_HEADS, MEM_HEAD_DIM),
        heads(proj_s[:, COL_K:COL_V])[None, :, None],
        heads(proj_s[:, COL_V:])[None, :, None],
        conv_s[None],
        h_s[None],
    )
```

```python
import functools

import jax
import jax.numpy as jnp
from jax import lax
from jax.experimental import pallas as pl
from jax.experimental.pallas import tpu as pltpu

F32 = jnp.float32
BF16 = jnp.bfloat16

D_MODEL = 2048
D_SSM = 1024
D_ATTN = 1024
HEAD_DIM = 64
SSM_HEADS = 16
SSM_GROUPS = 4
SSM_STATE = 128
CONV_W = 4
SSD_CHUNK = 128
XBC_DIM = D_SSM + 2 * SSM_GROUPS * SSM_STATE
ATTN_HEADS = 16
DILATED_BRANCHES = ((128, 1), (512, 4), (2048, 16))
MEM_LEN = 256
MEM_HEADS = 4
MEM_HEAD_DIM = 128
D_MEM = MEM_HEADS * MEM_HEAD_DIM
RMS_EPS = 1e-5
NEG = -1e30

LANES = 128
SUBLANES = 8
VMEM_BYTES_V7X = 64 * 1024 * 1024
VMEM_LIMIT = 56 * 1024 * 1024

COL_Z = 0
COL_XBC = D_SSM
COL_Q = D_SSM + XBC_DIM
COL_K = COL_Q + D_ATTN
COL_V = COL_K + D_ATTN
D_PROJ = COL_V + D_ATTN


def _params(sem):
    return pltpu.CompilerParams(dimension_semantics=sem, vmem_limit_bytes=VMEM_LIMIT)


def _rms(x, g):
    ms = jnp.mean(x * x, axis=-1, keepdims=True)
    return x * lax.rsqrt(ms + RMS_EPS) * g


def _silu(x):
    return x * (0.5 * jnp.tanh(0.5 * x) + 0.5)


def _softplus(x):
    return jnp.maximum(x, 0.0) + jnp.log1p(jnp.exp(-jnp.abs(x)))


def _split3(x):
    hi = x.astype(BF16)
    r1 = x - hi.astype(F32)
    mid = r1.astype(BF16)
    lo = (r1 - mid.astype(F32)).astype(BF16)
    return hi, mid, lo


def _dot(a, b):
    return jnp.dot(a, b, preferred_element_type=F32)


def _rms_matmul_kernel(x_ref, g_ref, w_ref, o_ref, hn_ref):
    @pl.when(pl.program_id(1) == 0)
    def _():
        hn_ref[...] = _rms(x_ref[...], g_ref[...]).astype(BF16)

    o_ref[...] = _dot(hn_ref[...], w_ref[...])


def _rms_matmul_dt_kernel(x_ref, g_ref, w_ref, wdt_ref, o_ref, dt_ref, hn_ref):
    @pl.when(pl.program_id(1) == 0)
    def _():
        hn = _rms(x_ref[...], g_ref[...])
        hi = hn.astype(BF16)
        lo = (hn - hi.astype(F32)).astype(BF16)
        hn_ref[...] = hi
        both = _dot(hi, wdt_ref[...]) + _dot(lo, wdt_ref[...])
        dt_ref[...] = both[:, :LANES] + both[:, LANES:]

    o_ref[...] = _dot(hn_ref[...], w_ref[...])


def rms_matmul(x, g, w, *, tm, tn, wdt=None):
    m, k = x.shape
    n = w.shape[1]
    grid = (m // tm, n // tn)
    in_specs = [
        pl.BlockSpec((tm, k), lambda i, j: (i, 0)),
        pl.BlockSpec((1, k), lambda i, j: (0, 0)),
        pl.BlockSpec((k, tn), lambda i, j: (0, j)),
    ]
    o_spec = pl.BlockSpec((tm, tn), lambda i, j: (i, j))
    scratch = [pltpu.VMEM((tm, k), BF16)]
    if wdt is None:
        return pl.pallas_call(
            _rms_matmul_kernel, grid=grid, in_specs=in_specs, out_specs=o_spec,
            out_shape=jax.ShapeDtypeStruct((m, n), F32), scratch_shapes=scratch,
            compiler_params=_params(("parallel", "arbitrary")), name="rms_matmul",
        )(x, g.reshape(1, k), w)
    in_specs.append(pl.BlockSpec((k, 2 * LANES), lambda i, j: (0, 0)))
    return pl.pallas_call(
        _rms_matmul_dt_kernel, grid=grid, in_specs=in_specs,
        out_specs=[o_spec, pl.BlockSpec((tm, LANES), lambda i, j: (i, 0))],
        out_shape=[jax.ShapeDtypeStruct((m, n), F32), jax.ShapeDtypeStruct((m, LANES), F32)],
        scratch_shapes=scratch,
        compiler_params=_params(("parallel", "arbitrary")), name="rms_matmul_dt",
    )(x, g.reshape(1, k), w, wdt)


def _mm_res_kernel(n_pairs, *refs):
    res_ref = refs[2 * n_pairs]
    o_ref = refs[2 * n_pairs + 1]
    acc = res_ref[...]
    for i in range(n_pairs):
        acc = acc + _dot(refs[2 * i][...].astype(BF16), refs[2 * i + 1][...])
    o_ref[...] = acc


def matmul_residual(pairs, res, *, tm, tn):
    m, n = res.shape
    grid = (m // tm, n // tn)
    in_specs, args = [], []
    for a, w, kblk in pairs:
        k = a.shape[1]
        in_specs.append(pl.BlockSpec((tm, k), lambda i, j: (i, 0)))
        in_specs.append(pl.BlockSpec((k, tn), lambda i, j, kblk=kblk: (kblk, j)))
        args += [a, w]
    in_specs.append(pl.BlockSpec((tm, tn), lambda i, j: (i, j)))
    args.append(res)
    return pl.pallas_call(
        functools.partial(_mm_res_kernel, len(pairs)), grid=grid, in_specs=in_specs,
        out_specs=pl.BlockSpec((tm, tn), lambda i, j: (i, j)),
        out_shape=jax.ShapeDtypeStruct((m, n), F32),
        compiler_params=_params(("parallel", "parallel")), name="matmul_residual",
    )(*args)


def _ffn_kernel(x_ref, g2_ref, wg_ref, wu_ref, wd_ref, gf_ref, o_ref, hn_ref, acc_ref):
    f = pl.program_id(1)

    @pl.when(f == 0)
    def _():
        hn_ref[...] = _rms(x_ref[...], g2_ref[...]).astype(BF16)
        acc_ref[...] = jnp.zeros_like(acc_ref)

    hn = hn_ref[...]
    act = _silu(_dot(hn, wg_ref[...])) * _dot(hn, wu_ref[...])
    acc_ref[...] += _dot(act.astype(BF16), wd_ref[...])

    @pl.when(f == pl.num_programs(1) - 1)
    def _():
        o_ref[...] = _rms(x_ref[...] + acc_ref[...], gf_ref[...])


def ffn_final(x, g2, wg, wu, wd, gf, *, tm, tf):
    m, d = x.shape
    dff = wg.shape[1]
    grid = (m // tm, dff // tf)
    return pl.pallas_call(
        _ffn_kernel, grid=grid,
        in_specs=[
            pl.BlockSpec((tm, d), lambda i, f: (i, 0)),
            pl.BlockSpec((1, d), lambda i, f: (0, 0)),
            pl.BlockSpec((d, tf), lambda i, f: (0, f)),
            pl.BlockSpec((d, tf), lambda i, f: (0, f)),
            pl.BlockSpec((tf, d), lambda i, f: (f, 0)),
            pl.BlockSpec((1, d), lambda i, f: (0, 0)),
        ],
        out_specs=pl.BlockSpec((tm, d), lambda i, f: (i, 0)),
        out_shape=jax.ShapeDtypeStruct((m, d), F32),
        scratch_shapes=[pltpu.VMEM((tm, d), BF16), pltpu.VMEM((tm, d), F32)],
        compiler_params=_params(("parallel", "arbitrary")), name="ffn_final",
    )(x, g2.reshape(1, d), wg, wu, wd, gf.reshape(1, d))


def _mem_attn_kernel(q_ref, k_ref, v_ref, o_ref):
    scale = MEM_HEAD_DIM ** -0.5
    for h in range(MEM_HEADS):
        sl = slice(h * MEM_HEAD_DIM, (h + 1) * MEM_HEAD_DIM)
        q = q_ref[:, sl].astype(BF16)
        k = k_ref[:, sl].astype(BF16)
        v = v_ref[:, sl].astype(BF16)
        s = lax.dot_general(q, k, (((1,), (1,)), ((), ())), preferred_element_type=F32) * scale
        m = jnp.max(s, axis=-1, keepdims=True)
        p = jnp.exp(s - m)
        l = jnp.sum(p, axis=-1, keepdims=True)
        p = p / l
        o_ref[:, sl] = _dot(p.astype(BF16), v).astype(o_ref.dtype)


def mem_attention(q, mkv, *, tm):
    m = q.shape[0]
    return pl.pallas_call(
        _mem_attn_kernel, grid=(m // tm,),
        in_specs=[
            pl.BlockSpec((tm, D_MEM), lambda i: (i, 0)),
            pl.BlockSpec((MEM_LEN, D_MEM), lambda i: (0, 0)),
            pl.BlockSpec((MEM_LEN, D_MEM), lambda i: (0, 1)),
        ],
        out_specs=pl.BlockSpec((tm, D_MEM), lambda i: (i, 0)),
        out_shape=jax.ShapeDtypeStruct((m, D_MEM), BF16),
        compiler_params=_params(("parallel",)), name="mem_attention",
    )(q, mkv, mkv)


def _head_cols(x, g, width):
    rows = x.shape[0]
    parts = [jnp.broadcast_to(x[:, 4 * g + i:4 * g + i + 1], (rows, width)) for i in range(4)]
    return jnp.concatenate(parts, axis=1)


def _ssd_prompt_kernel(z_ref, xs_ref, bc_ref, dt_ref, cw_ref, cb_ref, dtb_ref, alog_ref, dskip_ref, ng_ref,
                       y_ref, hfin_ref, convo_ref, cbuf, st_ref):
    c = pl.program_id(0)
    L = SSD_CHUNK
    P = HEAD_DIM
    GW = 4 * P

    @pl.when(c == 0)
    def _():
        cbuf[0:SUBLANES, :] = jnp.zeros((SUBLANES, XBC_DIM), F32)
        st_ref[...] = jnp.zeros_like(st_ref)

    cbuf[SUBLANES:SUBLANES + L, 0:D_SSM] = xs_ref[...]
    cbuf[SUBLANES:SUBLANES + L, D_SSM:XBC_DIM] = bc_ref[...]
    up = cbuf[...]
    taps = [(up if j == CONV_W - 1 else pltpu.roll(up, CONV_W - 1 - j, 0))[SUBLANES:SUBLANES + L] * cw_ref[j:j + 1, :]
            for j in range(CONV_W)]
    conv = cb_ref[...] + (((taps[0] + taps[1]) + taps[2]) + taps[3])

    @pl.when(c == pl.num_programs(0) - 1)
    def _():
        convo_ref[...] = cbuf[pl.ds(L + SUBLANES - (CONV_W - 1), CONV_W - 1), :]

    cbuf[0:SUBLANES, :] = cbuf[L:L + SUBLANES, :]

    act = _silu(conv)
    xs = act[:, 0:D_SSM]
    dt = _softplus(dt_ref[...] + dtb_ref[...])
    a = dt * (-jnp.exp(alog_ref[...]))
    ri = lax.broadcasted_iota(jnp.int32, (L, L), 0)
    ci = lax.broadcasted_iota(jnp.int32, (L, L), 1)
    causal = ri >= ci
    acum = jnp.dot(causal.astype(F32), a, precision=lax.Precision.HIGHEST, preferred_element_type=F32)
    acum_t = acum.T
    dt_t = dt.T
    last = acum[L - 1:L, :]
    ea = jnp.exp(acum)
    wend = jnp.exp(last - acum) * dt
    cdec = jnp.exp(last)
    lane = lax.broadcasted_iota(jnp.int32, (L, GW), 1)
    head_lanes = [(lane >= i * P) & (lane < (i + 1) * P) for i in range(4)]

    ys = []
    for g in range(SSM_GROUPS):
        bg = act[:, D_SSM + g * SSM_STATE:D_SSM + (g + 1) * SSM_STATE]
        cg = act[:, D_SSM + SSM_GROUPS * SSM_STATE + g * SSM_STATE:
                 D_SSM + SSM_GROUPS * SSM_STATE + (g + 1) * SSM_STATE]
        bgb = bg.astype(BF16)
        cgb = cg.astype(BF16)
        cb = lax.dot_general(cgb, bgb, (((1,), (1,)), ((), ())), preferred_element_type=F32)
        xg = xs[:, g * GW:(g + 1) * GW]
        yd = jnp.zeros((L, GW), F32)
        for i in range(4):
            h = 4 * g + i
            seg = acum[:, h:h + 1] - acum_t[h:h + 1, :]
            decay = jnp.exp(jnp.where(causal, seg, NEG))
            w = (cb * decay * dt_t[h:h + 1, :]).astype(BF16)
            xm = jnp.where(head_lanes[i], xg, 0.0).astype(BF16)
            yd = yd + _dot(w, xm)
        st = st_ref[g]
        yoff = _dot(cgb, st.astype(BF16)) * _head_cols(ea, g, P)
        ys.append(yd + yoff)
        xw = (xg * _head_cols(wend, g, P)).astype(BF16)
        new = _dot(bg.T.astype(BF16), xw)
        cd = jnp.concatenate([jnp.broadcast_to(cdec[:, 4 * g + i:4 * g + i + 1], (1, P)) for i in range(4)], axis=1)
        st_ref[g] = st * cd + new

    y = jnp.concatenate(ys, axis=1) + dskip_ref[...] * xs
    y = y * _silu(z_ref[...])
    outs = []
    for g in range(SSM_GROUPS):
        yg = y[:, g * GW:(g + 1) * GW]
        outs.append(yg * lax.rsqrt(jnp.mean(yg * yg, axis=-1, keepdims=True) + RMS_EPS))
    y_ref[...] = (jnp.concatenate(outs, axis=1) * ng_ref[...]).astype(y_ref.dtype)

    @pl.when(c == pl.num_programs(0) - 1)
    def _():
        for g in range(SSM_GROUPS):
            stg = st_ref[g]
            for i in range(4):
                hfin_ref[4 * g + i] = stg[:, i * P:(i + 1) * P].T


def _pad_lanes(v):
    return jnp.pad(v.astype(F32), (0, LANES - v.shape[0])).reshape(1, LANES)


def ssd_prompt(proj, dt_raw, conv_w, conv_b, dt_bias, a_log, d_skip, norm_g):
    s = proj.shape[0]
    L = SSD_CHUNK
    nblk = D_SSM // 1024
    col = lambda off: off // 1024
    const = lambda shape: pl.BlockSpec(shape, lambda c: (0,) * len(shape))
    return pl.pallas_call(
        _ssd_prompt_kernel, grid=(s // L,),
        in_specs=[
            pl.BlockSpec((L, D_SSM), lambda c: (c, col(COL_Z))),
            pl.BlockSpec((L, D_SSM), lambda c: (c, col(COL_XBC))),
            pl.BlockSpec((L, XBC_DIM - D_SSM), lambda c: (c, col(COL_XBC + D_SSM))),
            pl.BlockSpec((L, LANES), lambda c: (c, 0)),
            const((CONV_W, XBC_DIM)), const((1, XBC_DIM)), const((1, LANES)), const((1, LANES)),
            const((1, D_SSM)), const((1, D_SSM)),
        ],
        out_specs=[
            pl.BlockSpec((L, D_SSM), lambda c: (c, 0)),
            const((SSM_HEADS, HEAD_DIM, SSM_STATE)),
            const((CONV_W - 1, XBC_DIM)),
        ],
        out_shape=[
            jax.ShapeDtypeStruct((s, D_SSM), BF16),
            jax.ShapeDtypeStruct((SSM_HEADS, HEAD_DIM, SSM_STATE), F32),
            jax.ShapeDtypeStruct((CONV_W - 1, XBC_DIM), F32),
        ],
        scratch_shapes=[
            pltpu.VMEM((L + SUBLANES, XBC_DIM), F32),
            pltpu.VMEM((SSM_GROUPS, SSM_STATE, 4 * HEAD_DIM), F32),
        ],
        compiler_params=_params(("arbitrary",)), name="ssd_prompt",
    )(proj, proj, proj, dt_raw, conv_w, conv_b.reshape(1, XBC_DIM), _pad_lanes(dt_bias), _pad_lanes(a_log),
      jnp.repeat(d_skip, HEAD_DIM).reshape(1, D_SSM), norm_g.reshape(1, D_SSM))


DIL_BLK = 128
DIL_GROUP = 4


def _dil_kernel(q_ref, k_ref, v_ref, o_ref, acc_ref, l_ref, ma_ref, mb_ref, bias_ref):
    s_len = q_ref.shape[0]
    B = DIL_BLK
    scale = HEAD_DIM ** -0.5
    lane = lax.broadcasted_iota(jnp.int32, (B, LANES), 1)
    head_a = lane < HEAD_DIM
    qi = lax.broadcasted_iota(jnp.int32, (2 * B, 2 * B), 0) & (B - 1)
    ki = lax.broadcasted_iota(jnp.int32, (2 * B, 2 * B), 1)
    band = (ki >= qi) & (ki <= qi + B)
    bias_ref[0] = jnp.where(band & (ki >= B), 0.0, NEG)
    bias_ref[1] = jnp.where(band, 0.0, NEG)
    ones = jnp.ones((2 * B, LANES), BF16)

    def comb(x):
        return jnp.where(head_a, x[:B], x[B:])

    def rows(ref, start, d):
        if d == 1:
            return ref[pl.ds(start, B), :]
        return ref[pl.ds(start, B, stride=d), :]

    def store_rows(ref, start, d, val):
        if d == 1:
            ref[pl.ds(start, B), :] = val
        else:
            ref[pl.ds(start, B, stride=d), :] = val

    order = sorted(DILATED_BRANCHES, key=lambda wd: -wd[1])
    assert order[-1][1] == 1
    for bi, (w, d) in enumerate(order):
        assert w // d == B
        first_branch = bi == 0
        last_branch = bi == len(order) - 1
        per_res = s_len // (B * d)
        G = min(DIL_GROUP, per_res)
        assert per_res % G == 0

        def body(t, carry, d=d, first_branch=first_branch, last_branch=last_branch, per_res=per_res, G=G):
            r = t // (per_res // G)
            n0 = (t % (per_res // G)) * G
            starts = [(n0 + j) * (B * d) + r for j in range(G)]
            prev0 = jnp.maximum(n0 - 1, 0) * (B * d) + r
            if d == 1:
                starts = [pl.multiple_of(st, B) for st in starts]
                prev0 = pl.multiple_of(prev0, B)
            kb = [rows(k_ref, st, d).astype(BF16) for st in [prev0] + starts]
            vb = [rows(v_ref, st, d).astype(BF16) for st in [prev0] + starts]
            for j in range(G):
                start = starts[j]
                q = rows(q_ref, start, d) * scale
                qs = jnp.concatenate([jnp.where(head_a, q, 0.0), jnp.where(head_a, 0.0, q)], axis=0).astype(BF16)
                kk = jnp.concatenate([kb[j], kb[j + 1]], axis=0)
                s = lax.dot_general(qs, kk, (((1,), (1,)), ((), ())), preferred_element_type=F32)
                s = s + (bias_ref[jnp.minimum(n0, 1)] if j == 0 else bias_ref[1])
                m_row = jnp.broadcast_to(jnp.max(s, axis=-1, keepdims=True), (2 * B, LANES))
                if first_branch:
                    m_new = m_row
                else:
                    m_old = jnp.concatenate([rows(ma_ref, start, d), rows(mb_ref, start, d)], axis=0)
                    m_new = jnp.maximum(m_old, m_row)
                p = jnp.exp(s - jnp.concatenate([m_new, m_new], axis=1)).astype(BF16)
                pv = _dot(p, jnp.concatenate([jnp.concatenate([vb[j], vb[j + 1]], axis=0), ones], axis=1))
                l_new = comb(pv[:, LANES:])
                pv = comb(pv[:, :LANES])
                if first_branch:
                    acc, l_b = pv, l_new
                else:
                    alpha = comb(jnp.exp(m_old - m_new))
                    acc = rows(acc_ref, start, d) * alpha + pv
                    l_b = rows(l_ref, start, d) * alpha + l_new
                if last_branch:
                    o_ref[pl.ds(start, B), :] = (acc / l_b).astype(o_ref.dtype)
                else:
                    store_rows(acc_ref, start, d, acc)
                    store_rows(l_ref, start, d, l_b)
                    store_rows(ma_ref, start, d, m_new[:B])
                    store_rows(mb_ref, start, d, m_new[B:])
            return carry

        lax.fori_loop(0, s_len // (B * G), body, 0)


def dilated_prompt(proj):
    s = proj.shape[0]
    n_pairs = D_ATTN // LANES
    spec = lambda off: pl.BlockSpec((s, LANES), lambda h: (0, off // LANES + h))
    return pl.pallas_call(
        _dil_kernel, grid=(n_pairs,),
        in_specs=[spec(COL_Q), spec(COL_K), spec(COL_V)],
        out_specs=pl.BlockSpec((s, LANES), lambda h: (0, h)),
        out_shape=jax.ShapeDtypeStruct((s, D_ATTN), BF16),
        scratch_shapes=[pltpu.VMEM((s, LANES), F32)] * 4 + [pltpu.VMEM((2, 2 * DIL_BLK, 2 * DIL_BLK), F32)],
        compiler_params=_params(("parallel",)), name="dilated_prompt",
    )(proj, proj, proj)


SSD_SAMPLE_BB = 8


def _head_expand(x, seg_t):
    hi, mid, lo = _split3(x)
    return _dot(hi, seg_t) + _dot(mid, seg_t) + _dot(lo, seg_t)


def _ssd_sample_kernel(z_ref, xs_ref, bc_ref, dt_ref, sc_ref, st_ref, cw_ref, cb_ref, dtb_ref, alog_ref,
                       dskip_ref, ng_ref, segt_ref,
                       y_ref, ho_ref, convo_ref,
                       xs_s, t2_s, dec_s, b_s, c_s, col_s, yoff_s):
    step = pl.program_id(0)
    nb = xs_ref.shape[0]
    BB = SSD_SAMPLE_BB
    GW = 4 * HEAD_DIM
    HP = SSM_HEADS * HEAD_DIM

    @pl.when(step == 0)
    def _():
        xnew = jnp.concatenate([xs_ref[...], bc_ref[...]], axis=1)
        rows = [sc_ref[:, j * XBC_DIM:(j + 1) * XBC_DIM] for j in range(CONV_W - 1)] + [xnew]
        taps = [rows[j] * cw_ref[j:j + 1, :] for j in range(CONV_W)]
        act = _silu(cb_ref[...] + (((taps[0] + taps[1]) + taps[2]) + taps[3]))
        for j in range(CONV_W - 1):
            convo_ref[:, j * XBC_DIM:(j + 1) * XBC_DIM] = rows[j + 1]
        xs = act[:, 0:D_SSM]
        bm = act[:, D_SSM:D_SSM + SSM_GROUPS * SSM_STATE]
        cm = act[:, D_SSM + SSM_GROUPS * SSM_STATE:]
        dt = _softplus(dt_ref[...] + dtb_ref[...])
        dec = jnp.exp(dt * (-jnp.exp(alog_ref[...])))
        seg_t = segt_ref[...]
        u = _head_expand(dt, seg_t) * xs
        dec_e = _head_expand(dec, seg_t)
        xs_s[...] = xs
        dec_s[...] = dec_e
        b_s[...] = bm
        c_s[...] = cm
        for g in range(SSM_GROUPS):
            sl = slice(g * SSM_STATE, (g + 1) * SSM_STATE)
            bc = jnp.sum(bm[:, sl] * cm[:, sl], axis=-1, keepdims=True)
            t2_s[:, g * GW:(g + 1) * GW] = u[:, g * GW:(g + 1) * GW] * bc
        for j in range(HP // LANES):
            for src, base in ((u, 0), (dec_e, HP)):
                t = src[:, j * LANES:(j + 1) * LANES].T
                for k, part in enumerate(_split3(t)):
                    col_s[base + j * LANES:base + (j + 1) * LANES, k * nb:(k + 1) * nb] = part

    def per_seq(i, carry):
        b = step * BB + i
        ridx = lax.broadcasted_iota(jnp.int32, (3 * nb, SSM_STATE), 0)
        sel = ((ridx == b) | (ridx == b + nb) | (ridx == b + 2 * nb)).astype(BF16)
        cols = _dot(col_s[...], sel)
        h0 = st_ref[i].reshape(HP, SSM_STATE)
        brow = b_s[pl.ds(b, 1), :]
        c8 = c_s[pl.ds(pl.multiple_of(step * BB, BB), BB), :]
        pick = lax.broadcasted_iota(jnp.int32, (BB, GW), 0) == i
        for g in range(SSM_GROUPS):
            rs = slice(g * GW, (g + 1) * GW)
            ls = slice(g * SSM_STATE, (g + 1) * SSM_STATE)
            h0g = h0[rs]
            hn = cols[HP + g * GW:HP + (g + 1) * GW] * h0g + cols[rs] * brow[:, ls]
            ho_ref[i, 4 * g:4 * g + 4] = hn.reshape(4, HEAD_DIM, SSM_STATE)
            yo = lax.dot_general(c8[:, ls].astype(BF16), h0g.astype(BF16), (((1,), (1,)), ((), ())),
                                 preferred_element_type=F32)
            yoff_s[pl.ds(b, 1), rs] = jnp.sum(jnp.where(pick, yo, 0.0), axis=0, keepdims=True)
        return carry

    lax.fori_loop(0, BB, per_seq, 0)

    @pl.when(step == pl.num_programs(0) - 1)
    def _():
        xs = xs_s[...]
        y = dec_s[...] * yoff_s[...] + t2_s[...] + dskip_ref[...] * xs
        y = y * _silu(z_ref[...])
        for g in range(SSM_GROUPS):
            yg = y[:, g * GW:(g + 1) * GW]
            yg = yg * lax.rsqrt(jnp.mean(yg * yg, axis=-1, keepdims=True) + RMS_EPS)
            y_ref[:, g * GW:(g + 1) * GW] = (yg * ng_ref[:, g * GW:(g + 1) * GW]).astype(y_ref.dtype)


def _seg_ones(n_heads, width):
    c = jnp.arange(n_heads * width)[:, None] // width
    seg = (c == jnp.arange(LANES)[None, :]).astype(BF16)
    return seg, seg.T


def ssd_sample(proj, dt_raw, state_conv, state_ssm, conv_w, conv_b, dt_bias, a_log, d_skip, norm_g):
    nb = proj.shape[0]
    BB = SSD_SAMPLE_BB
    HP = SSM_HEADS * HEAD_DIM
    col = lambda off: off // 1024
    const = lambda shape: pl.BlockSpec(shape, lambda s: (0,) * len(shape))
    _, seg_t = _seg_ones(SSM_HEADS, HEAD_DIM)
    st_spec = pl.BlockSpec((BB, SSM_HEADS, HEAD_DIM, SSM_STATE), lambda s: (s, 0, 0, 0))
    y, h_new, conv_new = pl.pallas_call(
        _ssd_sample_kernel, grid=(nb // BB,),
        in_specs=[
            pl.BlockSpec((nb, D_SSM), lambda s: (0, col(COL_Z))),
            pl.BlockSpec((nb, D_SSM), lambda s: (0, col(COL_XBC))),
            pl.BlockSpec((nb, XBC_DIM - D_SSM), lambda s: (0, col(COL_XBC + D_SSM))),
            const((nb, LANES)), const((nb, (CONV_W - 1) * XBC_DIM)), st_spec,
            const((CONV_W, XBC_DIM)), const((1, XBC_DIM)), const((1, LANES)), const((1, LANES)),
            const((1, D_SSM)), const((1, D_SSM)), const((LANES, HP)),
        ],
        out_specs=[const((nb, D_SSM)), st_spec, const((nb, (CONV_W - 1) * XBC_DIM))],
        out_shape=[
            jax.ShapeDtypeStruct((nb, D_SSM), BF16),
            jax.ShapeDtypeStruct(state_ssm.shape, F32),
            jax.ShapeDtypeStruct((nb, (CONV_W - 1) * XBC_DIM), F32),
        ],
        scratch_shapes=[
            pltpu.VMEM((nb, HP), F32), pltpu.VMEM((nb, HP), F32), pltpu.VMEM((nb, HP), F32),
            pltpu.VMEM((nb, SSM_GROUPS * SSM_STATE), F32), pltpu.VMEM((nb, SSM_GROUPS * SSM_STATE), F32),
            pltpu.VMEM((2 * HP, 3 * nb), BF16), pltpu.VMEM((nb, HP), F32),
        ],
        compiler_params=_params(("arbitrary",)), name="ssd_sample",
    )(proj, proj, proj, dt_raw, state_conv.reshape(nb, (CONV_W - 1) * XBC_DIM), state_ssm, conv_w,
      conv_b.reshape(1, XBC_DIM), _pad_lanes(dt_bias), _pad_lanes(a_log),
      jnp.repeat(d_skip, HEAD_DIM).reshape(1, D_SSM), norm_g.reshape(1, D_SSM), seg_t)
    return y, h_new, conv_new.reshape(nb, CONV_W - 1, XBC_DIM)


DIL_SAMPLE_HG = 8


def _dil_sample_kernel(n_branch, q_ref, kn_ref, vn_ref, cnt_ref, kt_ref, vt_ref, o_ref):
    q = q_ref[...] * (HEAD_DIM ** -0.5)
    cnt = cnt_ref[...][None]
    s = jnp.einsum("hqd,hdw->hqw", q.astype(BF16), kt_ref[...].astype(BF16), preferred_element_type=F32)
    s = jnp.where(cnt > 0.0, s, NEG)
    s_new = jnp.sum(q * kn_ref[...], axis=-1, keepdims=True)
    m = jnp.maximum(jnp.max(s, axis=-1, keepdims=True), s_new)
    p = cnt * jnp.exp(s - m)
    p_new = n_branch * jnp.exp(s_new - m)
    l = jnp.sum(p, axis=-1, keepdims=True) + p_new
    o = jnp.einsum("hqw,hdw->hqd", p.astype(BF16), vt_ref[...].astype(BF16), preferred_element_type=F32)
    o_ref[...] = (o + p_new * vn_ref[...]) / l


def dilated_sample(q, kn, vn, cache_k, cache_v):
    nb, w_buf = cache_k.shape[0], cache_k.shape[1]
    hg = DIL_SAMPLE_HG
    pos = jnp.arange(w_buf)
    cnt = jnp.zeros((w_buf,), F32)
    for w, d in DILATED_BRANCHES:
        cnt = cnt + ((pos >= w_buf - w) & ((w_buf - pos) % d == 0)).astype(F32)
    rep = lambda t: jnp.broadcast_to(t[:, :, None, :], (nb, ATTN_HEADS, SUBLANES, HEAD_DIM))
    small = pl.BlockSpec((None, hg, SUBLANES, HEAD_DIM), lambda b, g: (b, g, 0, 0))
    big = pl.BlockSpec((None, hg, HEAD_DIM, w_buf), lambda b, g: (b, g, 0, 0))
    out = pl.pallas_call(
        functools.partial(_dil_sample_kernel, float(len(DILATED_BRANCHES))),
        grid=(nb, ATTN_HEADS // hg),
        in_specs=[small, small, small, pl.BlockSpec((1, w_buf), lambda b, g: (0, 0)), big, big],
        out_specs=small,
        out_shape=jax.ShapeDtypeStruct((nb, ATTN_HEADS, SUBLANES, HEAD_DIM), F32),
        compiler_params=_params(("parallel", "parallel")), name="dilated_sample",
    )(rep(q), rep(kn), rep(vn), cnt.reshape(1, w_buf),
      jnp.transpose(cache_k, (0, 2, 3, 1)), jnp.transpose(cache_v, (0, 2, 3, 1)))
    return out[:, :, 0, :].reshape(nb, D_ATTN)


def _mem_sample_kernel(q_ref, k_ref, v_ref, o_ref):
    q = q_ref[...] * (MEM_HEAD_DIM ** -0.5)
    s = jnp.sum(k_ref[...] * q[None], axis=-1, keepdims=True)
    m = jnp.max(s, axis=0, keepdims=True)
    p = jnp.exp(s - m)
    l = jnp.sum(p, axis=0)
    o_ref[...] = jnp.sum(p * v_ref[...], axis=0) / l


def mem_attention_sample(q, cache_mk, cache_mv):
    nb = q.shape[0]
    small = pl.BlockSpec((None, MEM_HEADS, MEM_HEAD_DIM), lambda b: (b, 0, 0))
    big = pl.BlockSpec((None, MEM_LEN, MEM_HEADS, MEM_HEAD_DIM), lambda b: (b, 0, 0, 0))
    out = pl.pallas_call(
        _mem_sample_kernel, grid=(nb,), in_specs=[small, big, big], out_specs=small,
        out_shape=jax.ShapeDtypeStruct((nb, MEM_HEADS, MEM_HEAD_DIM), F32),
        compiler_params=_params(("parallel",)), name="mem_attention_sample",
    )(q.reshape(nb, MEM_HEADS, MEM_HEAD_DIM), cache_mk, cache_mv)
    return out.reshape(nb, D_MEM)


def _layer_tail(x, y_ssm, y_att, mem_attend, lw, gf, *, tm, tf):
    half = D_SSM
    x1 = matmul_residual([(y_ssm, lw["w_out"], 0), (y_att, lw["w_out"], 1)], x, tm=tm, tn=1024)
    assert y_ssm.shape[1] == half and y_att.shape[1] == half
    hq = rms_matmul(x1, lw["ln_mem_g"], lw["w_mq"], tm=tm, tn=D_MEM)
    x2 = matmul_residual([(mem_attend(hq), lw["w_mo"], 0)], x1, tm=tm, tn=1024)
    return ffn_final(x2, lw["ln2_g"], lw["w_gate"], lw["w_up"], lw["w_down"], gf, tm=min(tm, 512), tf=tf)


def kernel(x_prompt, x_sample, cache_win_k, cache_win_v, state_conv, state_ssm, cache_mem_k, cache_mem_v, mem_prompt, ln1_g, w_in, conv_w, conv_b, dt_bias, a_log, d_skip, ssm_norm_g, w_out, ln_mem_g, mem_norm_g, w_mq, w_mk, w_mv, w_mo, ln2_g, w_gate, w_up, w_down, ln_f_g):
    bp, s, d = x_prompt.shape
    nb, ts, _ = x_sample.shape
    depth = w_in.shape[0]
    assert bp == 1 and ts == 1 and depth == 1
    w_keep = min(DILATED_BRANCHES[-1][0], s)
    l = 0
    dt_lo = D_SSM + XBC_DIM
    dt_hi = dt_lo + SSM_HEADS
    w_main = jnp.concatenate([w_in[l][:, :dt_lo], w_in[l][:, dt_hi:]], axis=1).astype(BF16)
    w_dt = jnp.concatenate(_split3(jnp.pad(w_in[l][:, dt_lo:dt_hi], ((0, 0), (0, LANES - SSM_HEADS))))[:2], axis=1)
    lw = dict(w_out=w_out[l].astype(BF16), ln_mem_g=ln_mem_g[l], w_mq=w_mq[l].astype(BF16),
              w_mo=w_mo[l].astype(BF16), ln2_g=ln2_g[l], w_gate=w_gate[l].astype(BF16),
              w_up=w_up[l].astype(BF16), w_down=w_down[l].astype(BF16))
    w_mkv = jnp.concatenate([w_mk[l], w_mv[l]], axis=1).astype(BF16)
    ssm_w = (conv_w[l], conv_b[l], dt_bias[l], a_log[l], d_skip[l], ssm_norm_g[l])

    xp = x_prompt.reshape(s, d)
    mkv = rms_matmul(mem_prompt.reshape(MEM_LEN, d), mem_norm_g[l], w_mkv, tm=MEM_LEN, tn=2 * D_MEM)
    proj, dt_raw = rms_matmul(xp, ln1_g[l], w_main, tm=1024, tn=1024, wdt=w_dt)
    y_ssm, h_p, conv_p = ssd_prompt(proj, dt_raw, *ssm_w)
    y_att = dilated_prompt(proj)
    y_prompt = _layer_tail(xp, y_ssm, y_att, lambda hq: mem_attention(hq, mkv, tm=1024), lw, ln_f_g,
                           tm=1024, tf=512)

    xs = x_sample.reshape(nb, d)
    proj_s, dt_s = rms_matmul(xs, ln1_g[l], w_main, tm=nb, tn=1024, wdt=w_dt)
    y_ssm_s, h_s, conv_s = ssd_sample(proj_s, dt_s, state_conv[l], state_ssm[l], *ssm_w)
    heads = lambda t: t.reshape(t.shape[0], ATTN_HEADS, HEAD_DIM)
    q_s, k_s, v_s = (heads(proj_s[:, c:c + D_ATTN]) for c in (COL_Q, COL_K, COL_V))
    y_att_s = dilated_sample(q_s, k_s, v_s, cache_win_k[l], cache_win_v[l])
    y_sample = _layer_tail(xs, y_ssm_s, y_att_s,
                           lambda hq: mem_attention_sample(hq, cache_mem_k[l], cache_mem_v[l]), lw, ln_f_g,
                           tm=nb, tf=512)

    return (
        y_prompt.reshape(bp, s, d),
        y_sample.reshape(nb, ts, d),
        heads(proj[s - w_keep:, COL_K:COL_V])[None, None],
        heads(proj[s - w_keep:, COL_V:])[None, None],
        conv_p[None, None],
        h_p[None, None],
        mkv[:, :D_MEM].reshape(1, 1, MEM_LEN, MEM_HEADS, MEM_HEAD_DIM),
        mkv[:, D_MEM:].reshape(1, 1, MEM_LEN, MEM_HEADS, MEM_HEAD_DIM),
        k_s[None, :, None],
        v_s[None, :, None],
        conv_s[None],
        h_s[None],
    )
```

```python
import functools

import jax
import jax.numpy as jnp
from jax import lax
from jax.experimental import pallas as pl
from jax.experimental.pallas import tpu as pltpu

F32 = jnp.float32
BF16 = jnp.bfloat16

D_MODEL = 2048
D_SSM = 1024
D_ATTN = 1024
HEAD_DIM = 64
SSM_HEADS = 16
SSM_GROUPS = 4
SSM_STATE = 128
CONV_W = 4
SSD_CHUNK = 128
XBC_DIM = D_SSM + 2 * SSM_GROUPS * SSM_STATE
ATTN_HEADS = 16
DILATED_BRANCHES = ((128, 1), (512, 4), (2048, 16))
MEM_LEN = 256
MEM_HEADS = 4
MEM_HEAD_DIM = 128
D_MEM = MEM_HEADS * MEM_HEAD_DIM
RMS_EPS = 1e-5
NEG = -1e30

LANES = 128
SUBLANES = 8
VMEM_BYTES_V7X = 64 * 1024 * 1024
VMEM_LIMIT = 56 * 1024 * 1024

COL_Z = 0
COL_XBC = D_SSM
COL_Q = D_SSM + XBC_DIM
COL_K = COL_Q + D_ATTN
COL_V = COL_K + D_ATTN
D_PROJ = COL_V + D_ATTN


def _params(sem):
    return pltpu.CompilerParams(dimension_semantics=sem, vmem_limit_bytes=VMEM_LIMIT)


def _rms(x, g):
    ms = jnp.mean(x * x, axis=-1, keepdims=True)
    return x * lax.rsqrt(ms + RMS_EPS) * g


def _silu(x):
    return x * (0.5 * jnp.tanh(0.5 * x) + 0.5)


def _softplus(x):
    return jnp.maximum(x, 0.0) + jnp.log1p(jnp.exp(-jnp.abs(x)))


def _split3(x):
    hi = x.astype(BF16)
    r1 = x - hi.astype(F32)
    mid = r1.astype(BF16)
    lo = (r1 - mid.astype(F32)).astype(BF16)
    return hi, mid, lo


def _dot(a, b):
    return jnp.dot(a, b, preferred_element_type=F32)


def _rms_matmul_kernel(x_ref, g_ref, w_ref, o_ref, hn_ref):
    @pl.when(pl.program_id(1) == 0)
    def _():
        hn_ref[...] = _rms(x_ref[...], g_ref[...]).astype(BF16)

    o_ref[...] = _dot(hn_ref[...], w_ref[...])


def _rms_matmul_dt_kernel(x_ref, g_ref, w_ref, wdt_ref, o_ref, dt_ref, hn_ref):
    @pl.when(pl.program_id(1) == 0)
    def _():
        hn = _rms(x_ref[...], g_ref[...])
        hi = hn.astype(BF16)
        lo = (hn - hi.astype(F32)).astype(BF16)
        hn_ref[...] = hi
        both = _dot(hi, wdt_ref[...]) + _dot(lo, wdt_ref[...])
        dt_ref[...] = both[:, :LANES] + both[:, LANES:]

    o_ref[...] = _dot(hn_ref[...], w_ref[...])


def rms_matmul(x, g, w, *, tm, tn, wdt=None):
    m, k = x.shape
    n = w.shape[1]
    grid = (m // tm, n // tn)
    in_specs = [
        pl.BlockSpec((tm, k), lambda i, j: (i, 0)),
        pl.BlockSpec((1, k), lambda i, j: (0, 0)),
        pl.BlockSpec((k, tn), lambda i, j: (0, j)),
    ]
    o_spec = pl.BlockSpec((tm, tn), lambda i, j: (i, j))
    scratch = [pltpu.VMEM((tm, k), BF16)]
    if wdt is None:
        return pl.pallas_call(
            _rms_matmul_kernel, grid=grid, in_specs=in_specs, out_specs=o_spec,
            out_shape=jax.ShapeDtypeStruct((m, n), F32), scratch_shapes=scratch,
            compiler_params=_params(("parallel", "arbitrary")), name="rms_matmul",
        )(x, g.reshape(1, k), w)
    in_specs.append(pl.BlockSpec((k, 2 * LANES), lambda i, j: (0, 0)))
    return pl.pallas_call(
        _rms_matmul_dt_kernel, grid=grid, in_specs=in_specs,
        out_specs=[o_spec, pl.BlockSpec((tm, LANES), lambda i, j: (i, 0))],
        out_shape=[jax.ShapeDtypeStruct((m, n), F32), jax.ShapeDtypeStruct((m, LANES), F32)],
        scratch_shapes=scratch,
        compiler_params=_params(("parallel", "arbitrary")), name="rms_matmul_dt",
    )(x, g.reshape(1, k), w, wdt)


def _mm_res_kernel(n_pairs, *refs):
    res_ref = refs[2 * n_pairs]
    o_ref = refs[2 * n_pairs + 1]
    acc = res_ref[...]
    for i in range(n_pairs):
        acc = acc + _dot(refs[2 * i][...].astype(BF16), refs[2 * i + 1][...])
    o_ref[...] = acc


def matmul_residual(pairs, res, *, tm, tn):
    m, n = res.shape
    grid = (m // tm, n // tn)
    in_specs, args = [], []
    for a, w, kblk in pairs:
        k = a.shape[1]
        in_specs.append(pl.BlockSpec((tm, k), lambda i, j: (i, 0)))
        in_specs.append(pl.BlockSpec((k, tn), lambda i, j, kblk=kblk: (kblk, j)))
        args += [a, w]
    in_specs.append(pl.BlockSpec((tm, tn), lambda i, j: (i, j)))
    args.append(res)
    return pl.pallas_call(
        functools.partial(_mm_res_kernel, len(pairs)), grid=grid, in_specs=in_specs,
        out_specs=pl.BlockSpec((tm, tn), lambda i, j: (i, j)),
        out_shape=jax.ShapeDtypeStruct((m, n), F32),
        compiler_params=_params(("parallel", "parallel")), name="matmul_residual",
    )(*args)


def _ffn_kernel(x_ref, g2_ref, wg_ref, wu_ref, wd_ref, gf_ref, o_ref, hn_ref, acc_ref):
    f = pl.program_id(1)

    @pl.when(f == 0)
    def _():
        hn_ref[...] = _rms(x_ref[...], g2_ref[...]).astype(BF16)
        acc_ref[...] = jnp.zeros_like(acc_ref)

    hn = hn_ref[...]
    act = _silu(_dot(hn, wg_ref[...])) * _dot(hn, wu_ref[...])
    acc_ref[...] += _dot(act.astype(BF16), wd_ref[...])

    @pl.when(f == pl.num_programs(1) - 1)
    def _():
        o_ref[...] = _rms(x_ref[...] + acc_ref[...], gf_ref[...])


def ffn_final(x, g2, wg, wu, wd, gf, *, tm, tf):
    m, d = x.shape
    dff = wg.shape[1]
    grid = (m // tm, dff // tf)
    return pl.pallas_call(
        _ffn_kernel, grid=grid,
        in_specs=[
            pl.BlockSpec((tm, d), lambda i, f: (i, 0)),
            pl.BlockSpec((1, d), lambda i, f: (0, 0)),
            pl.BlockSpec((d, tf), lambda i, f: (0, f)),
            pl.BlockSpec((d, tf), lambda i, f: (0, f)),
            pl.BlockSpec((tf, d), lambda i, f: (f, 0)),
            pl.BlockSpec((1, d), lambda i, f: (0, 0)),
        ],
        out_specs=pl.BlockSpec((tm, d), lambda i, f: (i, 0)),
        out_shape=jax.ShapeDtypeStruct((m, d), F32),
        scratch_shapes=[pltpu.VMEM((tm, d), BF16), pltpu.VMEM((tm, d), F32)],
        compiler_params=_params(("parallel", "arbitrary")), name="ffn_final",
    )(x, g2.reshape(1, d), wg, wu, wd, gf.reshape(1, d))


def _mix_mem_kernel(x_ref, ys_ref, ya_ref, wo_ref, g_ref, wq_ref, mkv_ref, wm_ref, o_ref):
    half = ys_ref.shape[1]
    x1 = x_ref[...] + _dot(ys_ref[...], wo_ref[0:half, :]) + _dot(ya_ref[...], wo_ref[half:, :])
    hq = _dot(_rms(x1, g_ref[...]).astype(BF16), wq_ref[...])
    scale = MEM_HEAD_DIM ** -0.5
    outs = []
    for h in range(MEM_HEADS):
        sl = slice(h * MEM_HEAD_DIM, (h + 1) * MEM_HEAD_DIM)
        k = mkv_ref[:, sl].astype(BF16)
        v = mkv_ref[:, D_MEM + h * MEM_HEAD_DIM:D_MEM + (h + 1) * MEM_HEAD_DIM].astype(BF16)
        s = lax.dot_general(hq[:, sl].astype(BF16), k, (((1,), (1,)), ((), ())), preferred_element_type=F32) * scale
        p = jnp.exp(s - jnp.max(s, axis=-1, keepdims=True))
        p = p / jnp.sum(p, axis=-1, keepdims=True)
        outs.append(_dot(p.astype(BF16), v).astype(BF16))
    o_ref[...] = x1 + _dot(jnp.concatenate(outs, axis=1), wm_ref[...])


def mix_mem_fused(x, y_ssm, y_att, w_out, ln_mem_g, w_mq, mkv, w_mo, *, tm):
    m, d = x.shape
    half = y_ssm.shape[1]
    row = lambda width: pl.BlockSpec((tm, width), lambda i: (i, 0))
    whole = lambda a: pl.BlockSpec(a.shape, lambda i: (0, 0), pipeline_mode=pl.Buffered(1))
    g = ln_mem_g.reshape(1, d)
    return pl.pallas_call(
        _mix_mem_kernel, grid=(m // tm,),
        in_specs=[row(d), row(half), row(half), whole(w_out), whole(g), whole(w_mq), whole(mkv), whole(w_mo)],
        out_specs=row(d),
        out_shape=jax.ShapeDtypeStruct((m, d), F32),
        compiler_params=_params(("parallel",)), name="mix_mem_fused",
    )(x, y_ssm, y_att, w_out, g, w_mq, mkv, w_mo)


def _head_cols(x, g, width):
    rows = x.shape[0]
    parts = [jnp.broadcast_to(x[:, 4 * g + i:4 * g + i + 1], (rows, width)) for i in range(4)]
    return jnp.concatenate(parts, axis=1)


def _ssd_prompt_kernel(z_ref, xs_ref, bc_ref, dt_ref, cw_ref, cb_ref, dtb_ref, alog_ref, dskip_ref, ng_ref,
                       y_ref, hfin_ref, convo_ref, cbuf, st_ref):
    c = pl.program_id(0)
    L = SSD_CHUNK
    P = HEAD_DIM
    GW = 4 * P

    @pl.when(c == 0)
    def _():
        cbuf[0:SUBLANES, :] = jnp.zeros((SUBLANES, XBC_DIM), F32)
        st_ref[...] = jnp.zeros_like(st_ref)

    cbuf[SUBLANES:SUBLANES + L, 0:D_SSM] = xs_ref[...]
    cbuf[SUBLANES:SUBLANES + L, D_SSM:XBC_DIM] = bc_ref[...]
    up = cbuf[...]
    taps = [(up if j == CONV_W - 1 else pltpu.roll(up, CONV_W - 1 - j, 0))[SUBLANES:SUBLANES + L] * cw_ref[j:j + 1, :]
            for j in range(CONV_W)]
    conv = cb_ref[...] + (((taps[0] + taps[1]) + taps[2]) + taps[3])

    @pl.when(c == pl.num_programs(0) - 1)
    def _():
        convo_ref[...] = cbuf[pl.ds(L + SUBLANES - (CONV_W - 1), CONV_W - 1), :]

    cbuf[0:SUBLANES, :] = cbuf[L:L + SUBLANES, :]

    act = _silu(conv)
    xs = act[:, 0:D_SSM]
    dt = _softplus(dt_ref[...] + dtb_ref[...])
    a = dt * (-jnp.exp(alog_ref[...]))
    ri = lax.broadcasted_iota(jnp.int32, (L, L), 0)
    ci = lax.broadcasted_iota(jnp.int32, (L, L), 1)
    causal = ri >= ci
    acum = jnp.dot(causal.astype(F32), a, precision=lax.Precision.HIGHEST, preferred_element_type=F32)
    acum_t = acum.T
    dt_t = dt.T
    last = acum[L - 1:L, :]
    ea = jnp.exp(acum)
    wend = jnp.exp(last - acum) * dt
    cdec = jnp.exp(last)
    lane = lax.broadcasted_iota(jnp.int32, (L, GW), 1)
    head_lanes = [(lane >= i * P) & (lane < (i + 1) * P) for i in range(4)]

    ys = []
    for g in range(SSM_GROUPS):
        bg = act[:, D_SSM + g * SSM_STATE:D_SSM + (g + 1) * SSM_STATE]
        cg = act[:, D_SSM + SSM_GROUPS * SSM_STATE + g * SSM_STATE:
                 D_SSM + SSM_GROUPS * SSM_STATE + (g + 1) * SSM_STATE]
        bgb = bg.astype(BF16)
        cgb = cg.astype(BF16)
        cb = lax.dot_general(cgb, bgb, (((1,), (1,)), ((), ())), preferred_element_type=F32)
        xg = xs[:, g * GW:(g + 1) * GW]
        yd = jnp.zeros((L, GW), F32)
        for i in range(4):
            h = 4 * g + i
            seg = acum[:, h:h + 1] - acum_t[h:h + 1, :]
            decay = jnp.exp(jnp.where(causal, seg, NEG))
            w = (cb * decay * dt_t[h:h + 1, :]).astype(BF16)
            xm = jnp.where(head_lanes[i], xg, 0.0).astype(BF16)
            yd = yd + _dot(w, xm)
        st = st_ref[g]
        yoff = _dot(cgb, st.astype(BF16)) * _head_cols(ea, g, P)
        ys.append(yd + yoff)
        xw = (xg * _head_cols(wend, g, P)).astype(BF16)
        new = _dot(bg.T.astype(BF16), xw)
        cd = jnp.concatenate([jnp.broadcast_to(cdec[:, 4 * g + i:4 * g + i + 1], (1, P)) for i in range(4)], axis=1)
        st_ref[g] = st * cd + new

    y = jnp.concatenate(ys, axis=1) + dskip_ref[...] * xs
    y = y * _silu(z_ref[...])
    outs = []
    for g in range(SSM_GROUPS):
        yg = y[:, g * GW:(g + 1) * GW]
        outs.append(yg * lax.rsqrt(jnp.mean(yg * yg, axis=-1, keepdims=True) + RMS_EPS))
    y_ref[...] = (jnp.concatenate(outs, axis=1) * ng_ref[...]).astype(y_ref.dtype)

    @pl.when(c == pl.num_programs(0) - 1)
    def _():
        for g in range(SSM_GROUPS):
            stg = st_ref[g]
            for i in range(4):
                hfin_ref[4 * g + i] = stg[:, i * P:(i + 1) * P].T


def _pad_lanes(v):
    return jnp.pad(v.astype(F32), (0, LANES - v.shape[0])).reshape(1, LANES)


def ssd_prompt(proj, dt_raw, conv_w, conv_b, dt_bias, a_log, d_skip, norm_g):
    s = proj.shape[0]
    L = SSD_CHUNK
    nblk = D_SSM // 1024
    col = lambda off: off // 1024
    const = lambda shape: pl.BlockSpec(shape, lambda c: (0,) * len(shape))
    return pl.pallas_call(
        _ssd_prompt_kernel, grid=(s // L,),
        in_specs=[
            pl.BlockSpec((L, D_SSM), lambda c: (c, col(COL_Z))),
            pl.BlockSpec((L, D_SSM), lambda c: (c, col(COL_XBC))),
            pl.BlockSpec((L, XBC_DIM - D_SSM), lambda c: (c, col(COL_XBC + D_SSM))),
            pl.BlockSpec((L, LANES), lambda c: (c, 0)),
            const((CONV_W, XBC_DIM)), const((1, XBC_DIM)), const((1, LANES)), const((1, LANES)),
            const((1, D_SSM)), const((1, D_SSM)),
        ],
        out_specs=[
            pl.BlockSpec((L, D_SSM), lambda c: (c, 0)),
            const((SSM_HEADS, HEAD_DIM, SSM_STATE)),
            const((CONV_W - 1, XBC_DIM)),
        ],
        out_shape=[
            jax.ShapeDtypeStruct((s, D_SSM), BF16),
            jax.ShapeDtypeStruct((SSM_HEADS, HEAD_DIM, SSM_STATE), F32),
            jax.ShapeDtypeStruct((CONV_W - 1, XBC_DIM), F32),
        ],
        scratch_shapes=[
            pltpu.VMEM((L + SUBLANES, XBC_DIM), F32),
            pltpu.VMEM((SSM_GROUPS, SSM_STATE, 4 * HEAD_DIM), F32),
        ],
        compiler_params=_params(("arbitrary",)), name="ssd_prompt",
    )(proj, proj, proj, dt_raw, conv_w, conv_b.reshape(1, XBC_DIM), _pad_lanes(dt_bias), _pad_lanes(a_log),
      jnp.repeat(d_skip, HEAD_DIM).reshape(1, D_SSM), norm_g.reshape(1, D_SSM))


DIL_BLK = 128
DIL_GROUP = 4


def _dil_kernel(q_ref, k_ref, v_ref, o_ref, acc_ref, l_ref, ma_ref, mb_ref, bias_ref):
    s_len = q_ref.shape[0]
    B = DIL_BLK
    scale = HEAD_DIM ** -0.5
    lane = lax.broadcasted_iota(jnp.int32, (B, LANES), 1)
    head_a = lane < HEAD_DIM
    qi = lax.broadcasted_iota(jnp.int32, (2 * B, 2 * B), 0) & (B - 1)
    ki = lax.broadcasted_iota(jnp.int32, (2 * B, 2 * B), 1)
    band = (ki >= qi) & (ki <= qi + B)
    bias_ref[0] = jnp.where(band & (ki >= B), 0.0, NEG)
    bias_ref[1] = jnp.where(band, 0.0, NEG)
    ones = jnp.ones((2 * B, LANES), BF16)

    def comb(x):
        return jnp.where(head_a, x[:B], x[B:])

    def rows(ref, start, d):
        if d == 1:
            return ref[pl.ds(start, B), :]
        return ref[pl.ds(start, B, stride=d), :]

    def store_rows(ref, start, d, val):
        if d == 1:
            ref[pl.ds(start, B), :] = val
        else:
            ref[pl.ds(start, B, stride=d), :] = val

    order = sorted(DILATED_BRANCHES, key=lambda wd: -wd[1])
    assert order[-1][1] == 1
    for bi, (w, d) in enumerate(order):
        assert w // d == B
        first_branch = bi == 0
        last_branch = bi == len(order) - 1
        per_res = s_len // (B * d)
        G = min(DIL_GROUP, per_res)
        assert per_res % G == 0

        def body(t, carry, d=d, first_branch=first_branch, last_branch=last_branch, per_res=per_res, G=G):
            r = t // (per_res // G)
            n0 = (t % (per_res // G)) * G
            starts = [(n0 + j) * (B * d) + r for j in range(G)]
            prev0 = jnp.maximum(n0 - 1, 0) * (B * d) + r
            if d == 1:
                starts = [pl.multiple_of(st, B) for st in starts]
                prev0 = pl.multiple_of(prev0, B)
            kb = [rows(k_ref, st, d).astype(BF16) for st in [prev0] + starts]
            vb = [rows(v_ref, st, d).astype(BF16) for st in [prev0] + starts]
            for j in range(G):
                start = starts[j]
                q = rows(q_ref, start, d) * scale
                qs = jnp.concatenate([jnp.where(head_a, q, 0.0), jnp.where(head_a, 0.0, q)], axis=0).astype(BF16)
                kk = jnp.concatenate([kb[j], kb[j + 1]], axis=0)
                s = lax.dot_general(qs, kk, (((1,), (1,)), ((), ())), preferred_element_type=F32)
                s = s + (bias_ref[jnp.minimum(n0, 1)] if j == 0 else bias_ref[1])
                m_row = jnp.broadcast_to(jnp.max(s, axis=-1, keepdims=True), (2 * B, LANES))
                if first_branch:
                    m_new = m_row
                else:
                    m_old = jnp.concatenate([rows(ma_ref, start, d), rows(mb_ref, start, d)], axis=0)
                    m_new = jnp.maximum(m_old, m_row)
                p = jnp.exp(s - jnp.concatenate([m_new, m_new], axis=1)).astype(BF16)
                pv = _dot(p, jnp.concatenate([jnp.concatenate([vb[j], vb[j + 1]], axis=0), ones], axis=1))
                l_new = comb(pv[:, LANES:])
                pv = comb(pv[:, :LANES])
                if first_branch:
                    acc, l_b = pv, l_new
                else:
                    alpha = comb(jnp.exp(m_old - m_new))
                    acc = rows(acc_ref, start, d) * alpha + pv
                    l_b = rows(l_ref, start, d) * alpha + l_new
                if last_branch:
                    o_ref[pl.ds(start, B), :] = (acc / l_b).astype(o_ref.dtype)
                else:
                    store_rows(acc_ref, start, d, acc)
                    store_rows(l_ref, start, d, l_b)
                    store_rows(ma_ref, start, d, m_new[:B])
                    store_rows(mb_ref, start, d, m_new[B:])
            return carry

        lax.fori_loop(0, s_len // (B * G), body, 0)


def dilated_prompt(proj):
    s = proj.shape[0]
    n_pairs = D_ATTN // LANES
    spec = lambda off: pl.BlockSpec((s, LANES), lambda h: (0, off // LANES + h))
    return pl.pallas_call(
        _dil_kernel, grid=(n_pairs,),
        in_specs=[spec(COL_Q), spec(COL_K), spec(COL_V)],
        out_specs=pl.BlockSpec((s, LANES), lambda h: (0, h)),
        out_shape=jax.ShapeDtypeStruct((s, D_ATTN), BF16),
        scratch_shapes=[pltpu.VMEM((s, LANES), F32)] * 4 + [pltpu.VMEM((2, 2 * DIL_BLK, 2 * DIL_BLK), F32)],
        compiler_params=_params(("parallel",)), name="dilated_prompt",
    )(proj, proj, proj)


SSD_SAMPLE_BB = 8


def _head_expand(x, seg_t):
    hi, mid, lo = _split3(x)
    return _dot(hi, seg_t) + _dot(mid, seg_t) + _dot(lo, seg_t)


def _ssd_sample_kernel(z_ref, xs_ref, bc_ref, dt_ref, sc_ref, st_ref, cw_ref, cb_ref, dtb_ref, alog_ref,
                       dskip_ref, ng_ref, segt_ref,
                       y_ref, ho_ref, convo_ref,
                       xs_s, t2_s, dec_s, b_s, c_s, col_s, yoff_s):
    step = pl.program_id(0)
    nb = xs_ref.shape[0]
    BB = SSD_SAMPLE_BB
    GW = 4 * HEAD_DIM
    HP = SSM_HEADS * HEAD_DIM

    @pl.when(step == 0)
    def _():
        xnew = jnp.concatenate([xs_ref[...], bc_ref[...]], axis=1)
        rows = [sc_ref[:, j * XBC_DIM:(j + 1) * XBC_DIM] for j in range(CONV_W - 1)] + [xnew]
        taps = [rows[j] * cw_ref[j:j + 1, :] for j in range(CONV_W)]
        act = _silu(cb_ref[...] + (((taps[0] + taps[1]) + taps[2]) + taps[3]))
        for j in range(CONV_W - 1):
            convo_ref[:, j * XBC_DIM:(j + 1) * XBC_DIM] = rows[j + 1]
        xs = act[:, 0:D_SSM]
        bm = act[:, D_SSM:D_SSM + SSM_GROUPS * SSM_STATE]
        cm = act[:, D_SSM + SSM_GROUPS * SSM_STATE:]
        dt = _softplus(dt_ref[...] + dtb_ref[...])
        dec = jnp.exp(dt * (-jnp.exp(alog_ref[...])))
        seg_t = segt_ref[...]
        u = _head_expand(dt, seg_t) * xs
        dec_e = _head_expand(dec, seg_t)
        xs_s[...] = xs
        dec_s[...] = dec_e
        b_s[...] = bm
        c_s[...] = cm
        for g in range(SSM_GROUPS):
            sl = slice(g * SSM_STATE, (g + 1) * SSM_STATE)
            bc = jnp.sum(bm[:, sl] * cm[:, sl], axis=-1, keepdims=True)
            t2_s[:, g * GW:(g + 1) * GW] = u[:, g * GW:(g + 1) * GW] * bc
        for j in range(HP // LANES):
            for src, base in ((u, 0), (dec_e, HP)):
                t = src[:, j * LANES:(j + 1) * LANES].T
                for k, part in enumerate(_split3(t)):
                    col_s[base + j * LANES:base + (j + 1) * LANES, k * nb:(k + 1) * nb] = part

    def per_seq(i, carry):
        b = step * BB + i
        ridx = lax.broadcasted_iota(jnp.int32, (3 * nb, SSM_STATE), 0)
        sel = ((ridx == b) | (ridx == b + nb) | (ridx == b + 2 * nb)).astype(BF16)
        cols = _dot(col_s[...], sel)
        h0 = st_ref[i].reshape(HP, SSM_STATE)
        brow = b_s[pl.ds(b, 1), :]
        c8 = c_s[pl.ds(pl.multiple_of(step * BB, BB), BB), :]
        pick = lax.broadcasted_iota(jnp.int32, (BB, GW), 0) == i
        for g in range(SSM_GROUPS):
            rs = slice(g * GW, (g + 1) * GW)
            ls = slice(g * SSM_STATE, (g + 1) * SSM_STATE)
            h0g = h0[rs]
            hn = cols[HP + g * GW:HP + (g + 1) * GW] * h0g + cols[rs] * brow[:, ls]
            ho_ref[i, 4 * g:4 * g + 4] = hn.reshape(4, HEAD_DIM, SSM_STATE)
            yo = lax.dot_general(c8[:, ls].astype(BF16), h0g.astype(BF16), (((1,), (1,)), ((), ())),
                                 preferred_element_type=F32)
            yoff_s[pl.ds(b, 1), rs] = jnp.sum(jnp.where(pick, yo, 0.0), axis=0, keepdims=True)
        return carry

    lax.fori_loop(0, BB, per_seq, 0)

    @pl.when(step == pl.num_programs(0) - 1)
    def _():
        xs = xs_s[...]
        y = dec_s[...] * yoff_s[...] + t2_s[...] + dskip_ref[...] * xs
        y = y * _silu(z_ref[...])
        for g in range(SSM_GROUPS):
            yg = y[:, g * GW:(g + 1) * GW]
            yg = yg * lax.rsqrt(jnp.mean(yg * yg, axis=-1, keepdims=True) + RMS_EPS)
            y_ref[:, g * GW:(g + 1) * GW] = (yg * ng_ref[:, g * GW:(g + 1) * GW]).astype(y_ref.dtype)


def _seg_ones(n_heads, width):
    c = jnp.arange(n_heads * width)[:, None] // width
    seg = (c == jnp.arange(LANES)[None, :]).astype(BF16)
    return seg, seg.T


def ssd_sample(proj, dt_raw, state_conv, state_ssm, conv_w, conv_b, dt_bias, a_log, d_skip, norm_g):
    nb = proj.shape[0]
    BB = SSD_SAMPLE_BB
    HP = SSM_HEADS * HEAD_DIM
    col = lambda off: off // 1024
    const = lambda shape: pl.BlockSpec(shape, lambda s: (0,) * len(shape))
    _, seg_t = _seg_ones(SSM_HEADS, HEAD_DIM)
    st_spec = pl.BlockSpec((BB, SSM_HEADS, HEAD_DIM, SSM_STATE), lambda s: (s, 0, 0, 0))
    y, h_new, conv_new = pl.pallas_call(
        _ssd_sample_kernel, grid=(nb // BB,),
        in_specs=[
            pl.BlockSpec((nb, D_SSM), lambda s: (0, col(COL_Z))),
            pl.BlockSpec((nb, D_SSM), lambda s: (0, col(COL_XBC))),
            pl.BlockSpec((nb, XBC_DIM - D_SSM), lambda s: (0, col(COL_XBC + D_SSM))),
            const((nb, LANES)), const((nb, (CONV_W - 1) * XBC_DIM)), st_spec,
            const((CONV_W, XBC_DIM)), const((1, XBC_DIM)), const((1, LANES)), const((1, LANES)),
            const((1, D_SSM)), const((1, D_SSM)), const((LANES, HP)),
        ],
        out_specs=[const((nb, D_SSM)), st_spec, const((nb, (CONV_W - 1) * XBC_DIM))],
        out_shape=[
            jax.ShapeDtypeStruct((nb, D_SSM), BF16),
            jax.ShapeDtypeStruct(state_ssm.shape, F32),
            jax.ShapeDtypeStruct((nb, (CONV_W - 1) * XBC_DIM), F32),
        ],
        scratch_shapes=[
            pltpu.VMEM((nb, HP), F32), pltpu.VMEM((nb, HP), F32), pltpu.VMEM((nb, HP), F32),
            pltpu.VMEM((nb, SSM_GROUPS * SSM_STATE), F32), pltpu.VMEM((nb, SSM_GROUPS * SSM_STATE), F32),
            pltpu.VMEM((2 * HP, 3 * nb), BF16), pltpu.VMEM((nb, HP), F32),
        ],
        compiler_params=_params(("arbitrary",)), name="ssd_sample",
    )(proj, proj, proj, dt_raw, state_conv.reshape(nb, (CONV_W - 1) * XBC_DIM), state_ssm, conv_w,
      conv_b.reshape(1, XBC_DIM), _pad_lanes(dt_bias), _pad_lanes(a_log),
      jnp.repeat(d_skip, HEAD_DIM).reshape(1, D_SSM), norm_g.reshape(1, D_SSM), seg_t)
    return y, h_new, conv_new.reshape(nb, CONV_W - 1, XBC_DIM)


DIL_SAMPLE_HG = 8


def _dil_sample_kernel(n_branch, q_ref, kn_ref, vn_ref, cnt_ref, kt_ref, vt_ref, o_ref):
    q = q_ref[...] * (HEAD_DIM ** -0.5)
    cnt = cnt_ref[...][None]
    s = jnp.einsum("hqd,hdw->hqw", q.astype(BF16), kt_ref[...].astype(BF16), preferred_element_type=F32)
    s = jnp.where(cnt > 0.0, s, NEG)
    s_new = jnp.sum(q * kn_ref[...], axis=-1, keepdims=True)
    m = jnp.maximum(jnp.max(s, axis=-1, keepdims=True), s_new)
    p = cnt * jnp.exp(s - m)
    p_new = n_branch * jnp.exp(s_new - m)
    l = jnp.sum(p, axis=-1, keepdims=True) + p_new
    o = jnp.einsum("hqw,hdw->hqd", p.astype(BF16), vt_ref[...].astype(BF16), preferred_element_type=F32)
    o_ref[...] = (o + p_new * vn_ref[...]) / l


def dilated_sample(q, kn, vn, cache_k, cache_v):
    nb, w_buf = cache_k.shape[0], cache_k.shape[1]
    hg = DIL_SAMPLE_HG
    pos = jnp.arange(w_buf)
    cnt = jnp.zeros((w_buf,), F32)
    for w, d in DILATED_BRANCHES:
        cnt = cnt + ((pos >= w_buf - w) & ((w_buf - pos) % d == 0)).astype(F32)
    rep = lambda t: jnp.broadcast_to(t[:, :, None, :], (nb, ATTN_HEADS, SUBLANES, HEAD_DIM))
    small = pl.BlockSpec((None, hg, SUBLANES, HEAD_DIM), lambda b, g: (b, g, 0, 0))
    big = pl.BlockSpec((None, hg, HEAD_DIM, w_buf), lambda b, g: (b, g, 0, 0))
    out = pl.pallas_call(
        functools.partial(_dil_sample_kernel, float(len(DILATED_BRANCHES))),
        grid=(nb, ATTN_HEADS // hg),
        in_specs=[small, small, small, pl.BlockSpec((1, w_buf), lambda b, g: (0, 0)), big, big],
        out_specs=small,
        out_shape=jax.ShapeDtypeStruct((nb, ATTN_HEADS, SUBLANES, HEAD_DIM), F32),
        compiler_params=_params(("parallel", "parallel")), name="dilated_sample",
    )(rep(q), rep(kn), rep(vn), cnt.reshape(1, w_buf),
      jnp.transpose(cache_k, (0, 2, 3, 1)), jnp.transpose(cache_v, (0, 2, 3, 1)))
    return out[:, :, 0, :].reshape(nb, D_ATTN)


MEM_SAMPLE_BB = 4


def _mem_sample_kernel(q_ref, k_ref, v_ref, o_ref):
    shape = (q_ref.shape[1], k_ref.shape[1])
    row = lax.broadcasted_iota(jnp.int32, shape, 0)
    col = lax.broadcasted_iota(jnp.int32, shape, 1)
    own = (col & (MEM_HEADS - 1)) == (row & (MEM_HEADS - 1))
    for i in range(q_ref.shape[0]):
        q = (q_ref[i] * (MEM_HEAD_DIM ** -0.5)).astype(BF16)
        s = lax.dot_general(q, k_ref[i].astype(BF16), (((1,), (1,)), ((), ())), preferred_element_type=F32)
        s = jnp.where(own, s, NEG)
        m = jnp.max(s, axis=-1, keepdims=True)
        p = jnp.where(own, jnp.exp(s - m), 0.0)
        l = jnp.sum(p, axis=-1, keepdims=True)
        o_ref[i] = _dot((p / l).astype(BF16), v_ref[i].astype(BF16))


def mem_attention_sample(q, cache_mk, cache_mv):
    nb = q.shape[0]
    assert MEM_HEADS & (MEM_HEADS - 1) == 0 and SUBLANES % MEM_HEADS == 0
    q8 = jnp.tile(q.reshape(nb, MEM_HEADS, MEM_HEAD_DIM), (1, SUBLANES // MEM_HEADS, 1))
    rows = MEM_LEN * MEM_HEADS
    bb = MEM_SAMPLE_BB
    small = pl.BlockSpec((bb, SUBLANES, MEM_HEAD_DIM), lambda b: (b, 0, 0))
    big = pl.BlockSpec((bb, rows, MEM_HEAD_DIM), lambda b: (b, 0, 0))
    out = pl.pallas_call(
        _mem_sample_kernel, grid=(nb // bb,), in_specs=[small, big, big], out_specs=small,
        out_shape=jax.ShapeDtypeStruct((nb, SUBLANES, MEM_HEAD_DIM), F32),
        compiler_params=_params(("parallel",)), name="mem_attention_sample",
    )(q8, cache_mk.reshape(nb, rows, MEM_HEAD_DIM), cache_mv.reshape(nb, rows, MEM_HEAD_DIM))
    return out[:, :MEM_HEADS].reshape(nb, D_MEM)


def _layer_tail(x, y_ssm, y_att, mem_attend, lw, gf, *, tm, tf):
    half = D_SSM
    x1 = matmul_residual([(y_ssm, lw["w_out"], 0), (y_att, lw["w_out"], 1)], x, tm=tm, tn=1024)
    assert y_ssm.shape[1] == half and y_att.shape[1] == half
    hq = rms_matmul(x1, lw["ln_mem_g"], lw["w_mq"], tm=tm, tn=D_MEM)
    x2 = matmul_residual([(mem_attend(hq), lw["w_mo"], 0)], x1, tm=tm, tn=1024)
    return ffn_final(x2, lw["ln2_g"], lw["w_gate"], lw["w_up"], lw["w_down"], gf, tm=min(tm, 512), tf=tf)


def kernel(x_prompt, x_sample, cache_win_k, cache_win_v, state_conv, state_ssm, cache_mem_k, cache_mem_v, mem_prompt, ln1_g, w_in, conv_w, conv_b, dt_bias, a_log, d_skip, ssm_norm_g, w_out, ln_mem_g, mem_norm_g, w_mq, w_mk, w_mv, w_mo, ln2_g, w_gate, w_up, w_down, ln_f_g):
    bp, s, d = x_prompt.shape
    nb, ts, _ = x_sample.shape
    depth = w_in.shape[0]
    assert bp == 1 and ts == 1 and depth == 1
    w_keep = min(DILATED_BRANCHES[-1][0], s)
    l = 0
    dt_lo = D_SSM + XBC_DIM
    dt_hi = dt_lo + SSM_HEADS
    w_main = jnp.concatenate([w_in[l][:, :dt_lo], w_in[l][:, dt_hi:]], axis=1).astype(BF16)
    w_dt = jnp.concatenate(_split3(jnp.pad(w_in[l][:, dt_lo:dt_hi], ((0, 0), (0, LANES - SSM_HEADS))))[:2], axis=1)
    lw = dict(w_out=w_out[l].astype(BF16), ln_mem_g=ln_mem_g[l], w_mq=w_mq[l].astype(BF16),
              w_mo=w_mo[l].astype(BF16), ln2_g=ln2_g[l], w_gate=w_gate[l].astype(BF16),
              w_up=w_up[l].astype(BF16), w_down=w_down[l].astype(BF16))
    w_mkv = jnp.concatenate([w_mk[l], w_mv[l]], axis=1).astype(BF16)
    ssm_w = (conv_w[l], conv_b[l], dt_bias[l], a_log[l], d_skip[l], ssm_norm_g[l])

    xp = x_prompt.reshape(s, d)
    mkv = rms_matmul(mem_prompt.reshape(MEM_LEN, d), mem_norm_g[l], w_mkv, tm=MEM_LEN, tn=2 * D_MEM)
    proj, dt_raw = rms_matmul(xp, ln1_g[l], w_main, tm=1024, tn=1024, wdt=w_dt)
    y_ssm, h_p, conv_p = ssd_prompt(proj, dt_raw, *ssm_w)
    y_att = dilated_prompt(proj)
    x2 = mix_mem_fused(xp, y_ssm, y_att, lw["w_out"], lw["ln_mem_g"], lw["w_mq"], mkv, lw["w_mo"], tm=512)
    y_prompt = ffn_final(x2, lw["ln2_g"], lw["w_gate"], lw["w_up"], lw["w_down"], ln_f_g, tm=512, tf=512)

    xs = x_sample.reshape(nb, d)
    proj_s, dt_s = rms_matmul(xs, ln1_g[l], w_main, tm=nb, tn=1024, wdt=w_dt)
    y_ssm_s, h_s, conv_s = ssd_sample(proj_s, dt_s, state_conv[l], state_ssm[l], *ssm_w)
    heads = lambda t: t.reshape(t.shape[0], ATTN_HEADS, HEAD_DIM)
    q_s, k_s, v_s = (heads(proj_s[:, c:c + D_ATTN]) for c in (COL_Q, COL_K, COL_V))
    y_att_s = dilated_sample(q_s, k_s, v_s, cache_win_k[l], cache_win_v[l])
    y_sample = _layer_tail(xs, y_ssm_s, y_att_s,
                           lambda hq: mem_attention_sample(hq, cache_mem_k[l], cache_mem_v[l]), lw, ln_f_g,
                           tm=nb, tf=512)

    return (
        y_prompt.reshape(bp, s, d),
        y_sample.reshape(nb, ts, d),
        heads(proj[s - w_keep:, COL_K:COL_V])[None, None],
        heads(proj[s - w_keep:, COL_V:])[None, None],
        conv_p[None, None],
        h_p[None, None],
        mkv[:, :D_MEM].reshape(1, 1, MEM_LEN, MEM_HEADS, MEM_HEAD_DIM),
        mkv[:, D_MEM:].reshape(1, 1, MEM_LEN, MEM_HEADS, MEM_HEAD_DIM),
        k_s[None, :, None],
        v_s[None, :, None],
        conv_s[None],
        h_s[None],
    )
```

```python
import functools

import jax
import jax.numpy as jnp
from jax import lax
from jax.experimental import pallas as pl
from jax.experimental.pallas import tpu as pltpu

F32 = jnp.float32
BF16 = jnp.bfloat16

D_MODEL = 2048
D_SSM = 1024
D_ATTN = 1024
HEAD_DIM = 64
SSM_HEADS = 16
SSM_GROUPS = 4
SSM_STATE = 128
CONV_W = 4
SSD_CHUNK = 128
XBC_DIM = D_SSM + 2 * SSM_GROUPS * SSM_STATE
ATTN_HEADS = 16
DILATED_BRANCHES = ((128, 1), (512, 4), (2048, 16))
MEM_LEN = 256
MEM_HEADS = 4
MEM_HEAD_DIM = 128
D_MEM = MEM_HEADS * MEM_HEAD_DIM
RMS_EPS = 1e-5
NEG = -1e30

LANES = 128
SUBLANES = 8
VMEM_BYTES_V7X = 64 * 1024 * 1024
VMEM_LIMIT = 56 * 1024 * 1024

COL_Z = 0
COL_XBC = D_SSM
COL_Q = D_SSM + XBC_DIM
COL_K = COL_Q + D_ATTN
COL_V = COL_K + D_ATTN
D_PROJ = COL_V + D_ATTN


def _params(sem):
    return pltpu.CompilerParams(dimension_semantics=sem, vmem_limit_bytes=VMEM_LIMIT)


def _rms(x, g):
    ms = jnp.mean(x * x, axis=-1, keepdims=True)
    return x * lax.rsqrt(ms + RMS_EPS) * g


def _silu(x):
    return x * (0.5 * jnp.tanh(0.5 * x) + 0.5)


def _softplus(x):
    return jnp.maximum(x, 0.0) + jnp.log1p(jnp.exp(-jnp.abs(x)))


def _split3(x):
    hi = x.astype(BF16)
    r1 = x - hi.astype(F32)
    mid = r1.astype(BF16)
    lo = (r1 - mid.astype(F32)).astype(BF16)
    return hi, mid, lo


def _dot(a, b):
    return jnp.dot(a, b, preferred_element_type=F32)


def _rms_matmul_kernel(x_ref, g_ref, w_ref, o_ref, hn_ref):
    @pl.when(pl.program_id(1) == 0)
    def _():
        hn_ref[...] = _rms(x_ref[...], g_ref[...]).astype(BF16)

    o_ref[...] = _dot(hn_ref[...], w_ref[...])


def _rms_matmul_dt_kernel(x_ref, g_ref, w_ref, wdt_ref, o_ref, dt_ref, hn_ref):
    @pl.when(pl.program_id(1) == 0)
    def _():
        hn = _rms(x_ref[...], g_ref[...])
        hi = hn.astype(BF16)
        lo = (hn - hi.astype(F32)).astype(BF16)
        hn_ref[...] = hi
        both = _dot(hi, wdt_ref[...]) + _dot(lo, wdt_ref[...])
        dt_ref[...] = both[:, :LANES] + both[:, LANES:]

    o_ref[...] = _dot(hn_ref[...], w_ref[...])


def rms_matmul(x, g, w, *, tm, tn, wdt=None):
    m, k = x.shape
    n = w.shape[1]
    grid = (m // tm, n // tn)
    in_specs = [
        pl.BlockSpec((tm, k), lambda i, j: (i, 0)),
        pl.BlockSpec((1, k), lambda i, j: (0, 0)),
        pl.BlockSpec((k, tn), lambda i, j: (0, j)),
    ]
    o_spec = pl.BlockSpec((tm, tn), lambda i, j: (i, j))
    scratch = [pltpu.VMEM((tm, k), BF16)]
    if wdt is None:
        return pl.pallas_call(
            _rms_matmul_kernel, grid=grid, in_specs=in_specs, out_specs=o_spec,
            out_shape=jax.ShapeDtypeStruct((m, n), F32), scratch_shapes=scratch,
            compiler_params=_params(("parallel", "arbitrary")), name="rms_matmul",
        )(x, g.reshape(1, k), w)
    in_specs.append(pl.BlockSpec((k, 2 * LANES), lambda i, j: (0, 0)))
    return pl.pallas_call(
        _rms_matmul_dt_kernel, grid=grid, in_specs=in_specs,
        out_specs=[o_spec, pl.BlockSpec((tm, LANES), lambda i, j: (i, 0))],
        out_shape=[jax.ShapeDtypeStruct((m, n), F32), jax.ShapeDtypeStruct((m, LANES), F32)],
        scratch_shapes=scratch,
        compiler_params=_params(("parallel", "arbitrary")), name="rms_matmul_dt",
    )(x, g.reshape(1, k), w, wdt)


def _mm_res_kernel(n_pairs, *refs):
    res_ref = refs[2 * n_pairs]
    o_ref = refs[2 * n_pairs + 1]
    acc = res_ref[...]
    for i in range(n_pairs):
        acc = acc + _dot(refs[2 * i][...].astype(BF16), refs[2 * i + 1][...])
    o_ref[...] = acc


def matmul_residual(pairs, res, *, tm, tn):
    m, n = res.shape
    grid = (m // tm, n // tn)
    in_specs, args = [], []
    for a, w, kblk in pairs:
        k = a.shape[1]
        in_specs.append(pl.BlockSpec((tm, k), lambda i, j: (i, 0)))
        in_specs.append(pl.BlockSpec((k, tn), lambda i, j, kblk=kblk: (kblk, j)))
        args += [a, w]
    in_specs.append(pl.BlockSpec((tm, tn), lambda i, j: (i, j)))
    args.append(res)
    return pl.pallas_call(
        functools.partial(_mm_res_kernel, len(pairs)), grid=grid, in_specs=in_specs,
        out_specs=pl.BlockSpec((tm, tn), lambda i, j: (i, j)),
        out_shape=jax.ShapeDtypeStruct((m, n), F32),
        compiler_params=_params(("parallel", "parallel")), name="matmul_residual",
    )(*args)


def _ffn_kernel(x_ref, g2_ref, wg_ref, wu_ref, wd_ref, gf_ref, o_ref, hn_ref, acc_ref):
    f = pl.program_id(1)

    @pl.when(f == 0)
    def _():
        hn_ref[...] = _rms(x_ref[...], g2_ref[...]).astype(BF16)
        acc_ref[...] = jnp.zeros_like(acc_ref)

    hn = hn_ref[...]
    act = _silu(_dot(hn, wg_ref[...])) * _dot(hn, wu_ref[...])
    acc_ref[...] += _dot(act.astype(BF16), wd_ref[...])

    @pl.when(f == pl.num_programs(1) - 1)
    def _():
        o_ref[...] = _rms(x_ref[...] + acc_ref[...], gf_ref[...])


def ffn_final(x, g2, wg, wu, wd, gf, *, tm, tf):
    m, d = x.shape
    dff = wg.shape[1]
    grid = (m // tm, dff // tf)
    return pl.pallas_call(
        _ffn_kernel, grid=grid,
        in_specs=[
            pl.BlockSpec((tm, d), lambda i, f: (i, 0)),
            pl.BlockSpec((1, d), lambda i, f: (0, 0)),
            pl.BlockSpec((d, tf), lambda i, f: (0, f)),
            pl.BlockSpec((d, tf), lambda i, f: (0, f)),
            pl.BlockSpec((tf, d), lambda i, f: (f, 0)),
            pl.BlockSpec((1, d), lambda i, f: (0, 0)),
        ],
        out_specs=pl.BlockSpec((tm, d), lambda i, f: (i, 0)),
        out_shape=jax.ShapeDtypeStruct((m, d), F32),
        scratch_shapes=[pltpu.VMEM((tm, d), BF16), pltpu.VMEM((tm, d), F32)],
        compiler_params=_params(("parallel", "arbitrary")), name="ffn_final",
    )(x, g2.reshape(1, d), wg, wu, wd, gf.reshape(1, d))


def _mix_mem_kernel(x_ref, ys_ref, ya_ref, wo_ref, g_ref, wq_ref, mkv_ref, wm_ref, o_ref):
    half = ys_ref.shape[1]
    x1 = x_ref[...] + _dot(ys_ref[...], wo_ref[0:half, :]) + _dot(ya_ref[...], wo_ref[half:, :])
    hq = _dot(_rms(x1, g_ref[...]).astype(BF16), wq_ref[...])
    scale = MEM_HEAD_DIM ** -0.5
    outs = []
    for h in range(MEM_HEADS):
        sl = slice(h * MEM_HEAD_DIM, (h + 1) * MEM_HEAD_DIM)
        k = mkv_ref[:, sl].astype(BF16)
        v = mkv_ref[:, D_MEM + h * MEM_HEAD_DIM:D_MEM + (h + 1) * MEM_HEAD_DIM].astype(BF16)
        s = lax.dot_general(hq[:, sl].astype(BF16), k, (((1,), (1,)), ((), ())), preferred_element_type=F32) * scale
        p = jnp.exp(s - jnp.max(s, axis=-1, keepdims=True))
        p = p / jnp.sum(p, axis=-1, keepdims=True)
        outs.append(_dot(p.astype(BF16), v).astype(BF16))
    o_ref[...] = x1 + _dot(jnp.concatenate(outs, axis=1), wm_ref[...])


def mix_mem_fused(x, y_ssm, y_att, w_out, ln_mem_g, w_mq, mkv, w_mo, *, tm):
    m, d = x.shape
    half = y_ssm.shape[1]
    row = lambda width: pl.BlockSpec((tm, width), lambda i: (i, 0))
    whole = lambda a: pl.BlockSpec(a.shape, lambda i: (0, 0), pipeline_mode=pl.Buffered(1))
    g = ln_mem_g.reshape(1, d)
    return pl.pallas_call(
        _mix_mem_kernel, grid=(m // tm,),
        in_specs=[row(d), row(half), row(half), whole(w_out), whole(g), whole(w_mq), whole(mkv), whole(w_mo)],
        out_specs=row(d),
        out_shape=jax.ShapeDtypeStruct((m, d), F32),
        compiler_params=_params(("parallel",)), name="mix_mem_fused",
    )(x, y_ssm, y_att, w_out, g, w_mq, mkv, w_mo)


def _head_cols(x, g, width):
    rows = x.shape[0]
    parts = [jnp.broadcast_to(x[:, 4 * g + i:4 * g + i + 1], (rows, width)) for i in range(4)]
    return jnp.concatenate(parts, axis=1)


def _ssd_prompt_kernel(z_ref, xs_ref, bc_ref, dt_ref, cw_ref, cb_ref, dtb_ref, alog_ref, dskip_ref, ng_ref,
                       y_ref, hfin_ref, convo_ref, cbuf, st_ref):
    c = pl.program_id(0)
    L = SSD_CHUNK
    P = HEAD_DIM
    GW = 4 * P

    @pl.when(c == 0)
    def _():
        cbuf[0:SUBLANES, :] = jnp.zeros((SUBLANES, XBC_DIM), F32)
        st_ref[...] = jnp.zeros_like(st_ref)

    cbuf[SUBLANES:SUBLANES + L, 0:D_SSM] = xs_ref[...]
    cbuf[SUBLANES:SUBLANES + L, D_SSM:XBC_DIM] = bc_ref[...]
    up = cbuf[...]
    taps = [(up if j == CONV_W - 1 else pltpu.roll(up, CONV_W - 1 - j, 0))[SUBLANES:SUBLANES + L] * cw_ref[j:j + 1, :]
            for j in range(CONV_W)]
    conv = cb_ref[...] + (((taps[0] + taps[1]) + taps[2]) + taps[3])

    @pl.when(c == pl.num_programs(0) - 1)
    def _():
        convo_ref[...] = cbuf[pl.ds(L + SUBLANES - (CONV_W - 1), CONV_W - 1), :]

    cbuf[0:SUBLANES, :] = cbuf[L:L + SUBLANES, :]

    act = _silu(conv)
    xs = act[:, 0:D_SSM]
    dt = _softplus(dt_ref[...] + dtb_ref[...])
    a = dt * (-jnp.exp(alog_ref[...]))
    ri = lax.broadcasted_iota(jnp.int32, (L, L), 0)
    ci = lax.broadcasted_iota(jnp.int32, (L, L), 1)
    causal = ri >= ci
    acum = jnp.dot(causal.astype(F32), a, precision=lax.Precision.HIGHEST, preferred_element_type=F32)
    acum_t = acum.T
    dt_t = dt.T
    last = acum[L - 1:L, :]
    ea = jnp.exp(acum)
    wend = jnp.exp(last - acum) * dt
    cdec = jnp.exp(last)
    lane = lax.broadcasted_iota(jnp.int32, (L, GW), 1)
    head_lanes = [(lane >= i * P) & (lane < (i + 1) * P) for i in range(4)]

    ys = []
    for g in range(SSM_GROUPS):
        bg = act[:, D_SSM + g * SSM_STATE:D_SSM + (g + 1) * SSM_STATE]
        cg = act[:, D_SSM + SSM_GROUPS * SSM_STATE + g * SSM_STATE:
                 D_SSM + SSM_GROUPS * SSM_STATE + (g + 1) * SSM_STATE]
        bgb = bg.astype(BF16)
        cgb = cg.astype(BF16)
        cb = lax.dot_general(cgb, bgb, (((1,), (1,)), ((), ())), preferred_element_type=F32)
        xg = xs[:, g * GW:(g + 1) * GW]
        yd = jnp.zeros((L, GW), F32)
        for i in range(4):
            h = 4 * g + i
            seg = acum[:, h:h + 1] - acum_t[h:h + 1, :]
            decay = jnp.exp(jnp.where(causal, seg, NEG))
            w = (cb * decay * dt_t[h:h + 1, :]).astype(BF16)
            xm = jnp.where(head_lanes[i], xg, 0.0).astype(BF16)
            yd = yd + _dot(w, xm)
        st = st_ref[g]
        yoff = _dot(cgb, st.astype(BF16)) * _head_cols(ea, g, P)
        ys.append(yd + yoff)
        xw = (xg * _head_cols(wend, g, P)).astype(BF16)
        new = _dot(bg.T.astype(BF16), xw)
        cd = jnp.concatenate([jnp.broadcast_to(cdec[:, 4 * g + i:4 * g + i + 1], (1, P)) for i in range(4)], axis=1)
        st_ref[g] = st * cd + new

    y = jnp.concatenate(ys, axis=1) + dskip_ref[...] * xs
    y = y * _silu(z_ref[...])
    outs = []
    for g in range(SSM_GROUPS):
        yg = y[:, g * GW:(g + 1) * GW]
        outs.append(yg * lax.rsqrt(jnp.mean(yg * yg, axis=-1, keepdims=True) + RMS_EPS))
    y_ref[...] = (jnp.concatenate(outs, axis=1) * ng_ref[...]).astype(y_ref.dtype)

    @pl.when(c == pl.num_programs(0) - 1)
    def _():
        for g in range(SSM_GROUPS):
            stg = st_ref[g]
            for i in range(4):
                hfin_ref[4 * g + i] = stg[:, i * P:(i + 1) * P].T


def _pad_lanes(v):
    return jnp.pad(v.astype(F32), (0, LANES - v.shape[0])).reshape(1, LANES)


def ssd_prompt(proj, dt_raw, conv_w, conv_b, dt_bias, a_log, d_skip, norm_g):
    s = proj.shape[0]
    L = SSD_CHUNK
    nblk = D_SSM // 1024
    col = lambda off: off // 1024
    const = lambda shape: pl.BlockSpec(shape, lambda c: (0,) * len(shape))
    return pl.pallas_call(
        _ssd_prompt_kernel, grid=(s // L,),
        in_specs=[
            pl.BlockSpec((L, D_SSM), lambda c: (c, col(COL_Z))),
            pl.BlockSpec((L, D_SSM), lambda c: (c, col(COL_XBC))),
            pl.BlockSpec((L, XBC_DIM - D_SSM), lambda c: (c, col(COL_XBC + D_SSM))),
            pl.BlockSpec((L, LANES), lambda c: (c, 0)),
            const((CONV_W, XBC_DIM)), const((1, XBC_DIM)), const((1, LANES)), const((1, LANES)),
            const((1, D_SSM)), const((1, D_SSM)),
        ],
        out_specs=[
            pl.BlockSpec((L, D_SSM), lambda c: (c, 0)),
            const((SSM_HEADS, HEAD_DIM, SSM_STATE)),
            const((CONV_W - 1, XBC_DIM)),
        ],
        out_shape=[
            jax.ShapeDtypeStruct((s, D_SSM), BF16),
            jax.ShapeDtypeStruct((SSM_HEADS, HEAD_DIM, SSM_STATE), F32),
            jax.ShapeDtypeStruct((CONV_W - 1, XBC_DIM), F32),
        ],
        scratch_shapes=[
            pltpu.VMEM((L + SUBLANES, XBC_DIM), F32),
            pltpu.VMEM((SSM_GROUPS, SSM_STATE, 4 * HEAD_DIM), F32),
        ],
        compiler_params=_params(("arbitrary",)), name="ssd_prompt",
    )(proj, proj, proj, dt_raw, conv_w, conv_b.reshape(1, XBC_DIM), _pad_lanes(dt_bias), _pad_lanes(a_log),
      jnp.repeat(d_skip, HEAD_DIM).reshape(1, D_SSM), norm_g.reshape(1, D_SSM))


DIL_BLK = 128
DIL_GROUP = 4


def _dil_kernel(q_ref, k_ref, v_ref, o_ref, acc_ref, l_ref, ma_ref, mb_ref, bias_ref):
    s_len = q_ref.shape[0]
    B = DIL_BLK
    scale = HEAD_DIM ** -0.5
    lane = lax.broadcasted_iota(jnp.int32, (B, LANES), 1)
    head_a = lane < HEAD_DIM
    qi = lax.broadcasted_iota(jnp.int32, (2 * B, 2 * B), 0) & (B - 1)
    ki = lax.broadcasted_iota(jnp.int32, (2 * B, 2 * B), 1)
    band = (ki >= qi) & (ki <= qi + B)
    bias_ref[0] = jnp.where(band & (ki >= B), 0.0, NEG)
    bias_ref[1] = jnp.where(band, 0.0, NEG)
    ones = jnp.ones((2 * B, LANES), BF16)

    def comb(x):
        return jnp.where(head_a, x[:B], x[B:])

    def rows(ref, start, d):
        if d == 1:
            return ref[pl.ds(start, B), :]
        return ref[pl.ds(start, B, stride=d), :]

    def store_rows(ref, start, d, val):
        if d == 1:
            ref[pl.ds(start, B), :] = val
        else:
            ref[pl.ds(start, B, stride=d), :] = val

    order = sorted(DILATED_BRANCHES, key=lambda wd: -wd[1])
    assert order[-1][1] == 1
    for bi, (w, d) in enumerate(order):
        assert w // d == B
        first_branch = bi == 0
        last_branch = bi == len(order) - 1
        per_res = s_len // (B * d)
        G = min(DIL_GROUP, per_res)
        assert per_res % G == 0

        def body(t, carry, d=d, first_branch=first_branch, last_branch=last_branch, per_res=per_res, G=G):
            r = t // (per_res // G)
            n0 = (t % (per_res // G)) * G
            starts = [(n0 + j) * (B * d) + r for j in range(G)]
            prev0 = jnp.maximum(n0 - 1, 0) * (B * d) + r
            if d == 1:
                starts = [pl.multiple_of(st, B) for st in starts]
                prev0 = pl.multiple_of(prev0, B)
            kb = [rows(k_ref, st, d).astype(BF16) for st in [prev0] + starts]
            vb = [rows(v_ref, st, d).astype(BF16) for st in [prev0] + starts]
            for j in range(G):
                start = starts[j]
                q = rows(q_ref, start, d) * scale
                qs = jnp.concatenate([jnp.where(head_a, q, 0.0), jnp.where(head_a, 0.0, q)], axis=0).astype(BF16)
                kk = jnp.concatenate([kb[j], kb[j + 1]], axis=0)
                s = lax.dot_general(qs, kk, (((1,), (1,)), ((), ())), preferred_element_type=F32)
                s = s + (bias_ref[jnp.minimum(n0, 1)] if j == 0 else bias_ref[1])
                m_row = jnp.broadcast_to(jnp.max(s, axis=-1, keepdims=True), (2 * B, LANES))
                if first_branch:
                    m_new = m_row
                else:
                    m_old = jnp.concatenate([rows(ma_ref, start, d), rows(mb_ref, start, d)], axis=0)
                    m_new = jnp.maximum(m_old, m_row)
                p = jnp.exp(s - jnp.concatenate([m_new, m_new], axis=1)).astype(BF16)
                pv = _dot(p, jnp.concatenate([jnp.concatenate([vb[j], vb[j + 1]], axis=0), ones], axis=1))
                l_new = comb(pv[:, LANES:])
                pv = comb(pv[:, :LANES])
                if first_branch:
                    acc, l_b = pv, l_new
                else:
                    alpha = comb(jnp.exp(m_old - m_new))
                    acc = rows(acc_ref, start, d) * alpha + pv
                    l_b = rows(l_ref, start, d) * alpha + l_new
                if last_branch:
                    o_ref[pl.ds(start, B), :] = (acc / l_b).astype(o_ref.dtype)
                else:
                    store_rows(acc_ref, start, d, acc)
                    store_rows(l_ref, start, d, l_b)
                    store_rows(ma_ref, start, d, m_new[:B])
                    store_rows(mb_ref, start, d, m_new[B:])
            return carry

        lax.fori_loop(0, s_len // (B * G), body, 0)


def dilated_prompt(proj):
    s = proj.shape[0]
    n_pairs = D_ATTN // LANES
    spec = lambda off: pl.BlockSpec((s, LANES), lambda h: (0, off // LANES + h))
    return pl.pallas_call(
        _dil_kernel, grid=(n_pairs,),
        in_specs=[spec(COL_Q), spec(COL_K), spec(COL_V)],
        out_specs=pl.BlockSpec((s, LANES), lambda h: (0, h)),
        out_shape=jax.ShapeDtypeStruct((s, D_ATTN), BF16),
        scratch_shapes=[pltpu.VMEM((s, LANES), F32)] * 4 + [pltpu.VMEM((2, 2 * DIL_BLK, 2 * DIL_BLK), F32)],
        compiler_params=_params(("parallel",)), name="dilated_prompt",
    )(proj, proj, proj)


SSD_SAMPLE_BB = 8


def _head_expand(x, seg_t):
    hi, mid, lo = _split3(x)
    return _dot(hi, seg_t) + _dot(mid, seg_t) + _dot(lo, seg_t)


def _ssd_sample_kernel(z_ref, xs_ref, bc_ref, dt_ref, sc_ref, st_ref, cw_ref, cb_ref, dtb_ref, alog_ref,
                       dskip_ref, ng_ref, segt_ref,
                       y_ref, ho_ref, convo_ref,
                       xs_s, t2_s, dec_s, b_s, c_s, col_s, yoff_s):
    step = pl.program_id(0)
    nb = xs_ref.shape[0]
    BB = SSD_SAMPLE_BB
    GW = 4 * HEAD_DIM
    HP = SSM_HEADS * HEAD_DIM

    @pl.when(step == 0)
    def _():
        xnew = jnp.concatenate([xs_ref[...], bc_ref[...]], axis=1)
        rows = [sc_ref[:, j * XBC_DIM:(j + 1) * XBC_DIM] for j in range(CONV_W - 1)] + [xnew]
        taps = [rows[j] * cw_ref[j:j + 1, :] for j in range(CONV_W)]
        act = _silu(cb_ref[...] + (((taps[0] + taps[1]) + taps[2]) + taps[3]))
        for j in range(CONV_W - 1):
            convo_ref[:, j * XBC_DIM:(j + 1) * XBC_DIM] = rows[j + 1]
        xs = act[:, 0:D_SSM]
        bm = act[:, D_SSM:D_SSM + SSM_GROUPS * SSM_STATE]
        cm = act[:, D_SSM + SSM_GROUPS * SSM_STATE:]
        dt = _softplus(dt_ref[...] + dtb_ref[...])
        dec = jnp.exp(dt * (-jnp.exp(alog_ref[...])))
        seg_t = segt_ref[...]
        u = _head_expand(dt, seg_t) * xs
        dec_e = _head_expand(dec, seg_t)
        xs_s[...] = xs
        dec_s[...] = dec_e
        b_s[...] = bm
        c_s[...] = cm
        for g in range(SSM_GROUPS):
            sl = slice(g * SSM_STATE, (g + 1) * SSM_STATE)
            bc = jnp.sum(bm[:, sl] * cm[:, sl], axis=-1, keepdims=True)
            t2_s[:, g * GW:(g + 1) * GW] = u[:, g * GW:(g + 1) * GW] * bc
        for j in range(HP // LANES):
            for src, base in ((u, 0), (dec_e, HP)):
                t = src[:, j * LANES:(j + 1) * LANES].T
                for k, part in enumerate(_split3(t)):
                    col_s[base + j * LANES:base + (j + 1) * LANES, k * nb:(k + 1) * nb] = part

    def per_seq(i, carry):
        b = step * BB + i
        ridx = lax.broadcasted_iota(jnp.int32, (3 * nb, SSM_STATE), 0)
        sel = ((ridx == b) | (ridx == b + nb) | (ridx == b + 2 * nb)).astype(BF16)
        cols = _dot(col_s[...], sel)
        h0 = st_ref[i].reshape(HP, SSM_STATE)
        brow = b_s[pl.ds(b, 1), :]
        c8 = c_s[pl.ds(pl.multiple_of(step * BB, BB), BB), :]
        pick = lax.broadcasted_iota(jnp.int32, (BB, GW), 0) == i
        for g in range(SSM_GROUPS):
            rs = slice(g * GW, (g + 1) * GW)
            ls = slice(g * SSM_STATE, (g + 1) * SSM_STATE)
            h0g = h0[rs]
            hn = cols[HP + g * GW:HP + (g + 1) * GW] * h0g + cols[rs] * brow[:, ls]
            ho_ref[i, 4 * g:4 * g + 4] = hn.reshape(4, HEAD_DIM, SSM_STATE)
            yo = lax.dot_general(c8[:, ls].astype(BF16), h0g.astype(BF16), (((1,), (1,)), ((), ())),
                                 preferred_element_type=F32)
            yoff_s[pl.ds(b, 1), rs] = jnp.sum(jnp.where(pick, yo, 0.0), axis=0, keepdims=True)
        return carry

    lax.fori_loop(0, BB, per_seq, 0)

    @pl.when(step == pl.num_programs(0) - 1)
    def _():
        xs = xs_s[...]
        y = dec_s[...] * yoff_s[...] + t2_s[...] + dskip_ref[...] * xs
        y = y * _silu(z_ref[...])
        for g in range(SSM_GROUPS):
            yg = y[:, g * GW:(g + 1) * GW]
            yg = yg * lax.rsqrt(jnp.mean(yg * yg, axis=-1, keepdims=True) + RMS_EPS)
            y_ref[:, g * GW:(g + 1) * GW] = (yg * ng_ref[:, g * GW:(g + 1) * GW]).astype(y_ref.dtype)


def _seg_ones(n_heads, width):
    c = jnp.arange(n_heads * width)[:, None] // width
    seg = (c == jnp.arange(LANES)[None, :]).astype(BF16)
    return seg, seg.T


def ssd_sample(proj, dt_raw, state_conv, state_ssm, conv_w, conv_b, dt_bias, a_log, d_skip, norm_g):
    nb = proj.shape[0]
    BB = SSD_SAMPLE_BB
    HP = SSM_HEADS * HEAD_DIM
    col = lambda off: off // 1024
    const = lambda shape: pl.BlockSpec(shape, lambda s: (0,) * len(shape))
    _, seg_t = _seg_ones(SSM_HEADS, HEAD_DIM)
    st_spec = pl.BlockSpec((BB, SSM_HEADS, HEAD_DIM, SSM_STATE), lambda s: (s, 0, 0, 0))
    y, h_new, conv_new = pl.pallas_call(
        _ssd_sample_kernel, grid=(nb // BB,),
        in_specs=[
            pl.BlockSpec((nb, D_SSM), lambda s: (0, col(COL_Z))),
            pl.BlockSpec((nb, D_SSM), lambda s: (0, col(COL_XBC))),
            pl.BlockSpec((nb, XBC_DIM - D_SSM), lambda s: (0, col(COL_XBC + D_SSM))),
            const((nb, LANES)), const((nb, (CONV_W - 1) * XBC_DIM)), st_spec,
            const((CONV_W, XBC_DIM)), const((1, XBC_DIM)), const((1, LANES)), const((1, LANES)),
            const((1, D_SSM)), const((1, D_SSM)), const((LANES, HP)),
        ],
        out_specs=[const((nb, D_SSM)), st_spec, const((nb, (CONV_W - 1) * XBC_DIM))],
        out_shape=[
            jax.ShapeDtypeStruct((nb, D_SSM), BF16),
            jax.ShapeDtypeStruct(state_ssm.shape, F32),
            jax.ShapeDtypeStruct((nb, (CONV_W - 1) * XBC_DIM), F32),
        ],
        scratch_shapes=[
            pltpu.VMEM((nb, HP), F32), pltpu.VMEM((nb, HP), F32), pltpu.VMEM((nb, HP), F32),
            pltpu.VMEM((nb, SSM_GROUPS * SSM_STATE), F32), pltpu.VMEM((nb, SSM_GROUPS * SSM_STATE), F32),
            pltpu.VMEM((2 * HP, 3 * nb), BF16), pltpu.VMEM((nb, HP), F32),
        ],
        compiler_params=_params(("arbitrary",)), name="ssd_sample",
    )(proj, proj, proj, dt_raw, state_conv.reshape(nb, (CONV_W - 1) * XBC_DIM), state_ssm, conv_w,
      conv_b.reshape(1, XBC_DIM), _pad_lanes(dt_bias), _pad_lanes(a_log),
      jnp.repeat(d_skip, HEAD_DIM).reshape(1, D_SSM), norm_g.reshape(1, D_SSM), seg_t)
    return y, h_new, conv_new.reshape(nb, CONV_W - 1, XBC_DIM)


WIN_UNIT_HEADS = 4
WIN_UNITS_PER_STEP = 2


def _win_unit_attention(n_branch, q4, kn4, vn4, cnt, kt, vt):
    rep = lambda t: jnp.stack([jnp.broadcast_to(t[h:h + 1, :], (SUBLANES, t.shape[1])) for h in range(t.shape[0])])
    q = rep(q4) * (HEAD_DIM ** -0.5)
    s = jnp.einsum("hqd,hdw->hqw", q.astype(BF16), kt.astype(BF16), preferred_element_type=F32)
    s = jnp.where(cnt > 0.0, s, NEG)
    s_new = jnp.sum(q * rep(kn4), axis=-1, keepdims=True)
    m = jnp.maximum(jnp.max(s, axis=-1, keepdims=True), s_new)
    p = cnt * jnp.exp(s - m)
    p_new = n_branch * jnp.exp(s_new - m)
    l = jnp.sum(p, axis=-1, keepdims=True) + p_new
    o = jnp.einsum("hqw,hdw->hqd", p.astype(BF16), vt.astype(BF16), preferred_element_type=F32)
    return (o + p_new * rep(vn4)) / l


def _ffn_stream_kernel(n_branch, x_ref, g2_ref, wg_ref, wu_ref, wd_ref, gf_ref,
                       q_ref, kn_ref, vn_ref, cnt_ref, kt_hbm, vt_hbm,
                       o_ref, oa_ref, hn_ref, acc_ref, kbuf, vbuf, sem):
    f = pl.program_id(1)
    step = pl.program_id(0) * pl.num_programs(1) + f
    uh = WIN_UNIT_HEADS
    per_seq = kt_hbm.shape[1] // uh
    n_units = kt_hbm.shape[0] * per_seq

    def copies(u, slot):
        b = u // per_seq
        h0 = (u % per_seq) * uh
        return (pltpu.make_async_copy(kt_hbm.at[b, pl.ds(h0, uh)], kbuf.at[slot], sem.at[0, slot]),
                pltpu.make_async_copy(vt_hbm.at[b, pl.ds(h0, uh)], vbuf.at[slot], sem.at[1, slot]))

    @pl.when(step == 0)
    def _():
        for slot in range(WIN_UNITS_PER_STEP):
            for c in copies(slot, slot):
                c.start()

    cnt = cnt_ref[...][None]
    for slot in range(WIN_UNITS_PER_STEP):
        u = step * WIN_UNITS_PER_STEP + slot

        @pl.when(u < n_units)
        def _(u=u, slot=slot):
            for c in copies(u, slot):
                c.wait()
            b = u // per_seq
            h0 = (u % per_seq) * uh
            rows = pl.ds(h0, uh)
            o = _win_unit_attention(n_branch, q_ref[b, rows, :], kn_ref[b, rows, :], vn_ref[b, rows, :], cnt,
                                    kbuf[slot], vbuf[slot])
            for h in range(uh):
                oa_ref[b, pl.ds(h0 + h, 1), :] = o[h, 0:1, :]

            @pl.when(u + WIN_UNITS_PER_STEP < n_units)
            def _():
                for c in copies(u + WIN_UNITS_PER_STEP, slot):
                    c.start()

    @pl.when(f == 0)
    def _():
        hn_ref[...] = _rms(x_ref[...], g2_ref[...]).astype(BF16)
        acc_ref[...] = jnp.zeros_like(acc_ref)

    hn = hn_ref[...]
    act = _silu(_dot(hn, wg_ref[...])) * _dot(hn, wu_ref[...])
    acc_ref[...] += _dot(act.astype(BF16), wd_ref[...])

    @pl.when(f == pl.num_programs(1) - 1)
    def _():
        o_ref[...] = _rms(x_ref[...] + acc_ref[...], gf_ref[...])


def ffn_final_with_window_attention(x, g2, wg, wu, wd, gf, q, kn, vn, cache_k, cache_v, *, tm, tf):
    m, d = x.shape
    dff = wg.shape[1]
    nb, w_buf = cache_k.shape[0], cache_k.shape[1]
    grid = (m // tm, dff // tf)
    assert nb * (ATTN_HEADS // WIN_UNIT_HEADS) <= grid[0] * grid[1] * WIN_UNITS_PER_STEP
    pos = jnp.arange(w_buf)
    cnt = jnp.zeros((w_buf,), F32)
    for w, dil in DILATED_BRANCHES:
        cnt = cnt + ((pos >= w_buf - w) & ((w_buf - pos) % dil == 0)).astype(F32)
    const = lambda shape: pl.BlockSpec(shape, lambda i, f: (0,) * len(shape))
    small = const((nb, ATTN_HEADS, HEAD_DIM))
    unit = (WIN_UNITS_PER_STEP, WIN_UNIT_HEADS, HEAD_DIM, w_buf)
    y, att = pl.pallas_call(
        functools.partial(_ffn_stream_kernel, float(len(DILATED_BRANCHES))), grid=grid,
        in_specs=[
            pl.BlockSpec((tm, d), lambda i, f: (i, 0)),
            const((1, d)),
            pl.BlockSpec((d, tf), lambda i, f: (0, f)),
            pl.BlockSpec((d, tf), lambda i, f: (0, f)),
            pl.BlockSpec((tf, d), lambda i, f: (f, 0)),
            const((1, d)),
            small, small, small, const((1, w_buf)),
            pl.BlockSpec(memory_space=pl.ANY), pl.BlockSpec(memory_space=pl.ANY),
        ],
        out_specs=[pl.BlockSpec((tm, d), lambda i, f: (i, 0)), small],
        out_shape=[jax.ShapeDtypeStruct((m, d), F32), jax.ShapeDtypeStruct((nb, ATTN_HEADS, HEAD_DIM), F32)],
        scratch_shapes=[pltpu.VMEM((tm, d), BF16), pltpu.VMEM((tm, d), F32),
                        pltpu.VMEM(unit, F32), pltpu.VMEM(unit, F32),
                        pltpu.SemaphoreType.DMA((2, WIN_UNITS_PER_STEP))],
        compiler_params=_params(("arbitrary", "arbitrary")), name="ffn_final_window_stream",
    )(x, g2.reshape(1, d), wg, wu, wd, gf.reshape(1, d), q, kn, vn, cnt.reshape(1, w_buf),
      jnp.transpose(cache_k, (0, 2, 3, 1)), jnp.transpose(cache_v, (0, 2, 3, 1)))
    return y, att.reshape(nb, D_ATTN)


MEM_SAMPLE_BB = 4


def _mem_sample_kernel(q_ref, k_ref, v_ref, o_ref):
    shape = (q_ref.shape[1], k_ref.shape[1])
    row = lax.broadcasted_iota(jnp.int32, shape, 0)
    col = lax.broadcasted_iota(jnp.int32, shape, 1)
    own = (col & (MEM_HEADS - 1)) == (row & (MEM_HEADS - 1))
    for i in range(q_ref.shape[0]):
        q = (q_ref[i] * (MEM_HEAD_DIM ** -0.5)).astype(BF16)
        s = lax.dot_general(q, k_ref[i].astype(BF16), (((1,), (1,)), ((), ())), preferred_element_type=F32)
        s = jnp.where(own, s, NEG)
        m = jnp.max(s, axis=-1, keepdims=True)
        p = jnp.where(own, jnp.exp(s - m), 0.0)
        l = jnp.sum(p, axis=-1, keepdims=True)
        o_ref[i] = _dot((p / l).astype(BF16), v_ref[i].astype(BF16))


def mem_attention_sample(q, cache_mk, cache_mv):
    nb = q.shape[0]
    assert MEM_HEADS & (MEM_HEADS - 1) == 0 and SUBLANES % MEM_HEADS == 0
    q8 = jnp.tile(q.reshape(nb, MEM_HEADS, MEM_HEAD_DIM), (1, SUBLANES // MEM_HEADS, 1))
    rows = MEM_LEN * MEM_HEADS
    bb = MEM_SAMPLE_BB
    small = pl.BlockSpec((bb, SUBLANES, MEM_HEAD_DIM), lambda b: (b, 0, 0))
    big = pl.BlockSpec((bb, rows, MEM_HEAD_DIM), lambda b: (b, 0, 0))
    out = pl.pallas_call(
        _mem_sample_kernel, grid=(nb // bb,), in_specs=[small, big, big], out_specs=small,
        out_shape=jax.ShapeDtypeStruct((nb, SUBLANES, MEM_HEAD_DIM), F32),
        compiler_params=_params(("parallel",)), name="mem_attention_sample",
    )(q8, cache_mk.reshape(nb, rows, MEM_HEAD_DIM), cache_mv.reshape(nb, rows, MEM_HEAD_DIM))
    return out[:, :MEM_HEADS].reshape(nb, D_MEM)


def _layer_tail(x, y_ssm, y_att, mem_attend, lw, gf, *, tm, tf):
    half = D_SSM
    x1 = matmul_residual([(y_ssm, lw["w_out"], 0), (y_att, lw["w_out"], 1)], x, tm=tm, tn=1024)
    assert y_ssm.shape[1] == half and y_att.shape[1] == half
    hq = rms_matmul(x1, lw["ln_mem_g"], lw["w_mq"], tm=tm, tn=D_MEM)
    x2 = matmul_residual([(mem_attend(hq), lw["w_mo"], 0)], x1, tm=tm, tn=1024)
    return ffn_final(x2, lw["ln2_g"], lw["w_gate"], lw["w_up"], lw["w_down"], gf, tm=min(tm, 512), tf=tf)


def kernel(x_prompt, x_sample, cache_win_k, cache_win_v, state_conv, state_ssm, cache_mem_k, cache_mem_v, mem_prompt, ln1_g, w_in, conv_w, conv_b, dt_bias, a_log, d_skip, ssm_norm_g, w_out, ln_mem_g, mem_norm_g, w_mq, w_mk, w_mv, w_mo, ln2_g, w_gate, w_up, w_down, ln_f_g):
    bp, s, d = x_prompt.shape
    nb, ts, _ = x_sample.shape
    depth = w_in.shape[0]
    assert bp == 1 and ts == 1 and depth == 1
    w_keep = min(DILATED_BRANCHES[-1][0], s)
    l = 0
    dt_lo = D_SSM + XBC_DIM
    dt_hi = dt_lo + SSM_HEADS
    w_main = jnp.concatenate([w_in[l][:, :dt_lo], w_in[l][:, dt_hi:]], axis=1).astype(BF16)
    w_dt = jnp.concatenate(_split3(jnp.pad(w_in[l][:, dt_lo:dt_hi], ((0, 0), (0, LANES - SSM_HEADS))))[:2], axis=1)
    lw = dict(w_out=w_out[l].astype(BF16), ln_mem_g=ln_mem_g[l], w_mq=w_mq[l].astype(BF16),
              w_mo=w_mo[l].astype(BF16), ln2_g=ln2_g[l], w_gate=w_gate[l].astype(BF16),
              w_up=w_up[l].astype(BF16), w_down=w_down[l].astype(BF16))
    w_mkv = jnp.concatenate([w_mk[l], w_mv[l]], axis=1).astype(BF16)
    ssm_w = (conv_w[l], conv_b[l], dt_bias[l], a_log[l], d_skip[l], ssm_norm_g[l])

    xs = x_sample.reshape(nb, d)
    proj_s, dt_s = rms_matmul(xs, ln1_g[l], w_main, tm=nb, tn=1024, wdt=w_dt)
    heads = lambda t: t.reshape(t.shape[0], ATTN_HEADS, HEAD_DIM)
    q_s, k_s, v_s = (heads(proj_s[:, c:c + D_ATTN]) for c in (COL_Q, COL_K, COL_V))

    xp = x_prompt.reshape(s, d)
    mkv = rms_matmul(mem_prompt.reshape(MEM_LEN, d), mem_norm_g[l], w_mkv, tm=MEM_LEN, tn=2 * D_MEM)
    proj, dt_raw = rms_matmul(xp, ln1_g[l], w_main, tm=1024, tn=1024, wdt=w_dt)
    y_ssm, h_p, conv_p = ssd_prompt(proj, dt_raw, *ssm_w)
    y_att = dilated_prompt(proj)
    x2 = mix_mem_fused(xp, y_ssm, y_att, lw["w_out"], lw["ln_mem_g"], lw["w_mq"], mkv, lw["w_mo"], tm=512)
    y_prompt, y_att_s = ffn_final_with_window_attention(
        x2, lw["ln2_g"], lw["w_gate"], lw["w_up"], lw["w_down"], ln_f_g,
        q_s, k_s, v_s, cache_win_k[l], cache_win_v[l], tm=512, tf=256)

    y_ssm_s, h_s, conv_s = ssd_sample(proj_s, dt_s, state_conv[l], state_ssm[l], *ssm_w)
    y_sample = _layer_tail(xs, y_ssm_s, y_att_s,
                           lambda hq: mem_attention_sample(hq, cache_mem_k[l], cache_mem_v[l]), lw, ln_f_g,
                           tm=nb, tf=512)

    return (
        y_prompt.reshape(bp, s, d),
        y_sample.reshape(nb, ts, d),
        heads(proj[s - w_keep:, COL_K:COL_V])[None, None],
        heads(proj[s - w_keep:, COL_V:])[None, None],
        conv_p[None, None],
        h_p[None, None],
        mkv[:, :D_MEM].reshape(1, 1, MEM_LEN, MEM_HEADS, MEM_HEAD_DIM),
        mkv[:, D_MEM:].reshape(1, 1, MEM_LEN, MEM_HEADS, MEM_HEAD_DIM),
        k_s[None, :, None],
        v_s[None, :, None],
        conv_s[None],
        h_s[None],
    )
```

```python
import functools

import jax
import jax.numpy as jnp
from jax import lax
from jax.experimental import pallas as pl
from jax.experimental.pallas import tpu as pltpu

F32 = jnp.float32
BF16 = jnp.bfloat16

D_MODEL = 2048
D_SSM = 1024
D_ATTN = 1024
HEAD_DIM = 64
SSM_HEADS = 16
SSM_GROUPS = 4
SSM_STATE = 128
CONV_W = 4
SSD_CHUNK = 128
XBC_DIM = D_SSM + 2 * SSM_GROUPS * SSM_STATE
ATTN_HEADS = 16
DILATED_BRANCHES = ((128, 1), (512, 4), (2048, 16))
MEM_LEN = 256
MEM_HEADS = 4
MEM_HEAD_DIM = 128
D_MEM = MEM_HEADS * MEM_HEAD_DIM
RMS_EPS = 1e-5
NEG = -1e30

LANES = 128
SUBLANES = 8
VMEM_BYTES_V7X = 64 * 1024 * 1024
VMEM_LIMIT = 56 * 1024 * 1024

COL_Z = 0
COL_XBC = D_SSM
COL_Q = D_SSM + XBC_DIM
COL_K = COL_Q + D_ATTN
COL_V = COL_K + D_ATTN
D_PROJ = COL_V + D_ATTN


def _params(sem):
    return pltpu.CompilerParams(dimension_semantics=sem, vmem_limit_bytes=VMEM_LIMIT)


def _rms(x, g):
    ms = jnp.mean(x * x, axis=-1, keepdims=True)
    return x * lax.rsqrt(ms + RMS_EPS) * g


def _silu(x):
    return x * (0.5 * jnp.tanh(0.5 * x) + 0.5)


def _softplus(x):
    return jnp.maximum(x, 0.0) + jnp.log1p(jnp.exp(-jnp.abs(x)))


def _split3(x):
    hi = x.astype(BF16)
    r1 = x - hi.astype(F32)
    mid = r1.astype(BF16)
    lo = (r1 - mid.astype(F32)).astype(BF16)
    return hi, mid, lo


def _dot(a, b):
    return jnp.dot(a, b, preferred_element_type=F32)


def _rms_matmul_kernel(x_ref, g_ref, w_ref, o_ref, hn_ref):
    @pl.when(pl.program_id(1) == 0)
    def _():
        hn_ref[...] = _rms(x_ref[...], g_ref[...]).astype(BF16)

    o_ref[...] = _dot(hn_ref[...], w_ref[...])


def _rms_matmul_dt_kernel(x_ref, g_ref, w_ref, wdt_ref, o_ref, dt_ref, hn_ref):
    @pl.when(pl.program_id(1) == 0)
    def _():
        hn = _rms(x_ref[...], g_ref[...])
        hi = hn.astype(BF16)
        lo = (hn - hi.astype(F32)).astype(BF16)
        hn_ref[...] = hi
        both = _dot(hi, wdt_ref[...]) + _dot(lo, wdt_ref[...])
        dt_ref[...] = both[:, :LANES] + both[:, LANES:]

    o_ref[...] = _dot(hn_ref[...], w_ref[...])


def rms_matmul(x, g, w, *, tm, tn, wdt=None):
    m, k = x.shape
    n = w.shape[1]
    grid = (m // tm, n // tn)
    in_specs = [
        pl.BlockSpec((tm, k), lambda i, j: (i, 0)),
        pl.BlockSpec((1, k), lambda i, j: (0, 0)),
        pl.BlockSpec((k, tn), lambda i, j: (0, j)),
    ]
    o_spec = pl.BlockSpec((tm, tn), lambda i, j: (i, j))
    scratch = [pltpu.VMEM((tm, k), BF16)]
    if wdt is None:
        return pl.pallas_call(
            _rms_matmul_kernel, grid=grid, in_specs=in_specs, out_specs=o_spec,
            out_shape=jax.ShapeDtypeStruct((m, n), F32), scratch_shapes=scratch,
            compiler_params=_params(("parallel", "arbitrary")), name="rms_matmul",
        )(x, g.reshape(1, k), w)
    in_specs.append(pl.BlockSpec((k, 2 * LANES), lambda i, j: (0, 0)))
    return pl.pallas_call(
        _rms_matmul_dt_kernel, grid=grid, in_specs=in_specs,
        out_specs=[o_spec, pl.BlockSpec((tm, LANES), lambda i, j: (i, 0))],
        out_shape=[jax.ShapeDtypeStruct((m, n), F32), jax.ShapeDtypeStruct((m, LANES), F32)],
        scratch_shapes=scratch,
        compiler_params=_params(("parallel", "arbitrary")), name="rms_matmul_dt",
    )(x, g.reshape(1, k), w, wdt)


def _mm_res_kernel(n_pairs, *refs):
    res_ref = refs[2 * n_pairs]
    o_ref = refs[2 * n_pairs + 1]
    acc = res_ref[...]
    for i in range(n_pairs):
        acc = acc + _dot(refs[2 * i][...].astype(BF16), refs[2 * i + 1][...])
    o_ref[...] = acc


def matmul_residual(pairs, res, *, tm, tn):
    m, n = res.shape
    grid = (m // tm, n // tn)
    in_specs, args = [], []
    for a, w, kblk in pairs:
        k = a.shape[1]
        in_specs.append(pl.BlockSpec((tm, k), lambda i, j: (i, 0)))
        in_specs.append(pl.BlockSpec((k, tn), lambda i, j, kblk=kblk: (kblk, j)))
        args += [a, w]
    in_specs.append(pl.BlockSpec((tm, tn), lambda i, j: (i, j)))
    args.append(res)
    return pl.pallas_call(
        functools.partial(_mm_res_kernel, len(pairs)), grid=grid, in_specs=in_specs,
        out_specs=pl.BlockSpec((tm, tn), lambda i, j: (i, j)),
        out_shape=jax.ShapeDtypeStruct((m, n), F32),
        compiler_params=_params(("parallel", "parallel")), name="matmul_residual",
    )(*args)


def _ffn_step(x_ref, g2_ref, wg_ref, wu_ref, wd_ref, gf_ref, o_ref, hn_ref):
    f = pl.program_id(1)

    @pl.when(f == 0)
    def _():
        hn_ref[...] = _rms(x_ref[...], g2_ref[...]).astype(BF16)
        o_ref[...] = jnp.zeros_like(o_ref)

    hn = hn_ref[...]
    act = _silu(_dot(hn, wg_ref[...])) * _dot(hn, wu_ref[...])
    o_ref[...] += _dot(act.astype(BF16), wd_ref[...])

    @pl.when(f == pl.num_programs(1) - 1)
    def _():
        o_ref[...] = _rms(x_ref[...] + o_ref[...], gf_ref[...])


def _ffn_kernel(x_ref, g2_ref, wg_ref, wu_ref, wd_ref, gf_ref, o_ref, hn_ref):
    _ffn_step(x_ref, g2_ref, wg_ref, wu_ref, wd_ref, gf_ref, o_ref, hn_ref)


def ffn_final(x, g2, wg, wu, wd, gf, *, tm, tf):
    m, d = x.shape
    dff = wg.shape[1]
    grid = (m // tm, dff // tf)
    return pl.pallas_call(
        _ffn_kernel, grid=grid,
        in_specs=[
            pl.BlockSpec((tm, d), lambda i, f: (i, 0)),
            pl.BlockSpec((1, d), lambda i, f: (0, 0)),
            pl.BlockSpec((d, tf), lambda i, f: (0, f)),
            pl.BlockSpec((d, tf), lambda i, f: (0, f)),
            pl.BlockSpec((tf, d), lambda i, f: (f, 0)),
            pl.BlockSpec((1, d), lambda i, f: (0, 0)),
        ],
        out_specs=pl.BlockSpec((tm, d), lambda i, f: (i, 0)),
        out_shape=jax.ShapeDtypeStruct((m, d), F32),
        scratch_shapes=[pltpu.VMEM((tm, d), BF16)],
        compiler_params=_params(("parallel", "arbitrary")), name="ffn_final",
    )(x, g2.reshape(1, d), wg, wu, wd, gf.reshape(1, d))


def _mix_mem_kernel(x_ref, ys_ref, ya_ref, wo_ref, g_ref, wq_ref, mkv_ref, wm_ref, o_ref):
    half = ys_ref.shape[1]
    x1 = x_ref[...] + _dot(ys_ref[...], wo_ref[0:half, :]) + _dot(ya_ref[...], wo_ref[half:, :])
    hq = _dot(_rms(x1, g_ref[...]).astype(BF16), wq_ref[...])
    scale = MEM_HEAD_DIM ** -0.5
    outs = []
    for h in range(MEM_HEADS):
        sl = slice(h * MEM_HEAD_DIM, (h + 1) * MEM_HEAD_DIM)
        k = mkv_ref[:, sl].astype(BF16)
        v = mkv_ref[:, D_MEM + h * MEM_HEAD_DIM:D_MEM + (h + 1) * MEM_HEAD_DIM].astype(BF16)
        s = lax.dot_general(hq[:, sl].astype(BF16), k, (((1,), (1,)), ((), ())), preferred_element_type=F32) * scale
        p = jnp.exp(s - jnp.max(s, axis=-1, keepdims=True))
        p = p / jnp.sum(p, axis=-1, keepdims=True)
        outs.append(_dot(p.astype(BF16), v).astype(BF16))
    o_ref[...] = x1 + _dot(jnp.concatenate(outs, axis=1), wm_ref[...])


def mix_mem_fused(x, y_ssm, y_att, w_out, ln_mem_g, w_mq, mkv, w_mo, *, tm):
    m, d = x.shape
    half = y_ssm.shape[1]
    row = lambda width: pl.BlockSpec((tm, width), lambda i: (i, 0))
    whole = lambda a: pl.BlockSpec(a.shape, lambda i: (0, 0), pipeline_mode=pl.Buffered(1))
    g = ln_mem_g.reshape(1, d)
    return pl.pallas_call(
        _mix_mem_kernel, grid=(m // tm,),
        in_specs=[row(d), row(half), row(half), whole(w_out), whole(g), whole(w_mq), whole(mkv), whole(w_mo)],
        out_specs=row(d),
        out_shape=jax.ShapeDtypeStruct((m, d), F32),
        compiler_params=_params(("parallel",)), name="mix_mem_fused",
    )(x, y_ssm, y_att, w_out, g, w_mq, mkv, w_mo)


def _head_cols(x, g, width):
    rows = x.shape[0]
    parts = [jnp.broadcast_to(x[:, 4 * g + i:4 * g + i + 1], (rows, width)) for i in range(4)]
    return jnp.concatenate(parts, axis=1)


def _ssd_prompt_kernel(z_ref, xs_ref, bc_ref, dt_ref, cw_ref, cb_ref, dtb_ref, alog_ref, dskip_ref, ng_ref,
                       y_ref, hfin_ref, convo_ref, cbuf, st_ref):
    c = pl.program_id(0)
    L = SSD_CHUNK
    P = HEAD_DIM
    GW = 4 * P

    @pl.when(c == 0)
    def _():
        cbuf[0:SUBLANES, :] = jnp.zeros((SUBLANES, XBC_DIM), F32)
        st_ref[...] = jnp.zeros_like(st_ref)

    cbuf[SUBLANES:SUBLANES + L, 0:D_SSM] = xs_ref[...]
    cbuf[SUBLANES:SUBLANES + L, D_SSM:XBC_DIM] = bc_ref[...]
    up = cbuf[...]
    taps = [(up if j == CONV_W - 1 else pltpu.roll(up, CONV_W - 1 - j, 0))[SUBLANES:SUBLANES + L] * cw_ref[j:j + 1, :]
            for j in range(CONV_W)]
    conv = cb_ref[...] + (((taps[0] + taps[1]) + taps[2]) + taps[3])

    @pl.when(c == pl.num_programs(0) - 1)
    def _():
        convo_ref[...] = cbuf[pl.ds(L + SUBLANES - (CONV_W - 1), CONV_W - 1), :]

    cbuf[0:SUBLANES, :] = cbuf[L:L + SUBLANES, :]

    act = _silu(conv)
    xs = act[:, 0:D_SSM]
    dt = _softplus(dt_ref[...] + dtb_ref[...])
    a = dt * (-jnp.exp(alog_ref[...]))
    ri = lax.broadcasted_iota(jnp.int32, (L, L), 0)
    ci = lax.broadcasted_iota(jnp.int32, (L, L), 1)
    causal = ri >= ci
    acum = jnp.dot(causal.astype(F32), a, precision=lax.Precision.HIGHEST, preferred_element_type=F32)
    acum_t = acum.T
    dt_t = dt.T
    last = acum[L - 1:L, :]
    ea = jnp.exp(acum)
    wend = jnp.exp(last - acum) * dt
    cdec = jnp.exp(last)
    lane = lax.broadcasted_iota(jnp.int32, (L, GW), 1)
    head_lanes = [(lane >= i * P) & (lane < (i + 1) * P) for i in range(4)]

    ys = []
    for g in range(SSM_GROUPS):
        bg = act[:, D_SSM + g * SSM_STATE:D_SSM + (g + 1) * SSM_STATE]
        cg = act[:, D_SSM + SSM_GROUPS * SSM_STATE + g * SSM_STATE:
                 D_SSM + SSM_GROUPS * SSM_STATE + (g + 1) * SSM_STATE]
        bgb = bg.astype(BF16)
        cgb = cg.astype(BF16)
        cb = lax.dot_general(cgb, bgb, (((1,), (1,)), ((), ())), preferred_element_type=F32)
        xg = xs[:, g * GW:(g + 1) * GW]
        yd = jnp.zeros((L, GW), F32)
        for i in range(4):
            h = 4 * g + i
            seg = acum[:, h:h + 1] - acum_t[h:h + 1, :]
            decay = jnp.exp(jnp.where(causal, seg, NEG))
            w = (cb * decay * dt_t[h:h + 1, :]).astype(BF16)
            xm = jnp.where(head_lanes[i], xg, 0.0).astype(BF16)
            yd = yd + _dot(w, xm)
        st = st_ref[g]
        yoff = _dot(cgb, st.astype(BF16)) * _head_cols(ea, g, P)
        ys.append(yd + yoff)
        xw = (xg * _head_cols(wend, g, P)).astype(BF16)
        new = _dot(bg.T.astype(BF16), xw)
        cd = jnp.concatenate([jnp.broadcast_to(cdec[:, 4 * g + i:4 * g + i + 1], (1, P)) for i in range(4)], axis=1)
        st_ref[g] = st * cd + new

    y = jnp.concatenate(ys, axis=1) + dskip_ref[...] * xs
    y = y * _silu(z_ref[...])
    outs = []
    for g in range(SSM_GROUPS):
        yg = y[:, g * GW:(g + 1) * GW]
        outs.append(yg * lax.rsqrt(jnp.mean(yg * yg, axis=-1, keepdims=True) + RMS_EPS))
    y_ref[...] = (jnp.concatenate(outs, axis=1) * ng_ref[...]).astype(y_ref.dtype)

    @pl.when(c == pl.num_programs(0) - 1)
    def _():
        for g in range(SSM_GROUPS):
            stg = st_ref[g]
            for i in range(4):
                hfin_ref[4 * g + i] = stg[:, i * P:(i + 1) * P].T


def _pad_lanes(v):
    return jnp.pad(v.astype(F32), (0, LANES - v.shape[0])).reshape(1, LANES)


def ssd_prompt(proj, dt_raw, conv_w, conv_b, dt_bias, a_log, d_skip, norm_g):
    s = proj.shape[0]
    L = SSD_CHUNK
    nblk = D_SSM // 1024
    col = lambda off: off // 1024
    const = lambda shape: pl.BlockSpec(shape, lambda c: (0,) * len(shape))
    return pl.pallas_call(
        _ssd_prompt_kernel, grid=(s // L,),
        in_specs=[
            pl.BlockSpec((L, D_SSM), lambda c: (c, col(COL_Z))),
            pl.BlockSpec((L, D_SSM), lambda c: (c, col(COL_XBC))),
            pl.BlockSpec((L, XBC_DIM - D_SSM), lambda c: (c, col(COL_XBC + D_SSM))),
            pl.BlockSpec((L, LANES), lambda c: (c, 0)),
            const((CONV_W, XBC_DIM)), const((1, XBC_DIM)), const((1, LANES)), const((1, LANES)),
            const((1, D_SSM)), const((1, D_SSM)),
        ],
        out_specs=[
            pl.BlockSpec((L, D_SSM), lambda c: (c, 0)),
            const((SSM_HEADS, HEAD_DIM, SSM_STATE)),
            const((CONV_W - 1, XBC_DIM)),
        ],
        out_shape=[
            jax.ShapeDtypeStruct((s, D_SSM), BF16),
            jax.ShapeDtypeStruct((SSM_HEADS, HEAD_DIM, SSM_STATE), F32),
            jax.ShapeDtypeStruct((CONV_W - 1, XBC_DIM), F32),
        ],
        scratch_shapes=[
            pltpu.VMEM((L + SUBLANES, XBC_DIM), F32),
            pltpu.VMEM((SSM_GROUPS, SSM_STATE, 4 * HEAD_DIM), F32),
        ],
        compiler_params=_params(("arbitrary",)), name="ssd_prompt",
    )(proj, proj, proj, dt_raw, conv_w, conv_b.reshape(1, XBC_DIM), _pad_lanes(dt_bias), _pad_lanes(a_log),
      jnp.repeat(d_skip, HEAD_DIM).reshape(1, D_SSM), norm_g.reshape(1, D_SSM))


DIL_BLK = 128
DIL_GROUP = 8


def _dil_kernel(q_ref, k_ref, v_ref, o_ref, acc_ref, l_ref, ma_ref, mb_ref, bias_ref):
    s_len = q_ref.shape[0]
    B = DIL_BLK
    scale = HEAD_DIM ** -0.5
    lane = lax.broadcasted_iota(jnp.int32, (B, LANES), 1)
    head_a = lane < HEAD_DIM
    qi = lax.broadcasted_iota(jnp.int32, (2 * B, 2 * B), 0) & (B - 1)
    ki = lax.broadcasted_iota(jnp.int32, (2 * B, 2 * B), 1)
    band = (ki >= qi) & (ki <= qi + B)
    bias_ref[0] = jnp.where(band & (ki >= B), 0.0, NEG)
    bias_ref[1] = jnp.where(band, 0.0, NEG)
    ones = jnp.ones((2 * B, LANES), BF16)

    def comb(x):
        return jnp.where(head_a, x[:B], x[B:])

    def rows(ref, start, d):
        if d == 1:
            return ref[pl.ds(start, B), :]
        return ref[pl.ds(start, B, stride=d), :]

    def store_rows(ref, start, d, val):
        if d == 1:
            ref[pl.ds(start, B), :] = val
        else:
            ref[pl.ds(start, B, stride=d), :] = val

    order = sorted(DILATED_BRANCHES, key=lambda wd: -wd[1])
    assert order[-1][1] == 1
    for bi, (w, d) in enumerate(order):
        assert w // d == B
        first_branch = bi == 0
        last_branch = bi == len(order) - 1
        per_res = s_len // (B * d)
        G = min(DIL_GROUP, per_res)
        assert per_res % G == 0

        def body(t, carry, d=d, first_branch=first_branch, last_branch=last_branch, per_res=per_res, G=G):
            r = t // (per_res // G)
            n0 = (t % (per_res // G)) * G
            starts = [(n0 + j) * (B * d) + r for j in range(G)]
            prev0 = jnp.maximum(n0 - 1, 0) * (B * d) + r
            if d == 1:
                starts = [pl.multiple_of(st, B) for st in starts]
                prev0 = pl.multiple_of(prev0, B)
            kb = [rows(k_ref, st, d).astype(BF16) for st in [prev0] + starts]
            vb = [rows(v_ref, st, d).astype(BF16) for st in [prev0] + starts]
            for j in range(G):
                start = starts[j]
                q = rows(q_ref, start, d) * scale
                qs = jnp.concatenate([jnp.where(head_a, q, 0.0), jnp.where(head_a, 0.0, q)], axis=0).astype(BF16)
                kk = jnp.concatenate([kb[j], kb[j + 1]], axis=0)
                s = lax.dot_general(qs, kk, (((1,), (1,)), ((), ())), preferred_element_type=F32)
                s = s + (bias_ref[jnp.minimum(n0, 1)] if j == 0 else bias_ref[1])
                m_row = jnp.broadcast_to(jnp.max(s, axis=-1, keepdims=True), (2 * B, LANES))
                if first_branch:
                    m_new = m_row
                else:
                    m_old = jnp.concatenate([rows(ma_ref, start, d), rows(mb_ref, start, d)], axis=0)
                    m_new = jnp.maximum(m_old, m_row)
                p = jnp.exp(s - jnp.concatenate([m_new, m_new], axis=1)).astype(BF16)
                pv = _dot(p, jnp.concatenate([jnp.concatenate([vb[j], vb[j + 1]], axis=0), ones], axis=1))
                l_new = comb(pv[:, LANES:])
                pv = comb(pv[:, :LANES])
                if first_branch:
                    acc, l_b = pv, l_new
                else:
                    alpha = comb(jnp.exp(m_old - m_new))
                    acc = rows(acc_ref, start, d) * alpha + pv
                    l_b = rows(l_ref, start, d) * alpha + l_new
                if last_branch:
                    o_ref[pl.ds(start, B), :] = (acc / l_b).astype(o_ref.dtype)
                else:
                    store_rows(acc_ref, start, d, acc)
                    store_rows(l_ref, start, d, l_b)
                    store_rows(ma_ref, start, d, m_new[:B])
                    store_rows(mb_ref, start, d, m_new[B:])
            return carry

        lax.fori_loop(0, s_len // (B * G), body, 0)


def dilated_prompt(proj):
    s = proj.shape[0]
    n_pairs = D_ATTN // LANES
    spec = lambda off: pl.BlockSpec((s, LANES), lambda h: (0, off // LANES + h))
    return pl.pallas_call(
        _dil_kernel, grid=(n_pairs,),
        in_specs=[spec(COL_Q), spec(COL_K), spec(COL_V)],
        out_specs=pl.BlockSpec((s, LANES), lambda h: (0, h)),
        out_shape=jax.ShapeDtypeStruct((s, D_ATTN), BF16),
        scratch_shapes=[pltpu.VMEM((s, LANES), F32)] * 4 + [pltpu.VMEM((2, 2 * DIL_BLK, 2 * DIL_BLK), F32)],
        compiler_params=_params(("parallel",)), name="dilated_prompt",
    )(proj, proj, proj)


SSD_SAMPLE_BB = 8


def _head_expand(x, seg_t):
    hi, mid, lo = _split3(x)
    return _dot(hi, seg_t) + _dot(mid, seg_t) + _dot(lo, seg_t)


def _ssd_sample_kernel(z_ref, xs_ref, bc_ref, dt_ref, sc_ref, st_ref, cw_ref, cb_ref, dtb_ref, alog_ref,
                       dskip_ref, ng_ref, segt_ref,
                       y_ref, ho_ref, convo_ref,
                       xs_s, t2_s, dec_s, b_s, c_s, col_s, yoff_s):
    step = pl.program_id(0)
    nb = xs_ref.shape[0]
    BB = SSD_SAMPLE_BB
    GW = 4 * HEAD_DIM
    HP = SSM_HEADS * HEAD_DIM

    @pl.when(step == 0)
    def _():
        xnew = jnp.concatenate([xs_ref[...], bc_ref[...]], axis=1)
        rows = [sc_ref[:, j * XBC_DIM:(j + 1) * XBC_DIM] for j in range(CONV_W - 1)] + [xnew]
        taps = [rows[j] * cw_ref[j:j + 1, :] for j in range(CONV_W)]
        act = _silu(cb_ref[...] + (((taps[0] + taps[1]) + taps[2]) + taps[3]))
        for j in range(CONV_W - 1):
            convo_ref[:, j * XBC_DIM:(j + 1) * XBC_DIM] = rows[j + 1]
        xs = act[:, 0:D_SSM]
        bm = act[:, D_SSM:D_SSM + SSM_GROUPS * SSM_STATE]
        cm = act[:, D_SSM + SSM_GROUPS * SSM_STATE:]
        dt = _softplus(dt_ref[...] + dtb_ref[...])
        dec = jnp.exp(dt * (-jnp.exp(alog_ref[...])))
        seg_t = segt_ref[...]
        u = _head_expand(dt, seg_t) * xs
        dec_e = _head_expand(dec, seg_t)
        xs_s[...] = xs
        dec_s[...] = dec_e
        b_s[...] = bm
        c_s[...] = cm
        for g in range(SSM_GROUPS):
            sl = slice(g * SSM_STATE, (g + 1) * SSM_STATE)
            bc = jnp.sum(bm[:, sl] * cm[:, sl], axis=-1, keepdims=True)
            t2_s[:, g * GW:(g + 1) * GW] = u[:, g * GW:(g + 1) * GW] * bc
        for j in range(HP // LANES):
            for src, base in ((u, 0), (dec_e, HP)):
                t = src[:, j * LANES:(j + 1) * LANES].T
                for k, part in enumerate(_split3(t)):
                    col_s[base + j * LANES:base + (j + 1) * LANES, k * nb:(k + 1) * nb] = part

    def per_seq(i, carry):
        b = step * BB + i
        ridx = lax.broadcasted_iota(jnp.int32, (3 * nb, SSM_STATE), 0)
        sel = ((ridx == b) | (ridx == b + nb) | (ridx == b + 2 * nb)).astype(BF16)
        cols = _dot(col_s[...], sel)
        h0 = st_ref[i].reshape(HP, SSM_STATE)
        brow = b_s[pl.ds(b, 1), :]
        c8 = c_s[pl.ds(pl.multiple_of(step * BB, BB), BB), :]
        pick = lax.broadcasted_iota(jnp.int32, (BB, GW), 0) == i
        for g in range(SSM_GROUPS):
            rs = slice(g * GW, (g + 1) * GW)
            ls = slice(g * SSM_STATE, (g + 1) * SSM_STATE)
            h0g = h0[rs]
            hn = cols[HP + g * GW:HP + (g + 1) * GW] * h0g + cols[rs] * brow[:, ls]
            ho_ref[i, 4 * g:4 * g + 4] = hn.reshape(4, HEAD_DIM, SSM_STATE)
            yo = lax.dot_general(c8[:, ls].astype(BF16), h0g.astype(BF16), (((1,), (1,)), ((), ())),
                                 preferred_element_type=F32)
            yoff_s[pl.ds(b, 1), rs] = jnp.sum(jnp.where(pick, yo, 0.0), axis=0, keepdims=True)
        return carry

    lax.fori_loop(0, BB, per_seq, 0)

    @pl.when(step == pl.num_programs(0) - 1)
    def _():
        xs = xs_s[...]
        y = dec_s[...] * yoff_s[...] + t2_s[...] + dskip_ref[...] * xs
        y = y * _silu(z_ref[...])
        for g in range(SSM_GROUPS):
            yg = y[:, g * GW:(g + 1) * GW]
            yg = yg * lax.rsqrt(jnp.mean(yg * yg, axis=-1, keepdims=True) + RMS_EPS)
            y_ref[:, g * GW:(g + 1) * GW] = (yg * ng_ref[:, g * GW:(g + 1) * GW]).astype(y_ref.dtype)


def _seg_ones(n_heads, width):
    c = jnp.arange(n_heads * width)[:, None] // width
    seg = (c == jnp.arange(LANES)[None, :]).astype(BF16)
    return seg, seg.T


def ssd_sample(proj, dt_raw, state_conv, state_ssm, conv_w, conv_b, dt_bias, a_log, d_skip, norm_g):
    nb = proj.shape[0]
    BB = SSD_SAMPLE_BB
    HP = SSM_HEADS * HEAD_DIM
    col = lambda off: off // 1024
    const = lambda shape: pl.BlockSpec(shape, lambda s: (0,) * len(shape))
    _, seg_t = _seg_ones(SSM_HEADS, HEAD_DIM)
    st_spec = pl.BlockSpec((BB, SSM_HEADS, HEAD_DIM, SSM_STATE), lambda s: (s, 0, 0, 0))
    y, h_new, conv_new = pl.pallas_call(
        _ssd_sample_kernel, grid=(nb // BB,),
        in_specs=[
            pl.BlockSpec((nb, D_SSM), lambda s: (0, col(COL_Z))),
            pl.BlockSpec((nb, D_SSM), lambda s: (0, col(COL_XBC))),
            pl.BlockSpec((nb, XBC_DIM - D_SSM), lambda s: (0, col(COL_XBC + D_SSM))),
            const((nb, LANES)), const((nb, (CONV_W - 1) * XBC_DIM)), st_spec,
            const((CONV_W, XBC_DIM)), const((1, XBC_DIM)), const((1, LANES)), const((1, LANES)),
            const((1, D_SSM)), const((1, D_SSM)), const((LANES, HP)),
        ],
        out_specs=[const((nb, D_SSM)), st_spec, const((nb, (CONV_W - 1) * XBC_DIM))],
        out_shape=[
            jax.ShapeDtypeStruct((nb, D_SSM), BF16),
            jax.ShapeDtypeStruct(state_ssm.shape, F32),
            jax.ShapeDtypeStruct((nb, (CONV_W - 1) * XBC_DIM), F32),
        ],
        scratch_shapes=[
            pltpu.VMEM((nb, HP), F32), pltpu.VMEM((nb, HP), F32), pltpu.VMEM((nb, HP), F32),
            pltpu.VMEM((nb, SSM_GROUPS * SSM_STATE), F32), pltpu.VMEM((nb, SSM_GROUPS * SSM_STATE), F32),
            pltpu.VMEM((2 * HP, 3 * nb), BF16), pltpu.VMEM((nb, HP), F32),
        ],
        compiler_params=_params(("arbitrary",)), name="ssd_sample",
    )(proj, proj, proj, dt_raw, state_conv.reshape(nb, (CONV_W - 1) * XBC_DIM), state_ssm, conv_w,
      conv_b.reshape(1, XBC_DIM), _pad_lanes(dt_bias), _pad_lanes(a_log),
      jnp.repeat(d_skip, HEAD_DIM).reshape(1, D_SSM), norm_g.reshape(1, D_SSM), seg_t)
    return y, h_new, conv_new.reshape(nb, CONV_W - 1, XBC_DIM)


WIN_UNIT_HEADS = 4
WIN_UNITS_PER_STEP = 3


def _win_unit_attention(n_branch, q4, kn4, vn4, cnt, kt, vt):
    rep = lambda t: jnp.stack([jnp.broadcast_to(t[h:h + 1, :], (SUBLANES, t.shape[1])) for h in range(t.shape[0])])
    q = rep(q4) * (HEAD_DIM ** -0.5)
    s = jnp.einsum("hqd,hdw->hqw", q.astype(BF16), kt.astype(BF16), preferred_element_type=F32)
    s = jnp.where(cnt > 0.0, s, NEG)
    s_new = jnp.sum(q * rep(kn4), axis=-1, keepdims=True)
    m = jnp.maximum(jnp.max(s, axis=-1, keepdims=True), s_new)
    p = cnt * jnp.exp(s - m)
    p_new = n_branch * jnp.exp(s_new - m)
    l = jnp.sum(p, axis=-1, keepdims=True) + p_new
    o = jnp.einsum("hqw,hdw->hqd", p.astype(BF16), vt.astype(BF16), preferred_element_type=F32)
    return (o + p_new * rep(vn4)) / l


def _ffn_stream_kernel(n_branch, x_ref, g2_ref, wg_ref, wu_ref, wd_ref, gf_ref,
                       q_ref, kn_ref, vn_ref, cnt_ref, kt_hbm, vt_hbm,
                       o_ref, oa_ref, hn_ref, kbuf, vbuf, sem):
    step = pl.program_id(0) * pl.num_programs(1) + pl.program_id(1)
    uh = WIN_UNIT_HEADS
    per_seq = kt_hbm.shape[1] // uh
    n_units = kt_hbm.shape[0] * per_seq

    def copies(u, slot):
        b = u // per_seq
        h0 = (u % per_seq) * uh
        return (pltpu.make_async_copy(kt_hbm.at[b, pl.ds(h0, uh)], kbuf.at[slot], sem.at[0, slot]),
                pltpu.make_async_copy(vt_hbm.at[b, pl.ds(h0, uh)], vbuf.at[slot], sem.at[1, slot]))

    @pl.when(step == 0)
    def _():
        for slot in range(WIN_UNITS_PER_STEP):
            for c in copies(slot, slot):
                c.start()

    cnt = cnt_ref[...][None]
    for slot in range(WIN_UNITS_PER_STEP):
        u = step * WIN_UNITS_PER_STEP + slot

        @pl.when(u < n_units)
        def _(u=u, slot=slot):
            for c in copies(u, slot):
                c.wait()
            b = u // per_seq
            h0 = (u % per_seq) * uh
            rows = pl.ds(h0, uh)
            o = _win_unit_attention(n_branch, q_ref[b, rows, :], kn_ref[b, rows, :], vn_ref[b, rows, :], cnt,
                                    kbuf[slot], vbuf[slot])
            for h in range(uh):
                oa_ref[b, pl.ds(h0 + h, 1), :] = o[h, 0:1, :]

            @pl.when(u + WIN_UNITS_PER_STEP < n_units)
            def _():
                for c in copies(u + WIN_UNITS_PER_STEP, slot):
                    c.start()

    _ffn_step(x_ref, g2_ref, wg_ref, wu_ref, wd_ref, gf_ref, o_ref, hn_ref)


def ffn_final_with_window_attention(x, g2, wg, wu, wd, gf, q, kn, vn, cache_k, cache_v, *, tm, tf):
    m, d = x.shape
    dff = wg.shape[1]
    nb, w_buf = cache_k.shape[0], cache_k.shape[1]
    grid = (m // tm, dff // tf)
    assert nb * (ATTN_HEADS // WIN_UNIT_HEADS) <= grid[0] * grid[1] * WIN_UNITS_PER_STEP
    pos = jnp.arange(w_buf)
    cnt = jnp.zeros((w_buf,), F32)
    for w, dil in DILATED_BRANCHES:
        cnt = cnt + ((pos >= w_buf - w) & ((w_buf - pos) % dil == 0)).astype(F32)
    const = lambda shape: pl.BlockSpec(shape, lambda i, f: (0,) * len(shape))
    small = const((nb, ATTN_HEADS, HEAD_DIM))
    unit = (WIN_UNITS_PER_STEP, WIN_UNIT_HEADS, HEAD_DIM, w_buf)
    y, att = pl.pallas_call(
        functools.partial(_ffn_stream_kernel, float(len(DILATED_BRANCHES))), grid=grid,
        in_specs=[
            pl.BlockSpec((tm, d), lambda i, f: (i, 0)),
            const((1, d)),
            pl.BlockSpec((d, tf), lambda i, f: (0, f)),
            pl.BlockSpec((d, tf), lambda i, f: (0, f)),
            pl.BlockSpec((tf, d), lambda i, f: (f, 0)),
            const((1, d)),
            small, small, small, const((1, w_buf)),
            pl.BlockSpec(memory_space=pl.ANY), pl.BlockSpec(memory_space=pl.ANY),
        ],
        out_specs=[pl.BlockSpec((tm, d), lambda i, f: (i, 0)), small],
        out_shape=[jax.ShapeDtypeStruct((m, d), F32), jax.ShapeDtypeStruct((nb, ATTN_HEADS, HEAD_DIM), F32)],
        scratch_shapes=[pltpu.VMEM((tm, d), BF16), pltpu.VMEM(unit, F32), pltpu.VMEM(unit, F32),
                        pltpu.SemaphoreType.DMA((2, WIN_UNITS_PER_STEP))],
        compiler_params=_params(("arbitrary", "arbitrary")), name="ffn_final_window_stream",
    )(x, g2.reshape(1, d), wg, wu, wd, gf.reshape(1, d), q, kn, vn, cnt.reshape(1, w_buf),
      jnp.transpose(cache_k, (0, 2, 3, 1)), jnp.transpose(cache_v, (0, 2, 3, 1)))
    return y, att.reshape(nb, D_ATTN)


MEM_SAMPLE_BB = 4


def _mem_sample_kernel(q_ref, k_ref, v_ref, o_ref):
    shape = (q_ref.shape[1], k_ref.shape[1])
    row = lax.broadcasted_iota(jnp.int32, shape, 0)
    col = lax.broadcasted_iota(jnp.int32, shape, 1)
    own = (col & (MEM_HEADS - 1)) == (row & (MEM_HEADS - 1))
    for i in range(q_ref.shape[0]):
        q = (q_ref[i] * (MEM_HEAD_DIM ** -0.5)).astype(BF16)
        s = lax.dot_general(q, k_ref[i].astype(BF16), (((1,), (1,)), ((), ())), preferred_element_type=F32)
        s = jnp.where(own, s, NEG)
        m = jnp.max(s, axis=-1, keepdims=True)
        p = jnp.where(own, jnp.exp(s - m), 0.0)
        l = jnp.sum(p, axis=-1, keepdims=True)
        o_ref[i] = _dot((p / l).astype(BF16), v_ref[i].astype(BF16))


def mem_attention_sample(q, cache_mk, cache_mv):
    nb = q.shape[0]
    assert MEM_HEADS & (MEM_HEADS - 1) == 0 and SUBLANES % MEM_HEADS == 0
    q8 = jnp.tile(q.reshape(nb, MEM_HEADS, MEM_HEAD_DIM), (1, SUBLANES // MEM_HEADS, 1))
    rows = MEM_LEN * MEM_HEADS
    bb = MEM_SAMPLE_BB
    small = pl.BlockSpec((bb, SUBLANES, MEM_HEAD_DIM), lambda b: (b, 0, 0))
    big = pl.BlockSpec((bb, rows, MEM_HEAD_DIM), lambda b: (b, 0, 0))
    out = pl.pallas_call(
        _mem_sample_kernel, grid=(nb // bb,), in_specs=[small, big, big], out_specs=small,
        out_shape=jax.ShapeDtypeStruct((nb, SUBLANES, MEM_HEAD_DIM), F32),
        compiler_params=_params(("parallel",)), name="mem_attention_sample",
    )(q8, cache_mk.reshape(nb, rows, MEM_HEAD_DIM), cache_mv.reshape(nb, rows, MEM_HEAD_DIM))
    return out[:, :MEM_HEADS].reshape(nb, D_MEM)


def _layer_tail(x, y_ssm, y_att, mem_attend, lw, gf, *, tm, tf):
    half = D_SSM
    x1 = matmul_residual([(y_ssm, lw["w_out"], 0), (y_att, lw["w_out"], 1)], x, tm=tm, tn=1024)
    assert y_ssm.shape[1] == half and y_att.shape[1] == half
    hq = rms_matmul(x1, lw["ln_mem_g"], lw["w_mq"], tm=tm, tn=D_MEM)
    x2 = matmul_residual([(mem_attend(hq), lw["w_mo"], 0)], x1, tm=tm, tn=1024)
    return ffn_final(x2, lw["ln2_g"], lw["w_gate"], lw["w_up"], lw["w_down"], gf, tm=min(tm, 512), tf=tf)


def kernel(x_prompt, x_sample, cache_win_k, cache_win_v, state_conv, state_ssm, cache_mem_k, cache_mem_v, mem_prompt, ln1_g, w_in, conv_w, conv_b, dt_bias, a_log, d_skip, ssm_norm_g, w_out, ln_mem_g, mem_norm_g, w_mq, w_mk, w_mv, w_mo, ln2_g, w_gate, w_up, w_down, ln_f_g):
    bp, s, d = x_prompt.shape
    nb, ts, _ = x_sample.shape
    depth = w_in.shape[0]
    assert bp == 1 and ts == 1 and depth == 1
    w_keep = min(DILATED_BRANCHES[-1][0], s)
    l = 0
    dt_lo = D_SSM + XBC_DIM
    dt_hi = dt_lo + SSM_HEADS
    w_main = jnp.concatenate([w_in[l][:, :dt_lo], w_in[l][:, dt_hi:]], axis=1).astype(BF16)
    w_dt = jnp.concatenate(_split3(jnp.pad(w_in[l][:, dt_lo:dt_hi], ((0, 0), (0, LANES - SSM_HEADS))))[:2], axis=1)
    lw = dict(w_out=w_out[l].astype(BF16), ln_mem_g=ln_mem_g[l], w_mq=w_mq[l].astype(BF16),
              w_mo=w_mo[l].astype(BF16), ln2_g=ln2_g[l], w_gate=w_gate[l].astype(BF16),
              w_up=w_up[l].astype(BF16), w_down=w_down[l].astype(BF16))
    w_mkv = jnp.concatenate([w_mk[l], w_mv[l]], axis=1).astype(BF16)
    ssm_w = (conv_w[l], conv_b[l], dt_bias[l], a_log[l], d_skip[l], ssm_norm_g[l])

    xs = x_sample.reshape(nb, d)
    proj_s, dt_s = rms_matmul(xs, ln1_g[l], w_main, tm=nb, tn=1024, wdt=w_dt)
    heads = lambda t: t.reshape(t.shape[0], ATTN_HEADS, HEAD_DIM)
    q_s, k_s, v_s = (heads(proj_s[:, c:c + D_ATTN]) for c in (COL_Q, COL_K, COL_V))

    xp = x_prompt.reshape(s, d)
    mkv = rms_matmul(mem_prompt.reshape(MEM_LEN, d), mem_norm_g[l], w_mkv, tm=MEM_LEN, tn=2 * D_MEM)
    proj, dt_raw = rms_matmul(xp, ln1_g[l], w_main, tm=1024, tn=1024, wdt=w_dt)
    y_ssm, h_p, conv_p = ssd_prompt(proj, dt_raw, *ssm_w)
    y_att = dilated_prompt(proj)
    x2 = mix_mem_fused(xp, y_ssm, y_att, lw["w_out"], lw["ln_mem_g"], lw["w_mq"], mkv, lw["w_mo"], tm=512)
    y_prompt, y_att_s = ffn_final_with_window_attention(
        x2, lw["ln2_g"], lw["w_gate"], lw["w_up"], lw["w_down"], ln_f_g,
        q_s, k_s, v_s, cache_win_k[l], cache_win_v[l], tm=512, tf=512)

    y_ssm_s, h_s, conv_s = ssd_sample(proj_s, dt_s, state_conv[l], state_ssm[l], *ssm_w)
    y_sample = _layer_tail(xs, y_ssm_s, y_att_s,
                           lambda hq: mem_attention_sample(hq, cache_mem_k[l], cache_mem_v[l]), lw, ln_f_g,
                           tm=nb, tf=512)

    return (
        y_prompt.reshape(bp, s, d),
        y_sample.reshape(nb, ts, d),
        heads(proj[s - w_keep:, COL_K:COL_V])[None, None],
        heads(proj[s - w_keep:, COL_V:])[None, None],
        conv_p[None, None],
        h_p[None, None],
        mkv[:, :D_MEM].reshape(1, 1, MEM_LEN, MEM_HEADS, MEM_HEAD_DIM),
        mkv[:, D_MEM:].reshape(1, 1, MEM_LEN, MEM_HEADS, MEM_HEAD_DIM),
        k_s[None, :, None],
        v_s[None, :, None],
        conv_s[None],
        h_s[None],
    )
```

```python
import functools

import jax
import jax.numpy as jnp
from jax import lax
from jax.experimental import pallas as pl
from jax.experimental.pallas import tpu as pltpu

F32 = jnp.float32
BF16 = jnp.bfloat16

D_MODEL = 2048
D_SSM = 1024
D_ATTN = 1024
HEAD_DIM = 64
SSM_HEADS = 16
SSM_GROUPS = 4
SSM_STATE = 128
CONV_W = 4
SSD_CHUNK = 128
XBC_DIM = D_SSM + 2 * SSM_GROUPS * SSM_STATE
ATTN_HEADS = 16
DILATED_BRANCHES = ((128, 1), (512, 4), (2048, 16))
MEM_LEN = 256
MEM_HEADS = 4
MEM_HEAD_DIM = 128
D_MEM = MEM_HEADS * MEM_HEAD_DIM
RMS_EPS = 1e-5
NEG = -1e30

LANES = 128
SUBLANES = 8
VMEM_BYTES_V7X = 64 * 1024 * 1024
VMEM_LIMIT = 56 * 1024 * 1024
VMEM_LIMIT_STREAM = 62 * 1024 * 1024

COL_Z = 0
COL_XBC = D_SSM
COL_Q = D_SSM + XBC_DIM
COL_K = COL_Q + D_ATTN
COL_V = COL_K + D_ATTN
D_PROJ = COL_V + D_ATTN


def _params(sem, vmem_limit=VMEM_LIMIT):
    return pltpu.CompilerParams(dimension_semantics=sem, vmem_limit_bytes=vmem_limit)


def _rms(x, g):
    ms = jnp.mean(x * x, axis=-1, keepdims=True)
    return x * lax.rsqrt(ms + RMS_EPS) * g


def _silu(x):
    return x * (0.5 * jnp.tanh(0.5 * x) + 0.5)


def _softplus(x):
    return jnp.maximum(x, 0.0) + jnp.log1p(jnp.exp(-jnp.abs(x)))


def _split3(x):
    hi = x.astype(BF16)
    r1 = x - hi.astype(F32)
    mid = r1.astype(BF16)
    lo = (r1 - mid.astype(F32)).astype(BF16)
    return hi, mid, lo


def _dot(a, b):
    return jnp.dot(a, b, preferred_element_type=F32)


def _rms_matmul_kernel(x_ref, g_ref, w_ref, o_ref, hn_ref):
    @pl.when(pl.program_id(1) == 0)
    def _():
        hn_ref[...] = _rms(x_ref[...], g_ref[...]).astype(BF16)

    o_ref[...] = _dot(hn_ref[...], w_ref[...])


def _rms_matmul_dt_kernel(n_a, x_ref, g_ref, wa_ref, wb_ref, wdt_ref, o_ref, dt_ref, hn_ref):
    j = pl.program_id(1)

    @pl.when(j == 0)
    def _():
        hn = _rms(x_ref[...], g_ref[...])
        hi = hn.astype(BF16)
        lo = (hn - hi.astype(F32)).astype(BF16)
        hn_ref[...] = hi
        both = _dot(hi, wdt_ref[...]) + _dot(lo, wdt_ref[...])
        dt_ref[...] = both[:, :LANES] + both[:, LANES:]

    @pl.when(j < n_a)
    def _():
        o_ref[...] = _dot(hn_ref[...], wa_ref[...])

    @pl.when(j >= n_a)
    def _():
        o_ref[...] = _dot(hn_ref[...], wb_ref[...])


def rms_matmul(x, g, w, *, tm, tn):
    m, k = x.shape
    n = w.shape[1]
    return pl.pallas_call(
        _rms_matmul_kernel, grid=(m // tm, n // tn),
        in_specs=[
            pl.BlockSpec((tm, k), lambda i, j: (i, 0)),
            pl.BlockSpec((1, k), lambda i, j: (0, 0)),
            pl.BlockSpec((k, tn), lambda i, j: (0, j)),
        ],
        out_specs=pl.BlockSpec((tm, tn), lambda i, j: (i, j)),
        out_shape=jax.ShapeDtypeStruct((m, n), F32), scratch_shapes=[pltpu.VMEM((tm, k), BF16)],
        compiler_params=_params(("parallel", "arbitrary")), name="rms_matmul",
    )(x, g.reshape(1, k), w)


def rms_matmul_dt(x, g, wa, wb, wdt, *, tm, tn):
    m, k = x.shape
    n_a, n_b = wa.shape[1] // tn, wb.shape[1] // tn
    return pl.pallas_call(
        functools.partial(_rms_matmul_dt_kernel, n_a), grid=(m // tm, n_a + n_b),
        in_specs=[
            pl.BlockSpec((tm, k), lambda i, j: (i, 0)),
            pl.BlockSpec((1, k), lambda i, j: (0, 0)),
            pl.BlockSpec((k, tn), lambda i, j: (0, jnp.minimum(j, n_a - 1))),
            pl.BlockSpec((k, tn), lambda i, j: (0, jnp.maximum(j - n_a, 0))),
            pl.BlockSpec((k, 2 * LANES), lambda i, j: (0, 0)),
        ],
        out_specs=[pl.BlockSpec((tm, tn), lambda i, j: (i, j)), pl.BlockSpec((tm, LANES), lambda i, j: (i, 0))],
        out_shape=[jax.ShapeDtypeStruct((m, (n_a + n_b) * tn), F32), jax.ShapeDtypeStruct((m, LANES), F32)],
        scratch_shapes=[pltpu.VMEM((tm, k), BF16)],
        compiler_params=_params(("parallel", "arbitrary")), name="rms_matmul_dt",
    )(x, g.reshape(1, k), wa, wb, wdt)


def _mm_res_kernel(n_pairs, *refs):
    res_ref = refs[2 * n_pairs]
    o_ref = refs[2 * n_pairs + 1]
    acc = res_ref[...]
    for i in range(n_pairs):
        acc = acc + _dot(refs[2 * i][...].astype(BF16), refs[2 * i + 1][...])
    o_ref[...] = acc


def matmul_residual(pairs, res, *, tm, tn):
    m, n = res.shape
    grid = (m // tm, n // tn)
    in_specs, args = [], []
    for a, w, kblk in pairs:
        k = a.shape[1]
        in_specs.append(pl.BlockSpec((tm, k), lambda i, j: (i, 0)))
        in_specs.append(pl.BlockSpec((k, tn), lambda i, j, kblk=kblk: (kblk, j)))
        args += [a, w]
    in_specs.append(pl.BlockSpec((tm, tn), lambda i, j: (i, j)))
    args.append(res)
    return pl.pallas_call(
        functools.partial(_mm_res_kernel, len(pairs)), grid=grid, in_specs=in_specs,
        out_specs=pl.BlockSpec((tm, tn), lambda i, j: (i, j)),
        out_shape=jax.ShapeDtypeStruct((m, n), F32),
        compiler_params=_params(("parallel", "parallel")), name="matmul_residual",
    )(*args)


def _ffn_step(x_ref, g2_ref, wg_ref, wu_ref, wd_ref, gf_ref, o_ref, hn_ref):
    f = pl.program_id(1)

    @pl.when(f == 0)
    def _():
        hn_ref[...] = _rms(x_ref[...], g2_ref[...]).astype(BF16)
        o_ref[...] = jnp.zeros_like(o_ref)

    hn = hn_ref[...]
    act = _silu(_dot(hn, wg_ref[...])) * _dot(hn, wu_ref[...])
    o_ref[...] += _dot(act.astype(BF16), wd_ref[...])

    @pl.when(f == pl.num_programs(1) - 1)
    def _():
        o_ref[...] = _rms(x_ref[...] + o_ref[...], gf_ref[...])


def _ffn_kernel(x_ref, g2_ref, wg_ref, wu_ref, wd_ref, gf_ref, o_ref, hn_ref):
    _ffn_step(x_ref, g2_ref, wg_ref, wu_ref, wd_ref, gf_ref, o_ref, hn_ref)


def ffn_final(x, g2, wg, wu, wd, gf, *, tm, tf):
    m, d = x.shape
    dff = wg.shape[1]
    grid = (m // tm, dff // tf)
    return pl.pallas_call(
        _ffn_kernel, grid=grid,
        in_specs=[
            pl.BlockSpec((tm, d), lambda i, f: (i, 0)),
            pl.BlockSpec((1, d), lambda i, f: (0, 0)),
            pl.BlockSpec((d, tf), lambda i, f: (0, f)),
            pl.BlockSpec((d, tf), lambda i, f: (0, f)),
            pl.BlockSpec((tf, d), lambda i, f: (f, 0)),
            pl.BlockSpec((1, d), lambda i, f: (0, 0)),
        ],
        out_specs=pl.BlockSpec((tm, d), lambda i, f: (i, 0)),
        out_shape=jax.ShapeDtypeStruct((m, d), F32),
        scratch_shapes=[pltpu.VMEM((tm, d), BF16)],
        compiler_params=_params(("parallel", "arbitrary")), name="ffn_final",
    )(x, g2.reshape(1, d), wg, wu, wd, gf.reshape(1, d))


def _mix_mem_kernel(x_ref, ys_ref, ya_ref, wo_ref, g_ref, wq_ref, mkv_ref, wm_ref, o_ref):
    half = ys_ref.shape[1]
    x1 = x_ref[...] + _dot(ys_ref[...], wo_ref[0:half, :]) + _dot(ya_ref[...], wo_ref[half:, :])
    hq = _dot(_rms(x1, g_ref[...]).astype(BF16), wq_ref[...])
    scale = MEM_HEAD_DIM ** -0.5
    outs = []
    for h in range(MEM_HEADS):
        sl = slice(h * MEM_HEAD_DIM, (h + 1) * MEM_HEAD_DIM)
        k = mkv_ref[:, sl].astype(BF16)
        v = mkv_ref[:, D_MEM + h * MEM_HEAD_DIM:D_MEM + (h + 1) * MEM_HEAD_DIM].astype(BF16)
        s = lax.dot_general(hq[:, sl].astype(BF16), k, (((1,), (1,)), ((), ())), preferred_element_type=F32) * scale
        p = jnp.exp(s - jnp.max(s, axis=-1, keepdims=True))
        p = p / jnp.sum(p, axis=-1, keepdims=True)
        outs.append(_dot(p.astype(BF16), v).astype(BF16))
    o_ref[...] = x1 + _dot(jnp.concatenate(outs, axis=1), wm_ref[...])


def mix_mem_fused(x, y_ssm, y_att, w_out, ln_mem_g, w_mq, mkv, w_mo, *, tm):
    m, d = x.shape
    half = y_ssm.shape[1]
    row = lambda width: pl.BlockSpec((tm, width), lambda i: (i, 0))
    whole = lambda a: pl.BlockSpec(a.shape, lambda i: (0, 0), pipeline_mode=pl.Buffered(1))
    g = ln_mem_g.reshape(1, d)
    return pl.pallas_call(
        _mix_mem_kernel, grid=(m // tm,),
        in_specs=[row(d), row(half), row(half), whole(w_out), whole(g), whole(w_mq), whole(mkv), whole(w_mo)],
        out_specs=row(d),
        out_shape=jax.ShapeDtypeStruct((m, d), F32),
        compiler_params=_params(("parallel",)), name="mix_mem_fused",
    )(x, y_ssm, y_att, w_out, g, w_mq, mkv, w_mo)


def _head_cols(x, g, width):
    rows = x.shape[0]
    parts = [jnp.broadcast_to(x[:, 4 * g + i:4 * g + i + 1], (rows, width)) for i in range(4)]
    return jnp.concatenate(parts, axis=1)


def _ssd_prompt_kernel(z_ref, xs_ref, bc_ref, dt_ref, cw_ref, cb_ref, dtb_ref, alog_ref, dskip_ref, ng_ref,
                       y_ref, hfin_ref, convo_ref, cbuf, st_ref):
    c = pl.program_id(0)
    L = SSD_CHUNK
    P = HEAD_DIM
    GW = 4 * P

    @pl.when(c == 0)
    def _():
        cbuf[0:SUBLANES, :] = jnp.zeros((SUBLANES, XBC_DIM), F32)
        st_ref[...] = jnp.zeros_like(st_ref)

    cbuf[SUBLANES:SUBLANES + L, 0:D_SSM] = xs_ref[...]
    cbuf[SUBLANES:SUBLANES + L, D_SSM:XBC_DIM] = bc_ref[...]
    up = cbuf[...]
    taps = [(up if j == CONV_W - 1 else pltpu.roll(up, CONV_W - 1 - j, 0))[SUBLANES:SUBLANES + L] * cw_ref[j:j + 1, :]
            for j in range(CONV_W)]
    conv = cb_ref[...] + (((taps[0] + taps[1]) + taps[2]) + taps[3])

    @pl.when(c == pl.num_programs(0) - 1)
    def _():
        convo_ref[...] = cbuf[pl.ds(L + SUBLANES - (CONV_W - 1), CONV_W - 1), :]

    cbuf[0:SUBLANES, :] = cbuf[L:L + SUBLANES, :]

    act = _silu(conv)
    xs = act[:, 0:D_SSM]
    dt = _softplus(dt_ref[...] + dtb_ref[...])
    a = dt * (-jnp.exp(alog_ref[...]))
    ri = lax.broadcasted_iota(jnp.int32, (L, L), 0)
    ci = lax.broadcasted_iota(jnp.int32, (L, L), 1)
    causal = ri >= ci
    acum = jnp.dot(causal.astype(F32), a, precision=lax.Precision.HIGHEST, preferred_element_type=F32)
    acum_t = acum.T
    dt_t = dt.T
    last = acum[L - 1:L, :]
    ea = jnp.exp(acum)
    wend = jnp.exp(last - acum) * dt
    cdec = jnp.exp(last)
    lane = lax.broadcasted_iota(jnp.int32, (L, GW), 1)
    head_lanes = [(lane >= i * P) & (lane < (i + 1) * P) for i in range(4)]

    ys = []
    for g in range(SSM_GROUPS):
        bg = act[:, D_SSM + g * SSM_STATE:D_SSM + (g + 1) * SSM_STATE]
        cg = act[:, D_SSM + SSM_GROUPS * SSM_STATE + g * SSM_STATE:
                 D_SSM + SSM_GROUPS * SSM_STATE + (g + 1) * SSM_STATE]
        bgb = bg.astype(BF16)
        cgb = cg.astype(BF16)
        cb = lax.dot_general(cgb, bgb, (((1,), (1,)), ((), ())), preferred_element_type=F32)
        xg = xs[:, g * GW:(g + 1) * GW]
        yd = jnp.zeros((L, GW), F32)
        for i in range(4):
            h = 4 * g + i
            seg = acum[:, h:h + 1] - acum_t[h:h + 1, :]
            decay = jnp.exp(jnp.where(causal, seg, NEG))
            w = (cb * decay * dt_t[h:h + 1, :]).astype(BF16)
            xm = jnp.where(head_lanes[i], xg, 0.0).astype(BF16)
            yd = yd + _dot(w, xm)
        st = st_ref[g]
        yoff = _dot(cgb, st.astype(BF16)) * _head_cols(ea, g, P)
        ys.append(yd + yoff)
        xw = (xg * _head_cols(wend, g, P)).astype(BF16)
        new = _dot(bg.T.astype(BF16), xw)
        cd = jnp.concatenate([jnp.broadcast_to(cdec[:, 4 * g + i:4 * g + i + 1], (1, P)) for i in range(4)], axis=1)
        st_ref[g] = st * cd + new

    y = jnp.concatenate(ys, axis=1) + dskip_ref[...] * xs
    y = y * _silu(z_ref[...])
    outs = []
    for g in range(SSM_GROUPS):
        yg = y[:, g * GW:(g + 1) * GW]
        outs.append(yg * lax.rsqrt(jnp.mean(yg * yg, axis=-1, keepdims=True) + RMS_EPS))
    y_ref[...] = (jnp.concatenate(outs, axis=1) * ng_ref[...]).astype(y_ref.dtype)

    @pl.when(c == pl.num_programs(0) - 1)
    def _():
        for g in range(SSM_GROUPS):
            stg = st_ref[g]
            for i in range(4):
                hfin_ref[4 * g + i] = stg[:, i * P:(i + 1) * P].T


def _pad_lanes(v):
    return jnp.pad(v.astype(F32), (0, LANES - v.shape[0])).reshape(1, LANES)


def ssd_prompt(proj, dt_raw, conv_w, conv_b, dt_bias, a_log, d_skip, norm_g):
    s = proj.shape[0]
    L = SSD_CHUNK
    nblk = D_SSM // 1024
    col = lambda off: off // 1024
    const = lambda shape: pl.BlockSpec(shape, lambda c: (0,) * len(shape))
    return pl.pallas_call(
        _ssd_prompt_kernel, grid=(s // L,),
        in_specs=[
            pl.BlockSpec((L, D_SSM), lambda c: (c, col(COL_Z))),
            pl.BlockSpec((L, D_SSM), lambda c: (c, col(COL_XBC))),
            pl.BlockSpec((L, XBC_DIM - D_SSM), lambda c: (c, col(COL_XBC + D_SSM))),
            pl.BlockSpec((L, LANES), lambda c: (c, 0)),
            const((CONV_W, XBC_DIM)), const((1, XBC_DIM)), const((1, LANES)), const((1, LANES)),
            const((1, D_SSM)), const((1, D_SSM)),
        ],
        out_specs=[
            pl.BlockSpec((L, D_SSM), lambda c: (c, 0)),
            const((SSM_HEADS, HEAD_DIM, SSM_STATE)),
            const((CONV_W - 1, XBC_DIM)),
        ],
        out_shape=[
            jax.ShapeDtypeStruct((s, D_SSM), BF16),
            jax.ShapeDtypeStruct((SSM_HEADS, HEAD_DIM, SSM_STATE), F32),
            jax.ShapeDtypeStruct((CONV_W - 1, XBC_DIM), F32),
        ],
        scratch_shapes=[
            pltpu.VMEM((L + SUBLANES, XBC_DIM), F32),
            pltpu.VMEM((SSM_GROUPS, SSM_STATE, 4 * HEAD_DIM), F32),
        ],
        compiler_params=_params(("arbitrary",)), name="ssd_prompt",
    )(proj, proj, proj, dt_raw, conv_w, conv_b.reshape(1, XBC_DIM), _pad_lanes(dt_bias), _pad_lanes(a_log),
      jnp.repeat(d_skip, HEAD_DIM).reshape(1, D_SSM), norm_g.reshape(1, D_SSM))


DIL_BLK = 128
DIL_GROUP = 8


def _dil_kernel(q_ref, k_ref, v_ref, o_ref, acc_ref, l_ref, ma_ref, mb_ref, bias_ref):
    s_len = q_ref.shape[0]
    B = DIL_BLK
    scale = HEAD_DIM ** -0.5
    lane = lax.broadcasted_iota(jnp.int32, (B, LANES), 1)
    head_a = lane < HEAD_DIM
    qi = lax.broadcasted_iota(jnp.int32, (2 * B, 2 * B), 0) & (B - 1)
    ki = lax.broadcasted_iota(jnp.int32, (2 * B, 2 * B), 1)
    band = (ki >= qi) & (ki <= qi + B)
    bias_ref[0] = jnp.where(band & (ki >= B), 0.0, NEG)
    bias_ref[1] = jnp.where(band, 0.0, NEG)
    ones = jnp.ones((2 * B, LANES), BF16)

    def comb(x):
        return jnp.where(head_a, x[:B], x[B:])

    def rows(ref, start, d):
        if d == 1:
            return ref[pl.ds(start, B), :]
        return ref[pl.ds(start, B, stride=d), :]

    def store_rows(ref, start, d, val):
        if d == 1:
            ref[pl.ds(start, B), :] = val
        else:
            ref[pl.ds(start, B, stride=d), :] = val

    order = sorted(DILATED_BRANCHES, key=lambda wd: -wd[1])
    assert order[-1][1] == 1
    for bi, (w, d) in enumerate(order):
        assert w // d == B
        first_branch = bi == 0
        last_branch = bi == len(order) - 1
        per_res = s_len // (B * d)
        G = min(DIL_GROUP, per_res)
        assert per_res % G == 0

        def body(t, carry, d=d, first_branch=first_branch, last_branch=last_branch, per_res=per_res, G=G):
            r = t // (per_res // G)
            n0 = (t % (per_res // G)) * G
            starts = [(n0 + j) * (B * d) + r for j in range(G)]
            prev0 = jnp.maximum(n0 - 1, 0) * (B * d) + r
            if d == 1:
                starts = [pl.multiple_of(st, B) for st in starts]
                prev0 = pl.multiple_of(prev0, B)
            kb = [rows(k_ref, st, d).astype(BF16) for st in [prev0] + starts]
            vb = [rows(v_ref, st, d).astype(BF16) for st in [prev0] + starts]
            for j in range(G):
                start = starts[j]
                q = rows(q_ref, start, d) * scale
                qs = jnp.concatenate([jnp.where(head_a, q, 0.0), jnp.where(head_a, 0.0, q)], axis=0).astype(BF16)
                kk = jnp.concatenate([kb[j], kb[j + 1]], axis=0)
                s = lax.dot_general(qs, kk, (((1,), (1,)), ((), ())), preferred_element_type=F32)
                s = s + (bias_ref[jnp.minimum(n0, 1)] if j == 0 else bias_ref[1])
                m_row = jnp.broadcast_to(jnp.max(s, axis=-1, keepdims=True), (2 * B, LANES))
                if first_branch:
                    m_new = m_row
                else:
                    m_old = jnp.concatenate([rows(ma_ref, start, d), rows(mb_ref, start, d)], axis=0)
                    m_new = jnp.maximum(m_old, m_row)
                p = jnp.exp(s - jnp.concatenate([m_new, m_new], axis=1)).astype(BF16)
                pv = _dot(p, jnp.concatenate([jnp.concatenate([vb[j], vb[j + 1]], axis=0), ones], axis=1))
                l_new = comb(pv[:, LANES:])
                pv = comb(pv[:, :LANES])
                if first_branch:
                    acc, l_b = pv, l_new
                else:
                    alpha = comb(jnp.exp(m_old - m_new))
                    acc = rows(acc_ref, start, d) * alpha + pv
                    l_b = rows(l_ref, start, d) * alpha + l_new
                if last_branch:
                    o_ref[pl.ds(start, B), :] = (acc / l_b).astype(o_ref.dtype)
                else:
                    store_rows(acc_ref, start, d, acc)
                    store_rows(l_ref, start, d, l_b)
                    store_rows(ma_ref, start, d, m_new[:B])
                    store_rows(mb_ref, start, d, m_new[B:])
            return carry

        lax.fori_loop(0, s_len // (B * G), body, 0)


def dilated_prompt(proj):
    s = proj.shape[0]
    n_pairs = D_ATTN // LANES
    spec = lambda off: pl.BlockSpec((s, LANES), lambda h: (0, off // LANES + h))
    return pl.pallas_call(
        _dil_kernel, grid=(n_pairs,),
        in_specs=[spec(COL_Q), spec(COL_K), spec(COL_V)],
        out_specs=pl.BlockSpec((s, LANES), lambda h: (0, h)),
        out_shape=jax.ShapeDtypeStruct((s, D_ATTN), BF16),
        scratch_shapes=[pltpu.VMEM((s, LANES), F32)] * 4 + [pltpu.VMEM((2, 2 * DIL_BLK, 2 * DIL_BLK), F32)],
        compiler_params=_params(("parallel",)), name="dilated_prompt",
    )(proj, proj, proj)


SSD_SAMPLE_BB = 8


def _head_expand(x, seg_t):
    hi, mid, lo = _split3(x)
    return _dot(hi, seg_t) + _dot(mid, seg_t) + _dot(lo, seg_t)


def _ssd_sample_kernel(z_ref, xs_ref, bc_ref, dt_ref, sc_ref, st_ref, cw_ref, cb_ref, dtb_ref, alog_ref,
                       dskip_ref, ng_ref, segt_ref,
                       y_ref, ho_ref, convo_ref,
                       xs_s, t2_s, dec_s, b_s, c_s, col_s, yoff_s):
    step = pl.program_id(0)
    nb = xs_ref.shape[0]
    BB = SSD_SAMPLE_BB
    GW = 4 * HEAD_DIM
    HP = SSM_HEADS * HEAD_DIM

    @pl.when(step == 0)
    def _():
        xnew = jnp.concatenate([xs_ref[...], bc_ref[...]], axis=1)
        rows = [sc_ref[:, j * XBC_DIM:(j + 1) * XBC_DIM] for j in range(CONV_W - 1)] + [xnew]
        taps = [rows[j] * cw_ref[j:j + 1, :] for j in range(CONV_W)]
        act = _silu(cb_ref[...] + (((taps[0] + taps[1]) + taps[2]) + taps[3]))
        for j in range(CONV_W - 1):
            convo_ref[:, j * XBC_DIM:(j + 1) * XBC_DIM] = rows[j + 1]
        xs = act[:, 0:D_SSM]
        bm = act[:, D_SSM:D_SSM + SSM_GROUPS * SSM_STATE]
        cm = act[:, D_SSM + SSM_GROUPS * SSM_STATE:]
        dt = _softplus(dt_ref[...] + dtb_ref[...])
        dec = jnp.exp(dt * (-jnp.exp(alog_ref[...])))
        seg_t = segt_ref[...]
        u = _head_expand(dt, seg_t) * xs
        dec_e = _head_expand(dec, seg_t)
        xs_s[...] = xs
        dec_s[...] = dec_e
        b_s[...] = bm
        c_s[...] = cm
        for g in range(SSM_GROUPS):
            sl = slice(g * SSM_STATE, (g + 1) * SSM_STATE)
            bc = jnp.sum(bm[:, sl] * cm[:, sl], axis=-1, keepdims=True)
            t2_s[:, g * GW:(g + 1) * GW] = u[:, g * GW:(g + 1) * GW] * bc
        for j in range(HP // LANES):
            for src, base in ((u, 0), (dec_e, HP)):
                t = src[:, j * LANES:(j + 1) * LANES].T
                for k, part in enumerate(_split3(t)):
                    col_s[base + j * LANES:base + (j + 1) * LANES, k * nb:(k + 1) * nb] = part

    def per_pair(ip, carry):
        b0 = step * BB + 2 * ip
        ridx = lax.broadcasted_iota(jnp.int32, (3 * nb, 2 * SSM_STATE), 0)
        lane = lax.broadcasted_iota(jnp.int32, (3 * nb, 2 * SSM_STATE), 1)
        tgt = b0 + (lane >= SSM_STATE).astype(jnp.int32)
        sel = ((ridx == tgt) | (ridx == tgt + nb) | (ridx == tgt + 2 * nb)).astype(BF16)
        cols2 = _dot(col_s[...], sel)
        c8 = c_s[pl.ds(pl.multiple_of(step * BB, BB), BB), :]
        for j in range(2):
            i = 2 * ip + j
            b = b0 + j
            cols = cols2[:, j * SSM_STATE:(j + 1) * SSM_STATE]
            h0 = st_ref[i].reshape(HP, SSM_STATE)
            brow = b_s[pl.ds(b, 1), :]
            pick = lax.broadcasted_iota(jnp.int32, (BB, GW), 0) == i
            for g in range(SSM_GROUPS):
                rs = slice(g * GW, (g + 1) * GW)
                ls = slice(g * SSM_STATE, (g + 1) * SSM_STATE)
                h0g = h0[rs]
                hn = cols[HP + g * GW:HP + (g + 1) * GW] * h0g + cols[rs] * brow[:, ls]
                ho_ref[i, 4 * g:4 * g + 4] = hn.reshape(4, HEAD_DIM, SSM_STATE)
                yo = lax.dot_general(c8[:, ls].astype(BF16), h0g.astype(BF16), (((1,), (1,)), ((), ())),
                                     preferred_element_type=F32)
                yoff_s[pl.ds(b, 1), rs] = jnp.sum(jnp.where(pick, yo, 0.0), axis=0, keepdims=True)
        return carry

    lax.fori_loop(0, BB // 2, per_pair, 0)

    @pl.when(step == pl.num_programs(0) - 1)
    def _():
        xs = xs_s[...]
        y = dec_s[...] * yoff_s[...] + t2_s[...] + dskip_ref[...] * xs
        y = y * _silu(z_ref[...])
        for g in range(SSM_GROUPS):
            yg = y[:, g * GW:(g + 1) * GW]
            yg = yg * lax.rsqrt(jnp.mean(yg * yg, axis=-1, keepdims=True) + RMS_EPS)
            y_ref[:, g * GW:(g + 1) * GW] = (yg * ng_ref[:, g * GW:(g + 1) * GW]).astype(y_ref.dtype)


def _seg_ones(n_heads, width):
    c = jnp.arange(n_heads * width)[:, None] // width
    seg = (c == jnp.arange(LANES)[None, :]).astype(BF16)
    return seg, seg.T


def ssd_sample(proj, dt_raw, state_conv, state_ssm, conv_w, conv_b, dt_bias, a_log, d_skip, norm_g):
    nb = proj.shape[0]
    BB = SSD_SAMPLE_BB
    HP = SSM_HEADS * HEAD_DIM
    col = lambda off: off // 1024
    const = lambda shape: pl.BlockSpec(shape, lambda s: (0,) * len(shape))
    _, seg_t = _seg_ones(SSM_HEADS, HEAD_DIM)
    st_spec = pl.BlockSpec((BB, SSM_HEADS, HEAD_DIM, SSM_STATE), lambda s: (s, 0, 0, 0))
    y, h_new, conv_new = pl.pallas_call(
        _ssd_sample_kernel, grid=(nb // BB,),
        in_specs=[
            pl.BlockSpec((nb, D_SSM), lambda s: (0, col(COL_Z))),
            pl.BlockSpec((nb, D_SSM), lambda s: (0, col(COL_XBC))),
            pl.BlockSpec((nb, XBC_DIM - D_SSM), lambda s: (0, col(COL_XBC + D_SSM))),
            const((nb, LANES)), const((nb, (CONV_W - 1) * XBC_DIM)), st_spec,
            const((CONV_W, XBC_DIM)), const((1, XBC_DIM)), const((1, LANES)), const((1, LANES)),
            const((1, D_SSM)), const((1, D_SSM)), const((LANES, HP)),
        ],
        out_specs=[const((nb, D_SSM)), st_spec, const((nb, (CONV_W - 1) * XBC_DIM))],
        out_shape=[
            jax.ShapeDtypeStruct((nb, D_SSM), BF16),
            jax.ShapeDtypeStruct(state_ssm.shape, F32),
            jax.ShapeDtypeStruct((nb, (CONV_W - 1) * XBC_DIM), F32),
        ],
        scratch_shapes=[
            pltpu.VMEM((nb, HP), F32), pltpu.VMEM((nb, HP), F32), pltpu.VMEM((nb, HP), F32),
            pltpu.VMEM((nb, SSM_GROUPS * SSM_STATE), F32), pltpu.VMEM((nb, SSM_GROUPS * SSM_STATE), F32),
            pltpu.VMEM((2 * HP, 3 * nb), BF16), pltpu.VMEM((nb, HP), F32),
        ],
        compiler_params=_params(("arbitrary",)), name="ssd_sample",
    )(proj, proj, proj, dt_raw, state_conv.reshape(nb, (CONV_W - 1) * XBC_DIM), state_ssm, conv_w,
      conv_b.reshape(1, XBC_DIM), _pad_lanes(dt_bias), _pad_lanes(a_log),
      jnp.repeat(d_skip, HEAD_DIM).reshape(1, D_SSM), norm_g.reshape(1, D_SSM), seg_t)
    return y, h_new, conv_new.reshape(nb, CONV_W - 1, XBC_DIM)


WIN_UNIT_HEADS = 4
WIN_UNITS_PER_STEP = 3


def _win_unit_attention(n_branch, q4, kn4, vn4, cnt, kt, vt):
    rep = lambda t: jnp.stack([jnp.broadcast_to(t[h:h + 1, :], (SUBLANES, t.shape[1])) for h in range(t.shape[0])])
    q = rep(q4) * (HEAD_DIM ** -0.5)
    s = jnp.einsum("hqd,hdw->hqw", q.astype(BF16), kt.astype(BF16), preferred_element_type=F32)
    s = jnp.where(cnt > 0.0, s, NEG)
    s_new = jnp.sum(q * rep(kn4), axis=-1, keepdims=True)
    m = jnp.maximum(jnp.max(s, axis=-1, keepdims=True), s_new)
    p = cnt * jnp.exp(s - m)
    p_new = n_branch * jnp.exp(s_new - m)
    l = jnp.sum(p, axis=-1, keepdims=True) + p_new
    o = jnp.einsum("hqw,hdw->hqd", p.astype(BF16), vt.astype(BF16), preferred_element_type=F32)
    return (o + p_new * rep(vn4)) / l


def _ffn_stream_kernel(n_branch, x_ref, g2_ref, wg_ref, wu_ref, wd_ref, gf_ref,
                       q_ref, kn_ref, vn_ref, cnt_ref, kt_hbm, vt_hbm,
                       o_ref, oa_ref, hn_ref, kbuf, vbuf, sem):
    step = pl.program_id(0) * pl.num_programs(1) + pl.program_id(1)
    uh = WIN_UNIT_HEADS
    per_seq = kt_hbm.shape[1] // uh
    n_units = kt_hbm.shape[0] * per_seq

    def copies(u, slot):
        b = u // per_seq
        h0 = (u % per_seq) * uh
        return (pltpu.make_async_copy(kt_hbm.at[b, pl.ds(h0, uh)], kbuf.at[slot], sem.at[0, slot]),
                pltpu.make_async_copy(vt_hbm.at[b, pl.ds(h0, uh)], vbuf.at[slot], sem.at[1, slot]))

    @pl.when(step == 0)
    def _():
        for slot in range(WIN_UNITS_PER_STEP):
            for c in copies(slot, slot):
                c.start()

    cnt = cnt_ref[...][None]
    for slot in range(WIN_UNITS_PER_STEP):
        u = step * WIN_UNITS_PER_STEP + slot

        @pl.when(u < n_units)
        def _(u=u, slot=slot):
            for c in copies(u, slot):
                c.wait()
            b = u // per_seq
            h0 = (u % per_seq) * uh
            rows = pl.ds(h0, uh)
            o = _win_unit_attention(n_branch, q_ref[b, rows, :], kn_ref[b, rows, :], vn_ref[b, rows, :], cnt,
                                    kbuf[slot], vbuf[slot])
            for h in range(uh):
                oa_ref[b, pl.ds(h0 + h, 1), :] = o[h, 0:1, :]

            @pl.when(u + WIN_UNITS_PER_STEP < n_units)
            def _():
                for c in copies(u + WIN_UNITS_PER_STEP, slot):
                    c.start()

    _ffn_step(x_ref, g2_ref, wg_ref, wu_ref, wd_ref, gf_ref, o_ref, hn_ref)


def ffn_final_with_window_attention(x, g2, wg, wu, wd, gf, q, kn, vn, cache_k, cache_v, *, tm, tf):
    m, d = x.shape
    dff = wg.shape[1]
    nb, w_buf = cache_k.shape[0], cache_k.shape[1]
    grid = (m // tm, dff // tf)
    assert nb * (ATTN_HEADS // WIN_UNIT_HEADS) <= grid[0] * grid[1] * WIN_UNITS_PER_STEP
    pos = jnp.arange(w_buf)
    cnt = jnp.zeros((w_buf,), F32)
    for w, dil in DILATED_BRANCHES:
        cnt = cnt + ((pos >= w_buf - w) & ((w_buf - pos) % dil == 0)).astype(F32)
    const = lambda shape: pl.BlockSpec(shape, lambda i, f: (0,) * len(shape))
    small = const((nb, ATTN_HEADS, HEAD_DIM))
    unit = (WIN_UNITS_PER_STEP, WIN_UNIT_HEADS, HEAD_DIM, w_buf)
    y, att = pl.pallas_call(
        functools.partial(_ffn_stream_kernel, float(len(DILATED_BRANCHES))), grid=grid,
        in_specs=[
            pl.BlockSpec((tm, d), lambda i, f: (i, 0), pipeline_mode=pl.Buffered(1)),
            const((1, d)),
            pl.BlockSpec((d, tf), lambda i, f: (0, f)),
            pl.BlockSpec((d, tf), lambda i, f: (0, f)),
            pl.BlockSpec((tf, d), lambda i, f: (f, 0)),
            const((1, d)),
            small, small, small, const((1, w_buf)),
            pl.BlockSpec(memory_space=pl.ANY), pl.BlockSpec(memory_space=pl.ANY),
        ],
        out_specs=[pl.BlockSpec((tm, d), lambda i, f: (i, 0)), small],
        out_shape=[jax.ShapeDtypeStruct((m, d), F32), jax.ShapeDtypeStruct((nb, ATTN_HEADS, HEAD_DIM), F32)],
        scratch_shapes=[pltpu.VMEM((tm, d), BF16), pltpu.VMEM(unit, F32), pltpu.VMEM(unit, F32),
                        pltpu.SemaphoreType.DMA((2, WIN_UNITS_PER_STEP))],
        compiler_params=_params(("arbitrary", "arbitrary"), VMEM_LIMIT_STREAM), name="ffn_final_window_stream",
    )(x, g2.reshape(1, d), wg, wu, wd, gf.reshape(1, d), q, kn, vn, cnt.reshape(1, w_buf),
      jnp.transpose(cache_k, (0, 2, 3, 1)), jnp.transpose(cache_v, (0, 2, 3, 1)))
    return y, att.reshape(nb, D_ATTN)


MEM_SAMPLE_BB = 4


def _mem_sample_kernel(q_ref, k_ref, v_ref, o_ref):
    shape = (q_ref.shape[1], k_ref.shape[1])
    row = lax.broadcasted_iota(jnp.int32, shape, 0)
    col = lax.broadcasted_iota(jnp.int32, shape, 1)
    own = (col & (MEM_HEADS - 1)) == (row & (MEM_HEADS - 1))
    for i in range(q_ref.shape[0]):
        q = (q_ref[i] * (MEM_HEAD_DIM ** -0.5)).astype(BF16)
        s = lax.dot_general(q, k_ref[i].astype(BF16), (((1,), (1,)), ((), ())), preferred_element_type=F32)
        s = jnp.where(own, s, NEG)
        m = jnp.max(s, axis=-1, keepdims=True)
        p = jnp.where(own, jnp.exp(s - m), 0.0)
        l = jnp.sum(p, axis=-1, keepdims=True)
        o_ref[i] = _dot((p / l).astype(BF16), v_ref[i].astype(BF16))


def mem_attention_sample(q, cache_mk, cache_mv):
    nb = q.shape[0]
    assert MEM_HEADS & (MEM_HEADS - 1) == 0 and SUBLANES % MEM_HEADS == 0
    q8 = jnp.tile(q.reshape(nb, MEM_HEADS, MEM_HEAD_DIM), (1, SUBLANES // MEM_HEADS, 1))
    rows = MEM_LEN * MEM_HEADS
    bb = MEM_SAMPLE_BB
    small = pl.BlockSpec((bb, SUBLANES, MEM_HEAD_DIM), lambda b: (b, 0, 0))
    big = pl.BlockSpec((bb, rows, MEM_HEAD_DIM), lambda b: (b, 0, 0))
    out = pl.pallas_call(
        _mem_sample_kernel, grid=(nb // bb,), in_specs=[small, big, big], out_specs=small,
        out_shape=jax.ShapeDtypeStruct((nb, SUBLANES, MEM_HEAD_DIM), F32),
        compiler_params=_params(("parallel",)), name="mem_attention_sample",
    )(q8, cache_mk.reshape(nb, rows, MEM_HEAD_DIM), cache_mv.reshape(nb, rows, MEM_HEAD_DIM))
    return out[:, :MEM_HEADS].reshape(nb, D_MEM)


def _layer_tail(x, y_ssm, y_att, mem_attend, lw, gf, *, tm, tf):
    half = D_SSM
    x1 = matmul_residual([(y_ssm, lw["w_out"], 0), (y_att, lw["w_out"], 1)], x, tm=tm, tn=1024)
    assert y_ssm.shape[1] == half and y_att.shape[1] == half
    hq = rms_matmul(x1, lw["ln_mem_g"], lw["w_mq"], tm=tm, tn=D_MEM)
    x2 = matmul_residual([(mem_attend(hq), lw["w_mo"], 0)], x1, tm=tm, tn=1024)
    return ffn_final(x2, lw["ln2_g"], lw["w_gate"], lw["w_up"], lw["w_down"], gf, tm=min(tm, 512), tf=tf)


def kernel(x_prompt, x_sample, cache_win_k, cache_win_v, state_conv, state_ssm, cache_mem_k, cache_mem_v, mem_prompt, ln1_g, w_in, conv_w, conv_b, dt_bias, a_log, d_skip, ssm_norm_g, w_out, ln_mem_g, mem_norm_g, w_mq, w_mk, w_mv, w_mo, ln2_g, w_gate, w_up, w_down, ln_f_g):
    bp, s, d = x_prompt.shape
    nb, ts, _ = x_sample.shape
    depth = w_in.shape[0]
    assert bp == 1 and ts == 1 and depth == 1
    w_keep = min(DILATED_BRANCHES[-1][0], s)
    l = 0
    dt_lo = D_SSM + XBC_DIM
    dt_hi = dt_lo + SSM_HEADS
    w_zx = w_in[l][:, :dt_lo].astype(BF16)
    w_qkv = w_in[l][:, dt_hi:].astype(BF16)
    w_dt = jnp.concatenate(_split3(jnp.pad(w_in[l][:, dt_lo:dt_hi], ((0, 0), (0, LANES - SSM_HEADS))))[:2], axis=1)
    lw = dict(w_out=w_out[l].astype(BF16), ln_mem_g=ln_mem_g[l], w_mq=w_mq[l].astype(BF16),
              w_mo=w_mo[l].astype(BF16), ln2_g=ln2_g[l], w_gate=w_gate[l].astype(BF16),
              w_up=w_up[l].astype(BF16), w_down=w_down[l].astype(BF16))
    w_mkv = jnp.concatenate([w_mk[l], w_mv[l]], axis=1).astype(BF16)
    ssm_w = (conv_w[l], conv_b[l], dt_bias[l], a_log[l], d_skip[l], ssm_norm_g[l])

    xs = x_sample.reshape(nb, d)
    proj_s, dt_s = rms_matmul_dt(xs, ln1_g[l], w_zx, w_qkv, w_dt, tm=nb, tn=1024)
    heads = lambda t: t.reshape(t.shape[0], ATTN_HEADS, HEAD_DIM)
    q_s, k_s, v_s = (heads(proj_s[:, c:c + D_ATTN]) for c in (COL_Q, COL_K, COL_V))

    xp = x_prompt.reshape(s, d)
    mkv = rms_matmul(mem_prompt.reshape(MEM_LEN, d), mem_norm_g[l], w_mkv, tm=MEM_LEN, tn=2 * D_MEM)
    proj, dt_raw = rms_matmul_dt(xp, ln1_g[l], w_zx, w_qkv, w_dt, tm=1024, tn=1024)
    y_ssm, h_p, conv_p = ssd_prompt(proj, dt_raw, *ssm_w)
    y_att = dilated_prompt(proj)
    x2 = mix_mem_fused(xp, y_ssm, y_att, lw["w_out"], lw["ln_mem_g"], lw["w_mq"], mkv, lw["w_mo"], tm=512)
    y_prompt, y_att_s = ffn_final_with_window_attention(
        x2, lw["ln2_g"], lw["w_gate"], lw["w_up"], lw["w_down"], ln_f_g,
        q_s, k_s, v_s, cache_win_k[l], cache_win_v[l], tm=1024, tf=256)

    y_ssm_s, h_s, conv_s = ssd_sample(proj_s, dt_s, state_conv[l], state_ssm[l], *ssm_w)
    y_sample = _layer_tail(xs, y_ssm_s, y_att_s,
                           lambda hq: mem_attention_sample(hq, cache_mem_k[l], cache_mem_v[l]), lw, ln_f_g,
                           tm=nb, tf=512)

    return (
        y_prompt.reshape(bp, s, d),
        y_sample.reshape(nb, ts, d),
        heads(proj[s - w_keep:, COL_K:COL_V])[None, None],
        heads(proj[s - w_keep:, COL_V:])[None, None],
        conv_p[None, None],
        h_p[None, None],
        mkv[:, :D_MEM].reshape(1, 1, MEM_LEN, MEM_HEADS, MEM_HEAD_DIM),
        mkv[:, D_MEM:].reshape(1, 1, MEM_LEN, MEM_HEADS, MEM_HEAD_DIM),
        k_s[None, :, None],
        v_s[None, :, None],
        conv_s[None],
        h_s[None],
    )
```

```python
import functools

import jax
import jax.numpy as jnp
from jax import lax
from jax.experimental import pallas as pl
from jax.experimental.pallas import tpu as pltpu

F32 = jnp.float32
BF16 = jnp.bfloat16

D_MODEL = 2048
D_SSM = 1024
D_ATTN = 1024
HEAD_DIM = 64
SSM_HEADS = 16
SSM_GROUPS = 4
SSM_STATE = 128
CONV_W = 4
SSD_CHUNK = 128
XBC_DIM = D_SSM + 2 * SSM_GROUPS * SSM_STATE
ATTN_HEADS = 16
DILATED_BRANCHES = ((128, 1), (512, 4), (2048, 16))
MEM_LEN = 256
MEM_HEADS = 4
MEM_HEAD_DIM = 128
D_MEM = MEM_HEADS * MEM_HEAD_DIM
RMS_EPS = 1e-5
NEG = -1e30

LANES = 128
SUBLANES = 8
VMEM_BYTES_V7X = 64 * 1024 * 1024
VMEM_LIMIT = 56 * 1024 * 1024

COL_Z = 0
COL_XBC = D_SSM
COL_Q = D_SSM + XBC_DIM
COL_K = COL_Q + D_ATTN
COL_V = COL_K + D_ATTN
D_PROJ = COL_V + D_ATTN


def _params(sem):
    return pltpu.CompilerParams(dimension_semantics=sem, vmem_limit_bytes=VMEM_LIMIT)


def _rms(x, g):
    ms = jnp.mean(x * x, axis=-1, keepdims=True)
    return x * lax.rsqrt(ms + RMS_EPS) * g


def _silu(x):
    return x * (0.5 * jnp.tanh(0.5 * x) + 0.5)


def _softplus(x):
    return jnp.maximum(x, 0.0) + jnp.log1p(jnp.exp(-jnp.abs(x)))


def _split3(x):
    hi = x.astype(BF16)
    r1 = x - hi.astype(F32)
    mid = r1.astype(BF16)
    lo = (r1 - mid.astype(F32)).astype(BF16)
    return hi, mid, lo


def _dot(a, b):
    return jnp.dot(a, b, preferred_element_type=F32)


def _rms_matmul_kernel(x_ref, g_ref, w_ref, o_ref, hn_ref):
    @pl.when(pl.program_id(1) == 0)
    def _():
        hn_ref[...] = _rms(x_ref[...], g_ref[...]).astype(BF16)

    o_ref[...] = _dot(hn_ref[...], w_ref[...])


def _dot_nt(a, b):
    return lax.dot_general(a, b, (((1,), (1,)), ((), ())), preferred_element_type=F32)


def _rms_matmul_dt_kernel(n_dt, x_ref, g_ref, wt_ref, wdt_ref, o_ref, dt_ref, hn_ref):
    @pl.when(pl.program_id(1) == 0)
    def _():
        hn = _rms(x_ref[...], g_ref[...])
        hi = hn.astype(BF16)
        lo = (hn - hi.astype(F32)).astype(BF16)
        hn_ref[...] = hi
        wd = wdt_ref[...]
        wd_hi = wd.astype(BF16)
        wd2 = jnp.concatenate([wd_hi, (wd - wd_hi.astype(F32)).astype(BF16)], axis=0)
        both = _dot_nt(hi, wd2) + _dot_nt(lo, wd2)
        dt = both[:, :LANES] + both[:, LANES:]
        dt_ref[...] = jnp.where(lax.broadcasted_iota(jnp.int32, dt.shape, 1) < n_dt, dt, 0.0)

    o_ref[...] = _dot_nt(hn_ref[...], wt_ref[...].astype(BF16))


def rms_matmul(x, g, w, *, tm, tn):
    m, k = x.shape
    n = w.shape[1]
    return pl.pallas_call(
        _rms_matmul_kernel, grid=(m // tm, n // tn),
        in_specs=[
            pl.BlockSpec((tm, k), lambda i, j: (i, 0)),
            pl.BlockSpec((1, k), lambda i, j: (0, 0)),
            pl.BlockSpec((k, tn), lambda i, j: (0, j)),
        ],
        out_specs=pl.BlockSpec((tm, tn), lambda i, j: (i, j)),
        out_shape=jax.ShapeDtypeStruct((m, n), F32), scratch_shapes=[pltpu.VMEM((tm, k), BF16)],
        compiler_params=_params(("parallel", "arbitrary")), name="rms_matmul",
    )(x, g.reshape(1, k), w)


def rms_matmul_dt(x, g, w_t, dt_lo, dt_hi, *, tm, tn):
    m, k = x.shape
    n_dt = dt_hi - dt_lo
    n = w_t.shape[0] - n_dt
    n_a = dt_lo // tn
    assert dt_lo % tn == 0 and n % tn == 0 and n_dt % SUBLANES == 0 and n_dt <= LANES
    assert dt_lo + LANES <= w_t.shape[0]
    return pl.pallas_call(
        functools.partial(_rms_matmul_dt_kernel, n_dt), grid=(m // tm, n // tn),
        in_specs=[
            pl.BlockSpec((tm, k), lambda i, j: (i, 0)),
            pl.BlockSpec((1, k), lambda i, j: (0, 0)),
            pl.BlockSpec((pl.Element(tn), pl.Element(k)),
                         lambda i, j: (pl.multiple_of(j * tn + jnp.where(j >= n_a, n_dt, 0), SUBLANES), 0)),
            pl.BlockSpec((pl.Element(LANES), pl.Element(k)), lambda i, j: (dt_lo, 0)),
        ],
        out_specs=[pl.BlockSpec((tm, tn), lambda i, j: (i, j)), pl.BlockSpec((tm, LANES), lambda i, j: (i, 0))],
        out_shape=[jax.ShapeDtypeStruct((m, n), F32), jax.ShapeDtypeStruct((m, LANES), F32)],
        scratch_shapes=[pltpu.VMEM((tm, k), BF16)],
        compiler_params=_params(("parallel", "arbitrary")), name="rms_matmul_dt",
    )(x, g.reshape(1, k), w_t, w_t)


def _mm_res_kernel(n_pairs, *refs):
    res_ref = refs[2 * n_pairs]
    o_ref = refs[2 * n_pairs + 1]
    acc = res_ref[...]
    for i in range(n_pairs):
        acc = acc + _dot(refs[2 * i][...].astype(BF16), refs[2 * i + 1][...])
    o_ref[...] = acc


def matmul_residual(pairs, res, *, tm, tn):
    m, n = res.shape
    grid = (m // tm, n // tn)
    in_specs, args = [], []
    for a, w, kblk in pairs:
        k = a.shape[1]
        in_specs.append(pl.BlockSpec((tm, k), lambda i, j: (i, 0)))
        in_specs.append(pl.BlockSpec((k, tn), lambda i, j, kblk=kblk: (kblk, j)))
        args += [a, w]
    in_specs.append(pl.BlockSpec((tm, tn), lambda i, j: (i, j)))
    args.append(res)
    return pl.pallas_call(
        functools.partial(_mm_res_kernel, len(pairs)), grid=grid, in_specs=in_specs,
        out_specs=pl.BlockSpec((tm, tn), lambda i, j: (i, j)),
        out_shape=jax.ShapeDtypeStruct((m, n), F32),
        compiler_params=_params(("parallel", "parallel")), name="matmul_residual",
    )(*args)


def _ffn_step(x_ref, g2_ref, wg_ref, wu_ref, wd_ref, gf_ref, o_ref, hn_ref):
    f = pl.program_id(1)

    @pl.when(f == 0)
    def _():
        hn_ref[...] = _rms(x_ref[...], g2_ref[...]).astype(BF16)
        o_ref[...] = jnp.zeros_like(o_ref)

    hn = hn_ref[...]
    act = _silu(_dot(hn, wg_ref[...])) * _dot(hn, wu_ref[...])
    o_ref[...] += _dot(act.astype(BF16), wd_ref[...])

    @pl.when(f == pl.num_programs(1) - 1)
    def _():
        o_ref[...] = _rms(x_ref[...] + o_ref[...], gf_ref[...])


def _ffn_kernel(x_ref, g2_ref, wg_ref, wu_ref, wd_ref, gf_ref, o_ref, hn_ref):
    _ffn_step(x_ref, g2_ref, wg_ref, wu_ref, wd_ref, gf_ref, o_ref, hn_ref)


def ffn_final(x, g2, wg, wu, wd, gf, *, tm, tf):
    m, d = x.shape
    dff = wg.shape[1]
    grid = (m // tm, dff // tf)
    return pl.pallas_call(
        _ffn_kernel, grid=grid,
        in_specs=[
            pl.BlockSpec((tm, d), lambda i, f: (i, 0)),
            pl.BlockSpec((1, d), lambda i, f: (0, 0)),
            pl.BlockSpec((d, tf), lambda i, f: (0, f)),
            pl.BlockSpec((d, tf), lambda i, f: (0, f)),
            pl.BlockSpec((tf, d), lambda i, f: (f, 0)),
            pl.BlockSpec((1, d), lambda i, f: (0, 0)),
        ],
        out_specs=pl.BlockSpec((tm, d), lambda i, f: (i, 0)),
        out_shape=jax.ShapeDtypeStruct((m, d), F32),
        scratch_shapes=[pltpu.VMEM((tm, d), BF16)],
        compiler_params=_params(("parallel", "arbitrary")), name="ffn_final",
    )(x, g2.reshape(1, d), wg, wu, wd, gf.reshape(1, d))


def _mix_mem_kernel(x_ref, ys_ref, ya_ref, wo_ref, g_ref, wq_ref, mkv_ref, wm_ref, o_ref):
    half = ys_ref.shape[1]
    x1 = x_ref[...] + _dot(ys_ref[...], wo_ref[0:half, :]) + _dot(ya_ref[...], wo_ref[half:, :])
    hq = _dot(_rms(x1, g_ref[...]).astype(BF16), wq_ref[...])
    scale = MEM_HEAD_DIM ** -0.5
    outs = []
    for h in range(MEM_HEADS):
        sl = slice(h * MEM_HEAD_DIM, (h + 1) * MEM_HEAD_DIM)
        k = mkv_ref[:, sl].astype(BF16)
        v = mkv_ref[:, D_MEM + h * MEM_HEAD_DIM:D_MEM + (h + 1) * MEM_HEAD_DIM].astype(BF16)
        s = lax.dot_general(hq[:, sl].astype(BF16), k, (((1,), (1,)), ((), ())), preferred_element_type=F32) * scale
        p = jnp.exp(s - jnp.max(s, axis=-1, keepdims=True))
        p = p / jnp.sum(p, axis=-1, keepdims=True)
        outs.append(_dot(p.astype(BF16), v).astype(BF16))
    o_ref[...] = x1 + _dot(jnp.concatenate(outs, axis=1), wm_ref[...])


def mix_mem_fused(x, y_ssm, y_att, w_out, ln_mem_g, w_mq, mkv, w_mo, *, tm):
    m, d = x.shape
    half = y_ssm.shape[1]
    row = lambda width: pl.BlockSpec((tm, width), lambda i: (i, 0))
    whole = lambda a: pl.BlockSpec(a.shape, lambda i: (0, 0), pipeline_mode=pl.Buffered(1))
    g = ln_mem_g.reshape(1, d)
    return pl.pallas_call(
        _mix_mem_kernel, grid=(m // tm,),
        in_specs=[row(d), row(half), row(half), whole(w_out), whole(g), whole(w_mq), whole(mkv), whole(w_mo)],
        out_specs=row(d),
        out_shape=jax.ShapeDtypeStruct((m, d), F32),
        compiler_params=_params(("parallel",)), name="mix_mem_fused",
    )(x, y_ssm, y_att, w_out, g, w_mq, mkv, w_mo)


def _head_cols(x, g, width):
    rows = x.shape[0]
    parts = [jnp.broadcast_to(x[:, 4 * g + i:4 * g + i + 1], (rows, width)) for i in range(4)]
    return jnp.concatenate(parts, axis=1)


def _ssd_prompt_kernel(z_ref, xs_ref, bc_ref, dt_ref, cw_ref, cb_ref, dtb_ref, alog_ref, dskip_ref, ng_ref,
                       y_ref, hfin_ref, convo_ref, cbuf, st_ref):
    c = pl.program_id(0)
    L = SSD_CHUNK
    P = HEAD_DIM
    GW = 4 * P

    @pl.when(c == 0)
    def _():
        cbuf[0:SUBLANES, :] = jnp.zeros((SUBLANES, XBC_DIM), F32)
        st_ref[...] = jnp.zeros_like(st_ref)

    cbuf[SUBLANES:SUBLANES + L, 0:D_SSM] = xs_ref[...]
    cbuf[SUBLANES:SUBLANES + L, D_SSM:XBC_DIM] = bc_ref[...]
    up = cbuf[...]
    taps = [(up if j == CONV_W - 1 else pltpu.roll(up, CONV_W - 1 - j, 0))[SUBLANES:SUBLANES + L] * cw_ref[j:j + 1, :]
            for j in range(CONV_W)]
    conv = cb_ref[...] + (((taps[0] + taps[1]) + taps[2]) + taps[3])

    @pl.when(c == pl.num_programs(0) - 1)
    def _():
        convo_ref[...] = cbuf[pl.ds(L + SUBLANES - (CONV_W - 1), CONV_W - 1), :]

    cbuf[0:SUBLANES, :] = cbuf[L:L + SUBLANES, :]

    act = _silu(conv)
    xs = act[:, 0:D_SSM]
    dt = _softplus(dt_ref[...] + dtb_ref[...])
    a = dt * (-jnp.exp(alog_ref[...]))
    ri = lax.broadcasted_iota(jnp.int32, (L, L), 0)
    ci = lax.broadcasted_iota(jnp.int32, (L, L), 1)
    causal = ri >= ci
    acum = jnp.dot(causal.astype(F32), a, precision=lax.Precision.HIGHEST, preferred_element_type=F32)
    acum_t = acum.T
    dt_t = dt.T
    last = acum[L - 1:L, :]
    ea = jnp.exp(acum)
    wend = jnp.exp(last - acum) * dt
    cdec = jnp.exp(last)
    lane = lax.broadcasted_iota(jnp.int32, (L, GW), 1)
    head_lanes = [(lane >= i * P) & (lane < (i + 1) * P) for i in range(4)]

    ys = []
    for g in range(SSM_GROUPS):
        bg = act[:, D_SSM + g * SSM_STATE:D_SSM + (g + 1) * SSM_STATE]
        cg = act[:, D_SSM + SSM_GROUPS * SSM_STATE + g * SSM_STATE:
                 D_SSM + SSM_GROUPS * SSM_STATE + (g + 1) * SSM_STATE]
        bgb = bg.astype(BF16)
        cgb = cg.astype(BF16)
        cb = lax.dot_general(cgb, bgb, (((1,), (1,)), ((), ())), preferred_element_type=F32)
        xg = xs[:, g * GW:(g + 1) * GW]
        yd = jnp.zeros((L, GW), F32)
        for i in range(4):
            h = 4 * g + i
            seg = acum[:, h:h + 1] - acum_t[h:h + 1, :]
            decay = jnp.exp(jnp.where(causal, seg, NEG))
            w = (cb * decay * dt_t[h:h + 1, :]).astype(BF16)
            xm = jnp.where(head_lanes[i], xg, 0.0).astype(BF16)
            yd = yd + _dot(w, xm)
        st = st_ref[g]
        yoff = _dot(cgb, st.astype(BF16)) * _head_cols(ea, g, P)
        ys.append(yd + yoff)
        xw = (xg * _head_cols(wend, g, P)).astype(BF16)
        new = _dot(bg.T.astype(BF16), xw)
        cd = jnp.concatenate([jnp.broadcast_to(cdec[:, 4 * g + i:4 * g + i + 1], (1, P)) for i in range(4)], axis=1)
        st_ref[g] = st * cd + new

    y = jnp.concatenate(ys, axis=1) + dskip_ref[...] * xs
    y = y * _silu(z_ref[...])
    outs = []
    for g in range(SSM_GROUPS):
        yg = y[:, g * GW:(g + 1) * GW]
        outs.append(yg * lax.rsqrt(jnp.mean(yg * yg, axis=-1, keepdims=True) + RMS_EPS))
    y_ref[...] = (jnp.concatenate(outs, axis=1) * ng_ref[...]).astype(y_ref.dtype)

    @pl.when(c == pl.num_programs(0) - 1)
    def _():
        for g in range(SSM_GROUPS):
            stg = st_ref[g]
            for i in range(4):
                hfin_ref[4 * g + i] = stg[:, i * P:(i + 1) * P].T


def _pad_lanes(v):
    return jnp.pad(v.astype(F32), (0, LANES - v.shape[0])).reshape(1, LANES)


def ssd_prompt(proj, dt_raw, conv_w, conv_b, dt_bias, a_log, d_skip, norm_g):
    s = proj.shape[0]
    L = SSD_CHUNK
    nblk = D_SSM // 1024
    col = lambda off: off // 1024
    const = lambda shape: pl.BlockSpec(shape, lambda c: (0,) * len(shape))
    return pl.pallas_call(
        _ssd_prompt_kernel, grid=(s // L,),
        in_specs=[
            pl.BlockSpec((L, D_SSM), lambda c: (c, col(COL_Z))),
            pl.BlockSpec((L, D_SSM), lambda c: (c, col(COL_XBC))),
            pl.BlockSpec((L, XBC_DIM - D_SSM), lambda c: (c, col(COL_XBC + D_SSM))),
            pl.BlockSpec((L, LANES), lambda c: (c, 0)),
            const((CONV_W, XBC_DIM)), const((1, XBC_DIM)), const((1, LANES)), const((1, LANES)),
            const((1, D_SSM)), const((1, D_SSM)),
        ],
        out_specs=[
            pl.BlockSpec((L, D_SSM), lambda c: (c, 0)),
            const((SSM_HEADS, HEAD_DIM, SSM_STATE)),
            const((CONV_W - 1, XBC_DIM)),
        ],
        out_shape=[
            jax.ShapeDtypeStruct((s, D_SSM), BF16),
            jax.ShapeDtypeStruct((SSM_HEADS, HEAD_DIM, SSM_STATE), F32),
            jax.ShapeDtypeStruct((CONV_W - 1, XBC_DIM), F32),
        ],
        scratch_shapes=[
            pltpu.VMEM((L + SUBLANES, XBC_DIM), F32),
            pltpu.VMEM((SSM_GROUPS, SSM_STATE, 4 * HEAD_DIM), F32),
        ],
        compiler_params=_params(("arbitrary",)), name="ssd_prompt",
    )(proj, proj, proj, dt_raw, conv_w, conv_b.reshape(1, XBC_DIM), _pad_lanes(dt_bias), _pad_lanes(a_log),
      jnp.repeat(d_skip, HEAD_DIM).reshape(1, D_SSM), norm_g.reshape(1, D_SSM))


DIL_BLK = 128
DIL_GROUP = 8


def _dil_kernel(q_ref, k_ref, v_ref, o_ref, acc_ref, l_ref, ma_ref, mb_ref, bias_ref):
    s_len = q_ref.shape[0]
    B = DIL_BLK
    scale = HEAD_DIM ** -0.5
    lane = lax.broadcasted_iota(jnp.int32, (B, LANES), 1)
    head_a = lane < HEAD_DIM
    qi = lax.broadcasted_iota(jnp.int32, (2 * B, 2 * B), 0) & (B - 1)
    ki = lax.broadcasted_iota(jnp.int32, (2 * B, 2 * B), 1)
    band = (ki >= qi) & (ki <= qi + B)
    bias_ref[0] = jnp.where(band & (ki >= B), 0.0, NEG)
    bias_ref[1] = jnp.where(band, 0.0, NEG)
    ones = jnp.ones((2 * B, LANES), BF16)

    def comb(x):
        return jnp.where(head_a, x[:B], x[B:])

    def rows(ref, start, d):
        if d == 1:
            return ref[pl.ds(start, B), :]
        return ref[pl.ds(start, B, stride=d), :]

    def store_rows(ref, start, d, val):
        if d == 1:
            ref[pl.ds(start, B), :] = val
        else:
            ref[pl.ds(start, B, stride=d), :] = val

    order = sorted(DILATED_BRANCHES, key=lambda wd: -wd[1])
    assert order[-1][1] == 1
    for bi, (w, d) in enumerate(order):
        assert w // d == B
        first_branch = bi == 0
        last_branch = bi == len(order) - 1
        per_res = s_len // (B * d)
        G = min(DIL_GROUP, per_res)
        assert per_res % G == 0

        def body(t, carry, d=d, first_branch=first_branch, last_branch=last_branch, per_res=per_res, G=G):
            r = t // (per_res // G)
            n0 = (t % (per_res // G)) * G
            starts = [(n0 + j) * (B * d) + r for j in range(G)]
            prev0 = jnp.maximum(n0 - 1, 0) * (B * d) + r
            if d == 1:
                starts = [pl.multiple_of(st, B) for st in starts]
                prev0 = pl.multiple_of(prev0, B)
            kb = [rows(k_ref, st, d).astype(BF16) for st in [prev0] + starts]
            vb = [rows(v_ref, st, d).astype(BF16) for st in [prev0] + starts]
            for j in range(G):
                start = starts[j]
                q = rows(q_ref, start, d) * scale
                qs = jnp.concatenate([jnp.where(head_a, q, 0.0), jnp.where(head_a, 0.0, q)], axis=0).astype(BF16)
                kk = jnp.concatenate([kb[j], kb[j + 1]], axis=0)
                s = lax.dot_general(qs, kk, (((1,), (1,)), ((), ())), preferred_element_type=F32)
                s = s + (bias_ref[jnp.minimum(n0, 1)] if j == 0 else bias_ref[1])
                m_row = jnp.broadcast_to(jnp.max(s, axis=-1, keepdims=True), (2 * B, LANES))
                if first_branch:
                    m_new = m_row
                else:
                    m_old = jnp.concatenate([rows(ma_ref, start, d), rows(mb_ref, start, d)], axis=0)
                    m_new = jnp.maximum(m_old, m_row)
                p = jnp.exp(s - jnp.concatenate([m_new, m_new], axis=1)).astype(BF16)
                pv = _dot(p, jnp.concatenate([jnp.concatenate([vb[j], vb[j + 1]], axis=0), ones], axis=1))
                l_new = comb(pv[:, LANES:])
                pv = comb(pv[:, :LANES])
                if first_branch:
                    acc, l_b = pv, l_new
                else:
                    alpha = comb(jnp.exp(m_old - m_new))
                    acc = rows(acc_ref, start, d) * alpha + pv
                    l_b = rows(l_ref, start, d) * alpha + l_new
                if last_branch:
                    o_ref[pl.ds(start, B), :] = (acc / l_b).astype(o_ref.dtype)
                else:
                    store_rows(acc_ref, start, d, acc)
                    store_rows(l_ref, start, d, l_b)
                    store_rows(ma_ref, start, d, m_new[:B])
                    store_rows(mb_ref, start, d, m_new[B:])
            return carry

        lax.fori_loop(0, s_len // (B * G), body, 0)


def dilated_prompt(proj):
    s = proj.shape[0]
    n_pairs = D_ATTN // LANES
    spec = lambda off: pl.BlockSpec((s, LANES), lambda h: (0, off // LANES + h))
    return pl.pallas_call(
        _dil_kernel, grid=(n_pairs,),
        in_specs=[spec(COL_Q), spec(COL_K), spec(COL_V)],
        out_specs=pl.BlockSpec((s, LANES), lambda h: (0, h)),
        out_shape=jax.ShapeDtypeStruct((s, D_ATTN), BF16),
        scratch_shapes=[pltpu.VMEM((s, LANES), F32)] * 4 + [pltpu.VMEM((2, 2 * DIL_BLK, 2 * DIL_BLK), F32)],
        compiler_params=_params(("parallel",)), name="dilated_prompt",
    )(proj, proj, proj)


SSD_SAMPLE_BB = 8


def _head_expand(x, seg_t):
    hi, mid, lo = _split3(x)
    return _dot(hi, seg_t) + _dot(mid, seg_t) + _dot(lo, seg_t)


def _ssd_sample_kernel(z_ref, xs_ref, bc_ref, dt_ref, sc_ref, st_ref, cw_ref, cb_ref, dtb_ref, alog_ref,
                       dskip_ref, ng_ref, segt_ref,
                       y_ref, ho_ref, convo_ref,
                       xs_s, t2_s, dec_s, b_s, c_s, col_s, yoff_s):
    step = pl.program_id(0)
    nb = xs_ref.shape[0]
    BB = SSD_SAMPLE_BB
    GW = 4 * HEAD_DIM
    HP = SSM_HEADS * HEAD_DIM

    @pl.when(step == 0)
    def _():
        xnew = jnp.concatenate([xs_ref[...], bc_ref[...]], axis=1)
        rows = [sc_ref[:, j * XBC_DIM:(j + 1) * XBC_DIM] for j in range(CONV_W - 1)] + [xnew]
        taps = [rows[j] * cw_ref[j:j + 1, :] for j in range(CONV_W)]
        act = _silu(cb_ref[...] + (((taps[0] + taps[1]) + taps[2]) + taps[3]))
        for j in range(CONV_W - 1):
            convo_ref[:, j * XBC_DIM:(j + 1) * XBC_DIM] = rows[j + 1]
        xs = act[:, 0:D_SSM]
        bm = act[:, D_SSM:D_SSM + SSM_GROUPS * SSM_STATE]
        cm = act[:, D_SSM + SSM_GROUPS * SSM_STATE:]
        dt = _softplus(dt_ref[...] + dtb_ref[...])
        dec = jnp.exp(dt * (-jnp.exp(alog_ref[...])))
        seg_t = segt_ref[...]
        u = _head_expand(dt, seg_t) * xs
        dec_e = _head_expand(dec, seg_t)
        xs_s[...] = xs
        dec_s[...] = dec_e
        b_s[...] = bm
        c_s[...] = cm
        for g in range(SSM_GROUPS):
            sl = slice(g * SSM_STATE, (g + 1) * SSM_STATE)
            bc = jnp.sum(bm[:, sl] * cm[:, sl], axis=-1, keepdims=True)
            t2_s[:, g * GW:(g + 1) * GW] = u[:, g * GW:(g + 1) * GW] * bc
        for j in range(HP // LANES):
            for src, base in ((u, 0), (dec_e, HP)):
                t = src[:, j * LANES:(j + 1) * LANES].T
                for k, part in enumerate(_split3(t)):
                    col_s[base + j * LANES:base + (j + 1) * LANES, k * nb:(k + 1) * nb] = part

    def per_pair(ip, carry):
        b0 = step * BB + 2 * ip
        ridx = lax.broadcasted_iota(jnp.int32, (3 * nb, 2 * SSM_STATE), 0)
        lane = lax.broadcasted_iota(jnp.int32, (3 * nb, 2 * SSM_STATE), 1)
        tgt = b0 + (lane >= SSM_STATE).astype(jnp.int32)
        sel = ((ridx == tgt) | (ridx == tgt + nb) | (ridx == tgt + 2 * nb)).astype(BF16)
        cols2 = _dot(col_s[...], sel)
        c8 = c_s[pl.ds(pl.multiple_of(step * BB, BB), BB), :]
        for j in range(2):
            i = 2 * ip + j
            b = b0 + j
            cols = cols2[:, j * SSM_STATE:(j + 1) * SSM_STATE]
            h0 = st_ref[i].reshape(HP, SSM_STATE)
            brow = b_s[pl.ds(b, 1), :]
            pick = lax.broadcasted_iota(jnp.int32, (BB, GW), 0) == i
            for g in range(SSM_GROUPS):
                rs = slice(g * GW, (g + 1) * GW)
                ls = slice(g * SSM_STATE, (g + 1) * SSM_STATE)
                h0g = h0[rs]
                hn = cols[HP + g * GW:HP + (g + 1) * GW] * h0g + cols[rs] * brow[:, ls]
                ho_ref[i, 4 * g:4 * g + 4] = hn.reshape(4, HEAD_DIM, SSM_STATE)
                yo = lax.dot_general(c8[:, ls].astype(BF16), h0g.astype(BF16), (((1,), (1,)), ((), ())),
                                     preferred_element_type=F32)
                yoff_s[pl.ds(b, 1), rs] = jnp.sum(jnp.where(pick, yo, 0.0), axis=0, keepdims=True)
        return carry

    lax.fori_loop(0, BB // 2, per_pair, 0)

    @pl.when(step == pl.num_programs(0) - 1)
    def _():
        xs = xs_s[...]
        y = dec_s[...] * yoff_s[...] + t2_s[...] + dskip_ref[...] * xs
        y = y * _silu(z_ref[...])
        for g in range(SSM_GROUPS):
            yg = y[:, g * GW:(g + 1) * GW]
            yg = yg * lax.rsqrt(jnp.mean(yg * yg, axis=-1, keepdims=True) + RMS_EPS)
            y_ref[:, g * GW:(g + 1) * GW] = (yg * ng_ref[:, g * GW:(g + 1) * GW]).astype(y_ref.dtype)


def _seg_ones(n_heads, width):
    c = jnp.arange(n_heads * width)[:, None] // width
    seg = (c == jnp.arange(LANES)[None, :]).astype(BF16)
    return seg, seg.T


def ssd_sample(proj, dt_raw, state_conv, state_ssm, conv_w, conv_b, dt_bias, a_log, d_skip, norm_g):
    nb = proj.shape[0]
    BB = SSD_SAMPLE_BB
    HP = SSM_HEADS * HEAD_DIM
    col = lambda off: off // 1024
    const = lambda shape: pl.BlockSpec(shape, lambda s: (0,) * len(shape))
    _, seg_t = _seg_ones(SSM_HEADS, HEAD_DIM)
    st_spec = pl.BlockSpec((BB, SSM_HEADS, HEAD_DIM, SSM_STATE), lambda s: (s, 0, 0, 0))
    y, h_new, conv_new = pl.pallas_call(
        _ssd_sample_kernel, grid=(nb // BB,),
        in_specs=[
            pl.BlockSpec((nb, D_SSM), lambda s: (0, col(COL_Z))),
            pl.BlockSpec((nb, D_SSM), lambda s: (0, col(COL_XBC))),
            pl.BlockSpec((nb, XBC_DIM - D_SSM), lambda s: (0, col(COL_XBC + D_SSM))),
            const((nb, LANES)), const((nb, (CONV_W - 1) * XBC_DIM)), st_spec,
            const((CONV_W, XBC_DIM)), const((1, XBC_DIM)), const((1, LANES)), const((1, LANES)),
            const((1, D_SSM)), const((1, D_SSM)), const((LANES, HP)),
        ],
        out_specs=[const((nb, D_SSM)), st_spec, const((nb, (CONV_W - 1) * XBC_DIM))],
        out_shape=[
            jax.ShapeDtypeStruct((nb, D_SSM), BF16),
            jax.ShapeDtypeStruct(state_ssm.shape, F32),
            jax.ShapeDtypeStruct((nb, (CONV_W - 1) * XBC_DIM), F32),
        ],
        scratch_shapes=[
            pltpu.VMEM((nb, HP), F32), pltpu.VMEM((nb, HP), F32), pltpu.VMEM((nb, HP), F32),
            pltpu.VMEM((nb, SSM_GROUPS * SSM_STATE), F32), pltpu.VMEM((nb, SSM_GROUPS * SSM_STATE), F32),
            pltpu.VMEM((2 * HP, 3 * nb), BF16), pltpu.VMEM((nb, HP), F32),
        ],
        compiler_params=_params(("arbitrary",)), name="ssd_sample",
    )(proj, proj, proj, dt_raw, state_conv.reshape(nb, (CONV_W - 1) * XBC_DIM), state_ssm, conv_w,
      conv_b.reshape(1, XBC_DIM), _pad_lanes(dt_bias), _pad_lanes(a_log),
      jnp.repeat(d_skip, HEAD_DIM).reshape(1, D_SSM), norm_g.reshape(1, D_SSM), seg_t)
    return y, h_new, conv_new.reshape(nb, CONV_W - 1, XBC_DIM)


WIN_UNIT_HEADS = 4
WIN_UNITS_PER_STEP = 3


def _win_unit_attention(n_branch, q4, kn4, vn4, cnt, kt, vt):
    rep = lambda t: jnp.stack([jnp.broadcast_to(t[h:h + 1, :], (SUBLANES, t.shape[1])) for h in range(t.shape[0])])
    q = rep(q4) * (HEAD_DIM ** -0.5)
    s = jnp.einsum("hqd,hdw->hqw", q.astype(BF16), kt.astype(BF16), preferred_element_type=F32)
    s = jnp.where(cnt > 0.0, s, NEG)
    s_new = jnp.sum(q * rep(kn4), axis=-1, keepdims=True)
    m = jnp.maximum(jnp.max(s, axis=-1, keepdims=True), s_new)
    p = cnt * jnp.exp(s - m)
    p_new = n_branch * jnp.exp(s_new - m)
    l = jnp.sum(p, axis=-1, keepdims=True) + p_new
    o = jnp.einsum("hqw,hdw->hqd", p.astype(BF16), vt.astype(BF16), preferred_element_type=F32)
    return (o + p_new * rep(vn4)) / l


def _ffn_stream_kernel(n_branch, x_ref, g2_ref, wg_ref, wu_ref, wd_ref, gf_ref,
                       q_ref, kn_ref, vn_ref, cnt_ref, kt_hbm, vt_hbm,
                       o_ref, oa_ref, hn_ref, kbuf, vbuf, sem):
    step = pl.program_id(0) * pl.num_programs(1) + pl.program_id(1)
    uh = WIN_UNIT_HEADS
    per_seq = kt_hbm.shape[1] // uh
    n_units = kt_hbm.shape[0] * per_seq

    def copies(u, slot):
        b = u // per_seq
        h0 = (u % per_seq) * uh
        return (pltpu.make_async_copy(kt_hbm.at[b, pl.ds(h0, uh)], kbuf.at[slot], sem.at[0, slot]),
                pltpu.make_async_copy(vt_hbm.at[b, pl.ds(h0, uh)], vbuf.at[slot], sem.at[1, slot]))

    @pl.when(step == 0)
    def _():
        for slot in range(WIN_UNITS_PER_STEP):
            for c in copies(slot, slot):
                c.start()

    cnt = cnt_ref[...][None]
    for slot in range(WIN_UNITS_PER_STEP):
        u = step * WIN_UNITS_PER_STEP + slot

        @pl.when(u < n_units)
        def _(u=u, slot=slot):
            for c in copies(u, slot):
                c.wait()
            b = u // per_seq
            h0 = (u % per_seq) * uh
            rows = pl.ds(h0, uh)
            o = _win_unit_attention(n_branch, q_ref[b, rows, :], kn_ref[b, rows, :], vn_ref[b, rows, :], cnt,
                                    kbuf[slot], vbuf[slot])
            for h in range(uh):
                oa_ref[b, pl.ds(h0 + h, 1), :] = o[h, 0:1, :]

            @pl.when(u + WIN_UNITS_PER_STEP < n_units)
            def _():
                for c in copies(u + WIN_UNITS_PER_STEP, slot):
                    c.start()

    _ffn_step(x_ref, g2_ref, wg_ref, wu_ref, wd_ref, gf_ref, o_ref, hn_ref)


def ffn_final_with_window_attention(x, g2, wg, wu, wd, gf, q, kn, vn, cache_k, cache_v, *, tm, tf):
    m, d = x.shape
    dff = wg.shape[1]
    nb, w_buf = cache_k.shape[0], cache_k.shape[1]
    grid = (m // tm, dff // tf)
    assert nb * (ATTN_HEADS // WIN_UNIT_HEADS) <= grid[0] * grid[1] * WIN_UNITS_PER_STEP
    pos = jnp.arange(w_buf)
    cnt = jnp.zeros((w_buf,), F32)
    for w, dil in DILATED_BRANCHES:
        cnt = cnt + ((pos >= w_buf - w) & ((w_buf - pos) % dil == 0)).astype(F32)
    const = lambda shape: pl.BlockSpec(shape, lambda i, f: (0,) * len(shape))
    small = const((nb, ATTN_HEADS, HEAD_DIM))
    unit = (WIN_UNITS_PER_STEP, WIN_UNIT_HEADS, HEAD_DIM, w_buf)
    y, att = pl.pallas_call(
        functools.partial(_ffn_stream_kernel, float(len(DILATED_BRANCHES))), grid=grid,
        in_specs=[
            pl.BlockSpec((tm, d), lambda i, f: (i, 0)),
            const((1, d)),
            pl.BlockSpec((d, tf), lambda i, f: (0, f)),
            pl.BlockSpec((d, tf), lambda i, f: (0, f)),
            pl.BlockSpec((tf, d), lambda i, f: (f, 0)),
            const((1, d)),
            small, small, small, const((1, w_buf)),
            pl.BlockSpec(memory_space=pl.ANY), pl.BlockSpec(memory_space=pl.ANY),
        ],
        out_specs=[pl.BlockSpec((tm, d), lambda i, f: (i, 0)), small],
        out_shape=[jax.ShapeDtypeStruct((m, d), F32), jax.ShapeDtypeStruct((nb, ATTN_HEADS, HEAD_DIM), F32)],
        scratch_shapes=[pltpu.VMEM((tm, d), BF16), pltpu.VMEM(unit, F32), pltpu.VMEM(unit, F32),
                        pltpu.SemaphoreType.DMA((2, WIN_UNITS_PER_STEP))],
        compiler_params=_params(("arbitrary", "arbitrary")), name="ffn_final_window_stream",
    )(x, g2.reshape(1, d), wg, wu, wd, gf.reshape(1, d), q, kn, vn, cnt.reshape(1, w_buf),
      jnp.transpose(cache_k, (0, 2, 3, 1)), jnp.transpose(cache_v, (0, 2, 3, 1)))
    return y, att.reshape(nb, D_ATTN)


MEM_SAMPLE_BB = 4


def _mem_sample_kernel(q_ref, k_ref, v_ref, o_ref):
    shape = (q_ref.shape[1], k_ref.shape[1])
    row = lax.broadcasted_iota(jnp.int32, shape, 0)
    col = lax.broadcasted_iota(jnp.int32, shape, 1)
    own = (col & (MEM_HEADS - 1)) == (row & (MEM_HEADS - 1))
    for i in range(q_ref.shape[0]):
        q = (q_ref[i] * (MEM_HEAD_DIM ** -0.5)).astype(BF16)
        s = lax.dot_general(q, k_ref[i].astype(BF16), (((1,), (1,)), ((), ())), preferred_element_type=F32)
        s = jnp.where(own, s, NEG)
        m = jnp.max(s, axis=-1, keepdims=True)
        p = jnp.where(own, jnp.exp(s - m), 0.0)
        l = jnp.sum(p, axis=-1, keepdims=True)
        o_ref[i] = _dot((p / l).astype(BF16), v_ref[i].astype(BF16))


def mem_attention_sample(q, cache_mk, cache_mv):
    nb = q.shape[0]
    assert MEM_HEADS & (MEM_HEADS - 1) == 0 and SUBLANES % MEM_HEADS == 0
    q8 = jnp.tile(q.reshape(nb, MEM_HEADS, MEM_HEAD_DIM), (1, SUBLANES // MEM_HEADS, 1))
    rows = MEM_LEN * MEM_HEADS
    bb = MEM_SAMPLE_BB
    small = pl.BlockSpec((bb, SUBLANES, MEM_HEAD_DIM), lambda b: (b, 0, 0))
    big = pl.BlockSpec((bb, rows, MEM_HEAD_DIM), lambda b: (b, 0, 0))
    out = pl.pallas_call(
        _mem_sample_kernel, grid=(nb // bb,), in_specs=[small, big, big], out_specs=small,
        out_shape=jax.ShapeDtypeStruct((nb, SUBLANES, MEM_HEAD_DIM), F32),
        compiler_params=_params(("parallel",)), name="mem_attention_sample",
    )(q8, cache_mk.reshape(nb, rows, MEM_HEAD_DIM), cache_mv.reshape(nb, rows, MEM_HEAD_DIM))
    return out[:, :MEM_HEADS].reshape(nb, D_MEM)


def _layer_tail(x, y_ssm, y_att, mem_attend, lw, gf, *, tm, tf):
    half = D_SSM
    x1 = matmul_residual([(y_ssm, lw["w_out"], 0), (y_att, lw["w_out"], 1)], x, tm=tm, tn=1024)
    assert y_ssm.shape[1] == half and y_att.shape[1] == half
    hq = rms_matmul(x1, lw["ln_mem_g"], lw["w_mq"], tm=tm, tn=D_MEM)
    x2 = matmul_residual([(mem_attend(hq), lw["w_mo"], 0)], x1, tm=tm, tn=1024)
    return ffn_final(x2, lw["ln2_g"], lw["w_gate"], lw["w_up"], lw["w_down"], gf, tm=min(tm, 512), tf=tf)


def kernel(x_prompt, x_sample, cache_win_k, cache_win_v, state_conv, state_ssm, cache_mem_k, cache_mem_v, mem_prompt, ln1_g, w_in, conv_w, conv_b, dt_bias, a_log, d_skip, ssm_norm_g, w_out, ln_mem_g, mem_norm_g, w_mq, w_mk, w_mv, w_mo, ln2_g, w_gate, w_up, w_down, ln_f_g):
    bp, s, d = x_prompt.shape
    nb, ts, _ = x_sample.shape
    depth = w_in.shape[0]
    assert bp == 1 and ts == 1 and depth == 1
    w_keep = min(DILATED_BRANCHES[-1][0], s)
    l = 0
    dt_lo = D_SSM + XBC_DIM
    dt_hi = dt_lo + SSM_HEADS
    w_in_t = jnp.transpose(w_in[l])
    lw = dict(w_out=w_out[l].astype(BF16), ln_mem_g=ln_mem_g[l], w_mq=w_mq[l].astype(BF16),
              w_mo=w_mo[l].astype(BF16), ln2_g=ln2_g[l], w_gate=w_gate[l].astype(BF16),
              w_up=w_up[l].astype(BF16), w_down=w_down[l].astype(BF16))
    w_mkv = jnp.concatenate([w_mk[l], w_mv[l]], axis=1).astype(BF16)
    ssm_w = (conv_w[l], conv_b[l], dt_bias[l], a_log[l], d_skip[l], ssm_norm_g[l])

    xs = x_sample.reshape(nb, d)
    proj_s, dt_s = rms_matmul_dt(xs, ln1_g[l], w_in_t, dt_lo, dt_hi, tm=nb, tn=1024)
    heads = lambda t: t.reshape(t.shape[0], ATTN_HEADS, HEAD_DIM)
    q_s, k_s, v_s = (heads(proj_s[:, c:c + D_ATTN]) for c in (COL_Q, COL_K, COL_V))

    xp = x_prompt.reshape(s, d)
    mkv = rms_matmul(mem_prompt.reshape(MEM_LEN, d), mem_norm_g[l], w_mkv, tm=MEM_LEN, tn=2 * D_MEM)
    proj, dt_raw = rms_matmul_dt(xp, ln1_g[l], w_in_t, dt_lo, dt_hi, tm=1024, tn=1024)
    y_ssm, h_p, conv_p = ssd_prompt(proj, dt_raw, *ssm_w)
    y_att = dilated_prompt(proj)
    x2 = mix_mem_fused(xp, y_ssm, y_att, lw["w_out"], lw["ln_mem_g"], lw["w_mq"], mkv, lw["w_mo"], tm=512)
    y_prompt, y_att_s = ffn_final_with_window_attention(
        x2, lw["ln2_g"], lw["w_gate"], lw["w_up"], lw["w_down"], ln_f_g,
        q_s, k_s, v_s, cache_win_k[l], cache_win_v[l], tm=512, tf=512)

    y_ssm_s, h_s, conv_s = ssd_sample(proj_s, dt_s, state_conv[l], state_ssm[l], *ssm_w)
    y_sample = _layer_tail(xs, y_ssm_s, y_att_s,
                           lambda hq: mem_attention_sample(hq, cache_mem_k[l], cache_mem_v[l]), lw, ln_f_g,
                           tm=nb, tf=512)

    return (
        y_prompt.reshape(bp, s, d),
        y_sample.reshape(nb, ts, d),
        heads(proj[s - w_keep:, COL_K:COL_V])[None, None],
        heads(proj[s - w_keep:, COL_V:])[None, None],
        conv_p[None, None],
        h_p[None, None],
        mkv[:, :D_MEM].reshape(1, 1, MEM_LEN, MEM_HEADS, MEM_HEAD_DIM),
        mkv[:, D_MEM:].reshape(1, 1, MEM_LEN, MEM_HEADS, MEM_HEAD_DIM),
        k_s[None, :, None],
        v_s[None, :, None],
        conv_s[None],
        h_s[None],
    )
```

```python
import functools

import jax
import jax.numpy as jnp
from jax import lax
from jax.experimental import pallas as pl
from jax.experimental.pallas import tpu as pltpu

F32 = jnp.float32
BF16 = jnp.bfloat16

D_MODEL = 2048
D_SSM = 1024
D_ATTN = 1024
HEAD_DIM = 64
SSM_HEADS = 16
SSM_GROUPS = 4
SSM_STATE = 128
CONV_W = 4
SSD_CHUNK = 128
XBC_DIM = D_SSM + 2 * SSM_GROUPS * SSM_STATE
ATTN_HEADS = 16
DILATED_BRANCHES = ((128, 1), (512, 4), (2048, 16))
MEM_LEN = 256
MEM_HEADS = 4
MEM_HEAD_DIM = 128
D_MEM = MEM_HEADS * MEM_HEAD_DIM
RMS_EPS = 1e-5
NEG = -1e30

LANES = 128
SUBLANES = 8
VMEM_BYTES_V7X = 64 * 1024 * 1024
VMEM_LIMIT = 56 * 1024 * 1024

COL_Z = 0
COL_XBC = D_SSM
COL_Q = D_SSM + XBC_DIM
COL_K = COL_Q + D_ATTN
COL_V = COL_K + D_ATTN
D_PROJ = COL_V + D_ATTN


def _params(sem):
    return pltpu.CompilerParams(dimension_semantics=sem, vmem_limit_bytes=VMEM_LIMIT)


def _rms(x, g):
    ms = jnp.mean(x * x, axis=-1, keepdims=True)
    return x * lax.rsqrt(ms + RMS_EPS) * g


def _silu(x):
    return x * (0.5 * jnp.tanh(0.5 * x) + 0.5)


def _softplus(x):
    return jnp.maximum(x, 0.0) + jnp.log1p(jnp.exp(-jnp.abs(x)))


def _split3(x):
    hi = x.astype(BF16)
    r1 = x - hi.astype(F32)
    mid = r1.astype(BF16)
    lo = (r1 - mid.astype(F32)).astype(BF16)
    return hi, mid, lo


def _dot(a, b):
    return jnp.dot(a, b, preferred_element_type=F32)


def _rms_matmul_kernel(x_ref, g_ref, w_ref, o_ref, hn_ref):
    @pl.when(pl.program_id(1) == 0)
    def _():
        hn_ref[...] = _rms(x_ref[...], g_ref[...]).astype(BF16)

    o_ref[...] = _dot(hn_ref[...], w_ref[...])


def _dot_nt(a, b):
    return lax.dot_general(a, b, (((1,), (1,)), ((), ())), preferred_element_type=F32)


def _rms_matmul_dt_kernel(n_dt, x_ref, g_ref, wt_ref, wdt_ref, o_ref, dt_ref, hn_ref):
    @pl.when(pl.program_id(1) == 0)
    def _():
        hn = _rms(x_ref[...], g_ref[...])
        hi = hn.astype(BF16)
        lo = (hn - hi.astype(F32)).astype(BF16)
        hn_ref[...] = hi
        wd = wdt_ref[...]
        wd_hi = wd.astype(BF16)
        wd2 = jnp.concatenate([wd_hi, (wd - wd_hi.astype(F32)).astype(BF16)], axis=0)
        both = _dot_nt(hi, wd2) + _dot_nt(lo, wd2)
        dt = both[:, :LANES] + both[:, LANES:]
        dt_ref[...] = jnp.where(lax.broadcasted_iota(jnp.int32, dt.shape, 1) < n_dt, dt, 0.0)

    o_ref[...] = _dot_nt(hn_ref[...], wt_ref[...].astype(BF16))


def rms_matmul(x, g, w, *, tm, tn):
    m, k = x.shape
    n = w.shape[1]
    return pl.pallas_call(
        _rms_matmul_kernel, grid=(m // tm, n // tn),
        in_specs=[
            pl.BlockSpec((tm, k), lambda i, j: (i, 0)),
            pl.BlockSpec((1, k), lambda i, j: (0, 0)),
            pl.BlockSpec((k, tn), lambda i, j: (0, j)),
        ],
        out_specs=pl.BlockSpec((tm, tn), lambda i, j: (i, j)),
        out_shape=jax.ShapeDtypeStruct((m, n), F32), scratch_shapes=[pltpu.VMEM((tm, k), BF16)],
        compiler_params=_params(("parallel", "arbitrary")), name="rms_matmul",
    )(x, g.reshape(1, k), w)


def rms_matmul_dt(x, g, w_t, dt_lo, dt_hi, *, tm, tn):
    m, k = x.shape
    n_dt = dt_hi - dt_lo
    n = w_t.shape[0] - n_dt
    n_a = dt_lo // tn
    assert dt_lo % tn == 0 and n % tn == 0 and n_dt % SUBLANES == 0 and n_dt <= LANES
    assert dt_lo + LANES <= w_t.shape[0]
    return pl.pallas_call(
        functools.partial(_rms_matmul_dt_kernel, n_dt), grid=(m // tm, n // tn),
        in_specs=[
            pl.BlockSpec((tm, k), lambda i, j: (i, 0)),
            pl.BlockSpec((1, k), lambda i, j: (0, 0)),
            pl.BlockSpec((pl.Element(tn), pl.Element(k)),
                         lambda i, j: (pl.multiple_of(j * tn + jnp.where(j >= n_a, n_dt, 0), SUBLANES), 0)),
            pl.BlockSpec((pl.Element(LANES), pl.Element(k)), lambda i, j: (dt_lo, 0)),
        ],
        out_specs=[pl.BlockSpec((tm, tn), lambda i, j: (i, j)), pl.BlockSpec((tm, LANES), lambda i, j: (i, 0))],
        out_shape=[jax.ShapeDtypeStruct((m, n), F32), jax.ShapeDtypeStruct((m, LANES), F32)],
        scratch_shapes=[pltpu.VMEM((tm, k), BF16)],
        compiler_params=_params(("parallel", "arbitrary")), name="rms_matmul_dt",
    )(x, g.reshape(1, k), w_t, w_t)


def _mm_res_kernel(n_pairs, *refs):
    res_ref = refs[2 * n_pairs]
    o_ref = refs[2 * n_pairs + 1]
    acc = res_ref[...]
    for i in range(n_pairs):
        acc = acc + _dot(refs[2 * i][...].astype(BF16), refs[2 * i + 1][...])
    o_ref[...] = acc


def matmul_residual(pairs, res, *, tm, tn):
    m, n = res.shape
    grid = (m // tm, n // tn)
    in_specs, args = [], []
    for a, w, kblk in pairs:
        k = a.shape[1]
        in_specs.append(pl.BlockSpec((tm, k), lambda i, j: (i, 0)))
        in_specs.append(pl.BlockSpec((k, tn), lambda i, j, kblk=kblk: (kblk, j)))
        args += [a, w]
    in_specs.append(pl.BlockSpec((tm, tn), lambda i, j: (i, j)))
    args.append(res)
    return pl.pallas_call(
        functools.partial(_mm_res_kernel, len(pairs)), grid=grid, in_specs=in_specs,
        out_specs=pl.BlockSpec((tm, tn), lambda i, j: (i, j)),
        out_shape=jax.ShapeDtypeStruct((m, n), F32),
        compiler_params=_params(("parallel", "parallel")), name="matmul_residual",
    )(*args)


def _ffn_step(x_ref, g2_ref, wg_ref, wu_ref, wd_ref, gf_ref, o_ref, hn_ref):
    f = pl.program_id(1)

    @pl.when(f == 0)
    def _():
        hn_ref[...] = _rms(x_ref[...], g2_ref[...]).astype(BF16)
        o_ref[...] = jnp.zeros_like(o_ref)

    hn = hn_ref[...]
    act = _silu(_dot(hn, wg_ref[...])) * _dot(hn, wu_ref[...])
    o_ref[...] += _dot(act.astype(BF16), wd_ref[...])

    @pl.when(f == pl.num_programs(1) - 1)
    def _():
        o_ref[...] = _rms(x_ref[...] + o_ref[...], gf_ref[...])


def _ffn_kernel(x_ref, g2_ref, wg_ref, wu_ref, wd_ref, gf_ref, o_ref, hn_ref):
    _ffn_step(x_ref, g2_ref, wg_ref, wu_ref, wd_ref, gf_ref, o_ref, hn_ref)


def ffn_final(x, g2, wg, wu, wd, gf, *, tm, tf):
    m, d = x.shape
    dff = wg.shape[1]
    grid = (m // tm, dff // tf)
    return pl.pallas_call(
        _ffn_kernel, grid=grid,
        in_specs=[
            pl.BlockSpec((tm, d), lambda i, f: (i, 0)),
            pl.BlockSpec((1, d), lambda i, f: (0, 0)),
            pl.BlockSpec((d, tf), lambda i, f: (0, f)),
            pl.BlockSpec((d, tf), lambda i, f: (0, f)),
            pl.BlockSpec((tf, d), lambda i, f: (f, 0)),
            pl.BlockSpec((1, d), lambda i, f: (0, 0)),
        ],
        out_specs=pl.BlockSpec((tm, d), lambda i, f: (i, 0)),
        out_shape=jax.ShapeDtypeStruct((m, d), F32),
        scratch_shapes=[pltpu.VMEM((tm, d), BF16)],
        compiler_params=_params(("parallel", "arbitrary")), name="ffn_final",
    )(x, g2.reshape(1, d), wg, wu, wd, gf.reshape(1, d))


def _mix_mem_kernel(x_ref, ys_ref, ya_ref, wo_ref, g_ref, wq_ref, mkv_ref, wm_ref, o_ref):
    half = ys_ref.shape[1]
    x1 = x_ref[...] + _dot(ys_ref[...], wo_ref[0:half, :]) + _dot(ya_ref[...], wo_ref[half:, :])
    hq = _dot(_rms(x1, g_ref[...]).astype(BF16), wq_ref[...])
    scale = MEM_HEAD_DIM ** -0.5
    outs = []
    for h in range(MEM_HEADS):
        sl = slice(h * MEM_HEAD_DIM, (h + 1) * MEM_HEAD_DIM)
        k = mkv_ref[:, sl].astype(BF16)
        v = mkv_ref[:, D_MEM + h * MEM_HEAD_DIM:D_MEM + (h + 1) * MEM_HEAD_DIM].astype(BF16)
        s = lax.dot_general(hq[:, sl].astype(BF16), k, (((1,), (1,)), ((), ())), preferred_element_type=F32) * scale
        p = jnp.exp(s - jnp.max(s, axis=-1, keepdims=True))
        p = p / jnp.sum(p, axis=-1, keepdims=True)
        outs.append(_dot(p.astype(BF16), v).astype(BF16))
    o_ref[...] = x1 + _dot(jnp.concatenate(outs, axis=1), wm_ref[...])


def mix_mem_fused(x, y_ssm, y_att, w_out, ln_mem_g, w_mq, mkv, w_mo, *, tm):
    m, d = x.shape
    half = y_ssm.shape[1]
    row = lambda width: pl.BlockSpec((tm, width), lambda i: (i, 0))
    whole = lambda a: pl.BlockSpec(a.shape, lambda i: (0, 0), pipeline_mode=pl.Buffered(1))
    g = ln_mem_g.reshape(1, d)
    return pl.pallas_call(
        _mix_mem_kernel, grid=(m // tm,),
        in_specs=[row(d), row(half), row(half), whole(w_out), whole(g), whole(w_mq), whole(mkv), whole(w_mo)],
        out_specs=row(d),
        out_shape=jax.ShapeDtypeStruct((m, d), F32),
        compiler_params=_params(("parallel",)), name="mix_mem_fused",
    )(x, y_ssm, y_att, w_out, g, w_mq, mkv, w_mo)


def _head_cols(x, g, width):
    rows = x.shape[0]
    parts = [jnp.broadcast_to(x[:, 4 * g + i:4 * g + i + 1], (rows, width)) for i in range(4)]
    return jnp.concatenate(parts, axis=1)


def _ssd_prompt_kernel(n_cast, z_ref, xs_ref, bc_ref, dt_ref, cw_ref, cb_ref, dtb_ref, alog_ref, dskip_ref, ng_ref,
                       *rest):
    cast_in = rest[:n_cast]
    y_ref, hfin_ref, convo_ref = rest[n_cast:n_cast + 3]
    cast_out = rest[n_cast + 3:2 * n_cast + 3]
    cbuf, st_ref = rest[2 * n_cast + 3:]
    c = pl.program_id(0)
    L = SSD_CHUNK
    P = HEAD_DIM
    GW = 4 * P

    @pl.when(c == 0)
    def _():
        cbuf[0:SUBLANES, :] = jnp.zeros((SUBLANES, XBC_DIM), F32)
        st_ref[...] = jnp.zeros_like(st_ref)

    cbuf[SUBLANES:SUBLANES + L, 0:D_SSM] = xs_ref[...]
    cbuf[SUBLANES:SUBLANES + L, D_SSM:XBC_DIM] = bc_ref[...]
    up = cbuf[...]
    taps = [(up if j == CONV_W - 1 else pltpu.roll(up, CONV_W - 1 - j, 0))[SUBLANES:SUBLANES + L] * cw_ref[j:j + 1, :]
            for j in range(CONV_W)]
    conv = cb_ref[...] + (((taps[0] + taps[1]) + taps[2]) + taps[3])

    @pl.when(c == pl.num_programs(0) - 1)
    def _():
        convo_ref[...] = cbuf[pl.ds(L + SUBLANES - (CONV_W - 1), CONV_W - 1), :]

    cbuf[0:SUBLANES, :] = cbuf[L:L + SUBLANES, :]

    act = _silu(conv)
    xs = act[:, 0:D_SSM]
    dt = _softplus(dt_ref[...] + dtb_ref[...])
    a = dt * (-jnp.exp(alog_ref[...]))
    ri = lax.broadcasted_iota(jnp.int32, (L, L), 0)
    ci = lax.broadcasted_iota(jnp.int32, (L, L), 1)
    causal = ri >= ci
    acum = jnp.dot(causal.astype(F32), a, precision=lax.Precision.HIGHEST, preferred_element_type=F32)
    acum_t = acum.T
    dt_t = dt.T
    last = acum[L - 1:L, :]
    ea = jnp.exp(acum)
    wend = jnp.exp(last - acum) * dt
    cdec = jnp.exp(last)
    lane = lax.broadcasted_iota(jnp.int32, (L, GW), 1)
    head_lanes = [(lane >= i * P) & (lane < (i + 1) * P) for i in range(4)]

    for src, dst in zip(cast_in, cast_out):
        dst[...] = src[...].astype(dst.dtype)

    ys = []
    for g in range(SSM_GROUPS):
        bg = act[:, D_SSM + g * SSM_STATE:D_SSM + (g + 1) * SSM_STATE]
        cg = act[:, D_SSM + SSM_GROUPS * SSM_STATE + g * SSM_STATE:
                 D_SSM + SSM_GROUPS * SSM_STATE + (g + 1) * SSM_STATE]
        bgb = bg.astype(BF16)
        cgb = cg.astype(BF16)
        cb = lax.dot_general(cgb, bgb, (((1,), (1,)), ((), ())), preferred_element_type=F32)
        xg = xs[:, g * GW:(g + 1) * GW]
        yd = jnp.zeros((L, GW), F32)
        for i in range(4):
            h = 4 * g + i
            seg = acum[:, h:h + 1] - acum_t[h:h + 1, :]
            decay = jnp.exp(jnp.where(causal, seg, NEG))
            w = (cb * decay * dt_t[h:h + 1, :]).astype(BF16)
            xm = jnp.where(head_lanes[i], xg, 0.0).astype(BF16)
            yd = yd + _dot(w, xm)
        st = st_ref[g]
        yoff = _dot(cgb, st.astype(BF16)) * _head_cols(ea, g, P)
        ys.append(yd + yoff)
        xw = (xg * _head_cols(wend, g, P)).astype(BF16)
        new = _dot(bg.T.astype(BF16), xw)
        cd = jnp.concatenate([jnp.broadcast_to(cdec[:, 4 * g + i:4 * g + i + 1], (1, P)) for i in range(4)], axis=1)
        st_ref[g] = st * cd + new

    y = jnp.concatenate(ys, axis=1) + dskip_ref[...] * xs
    y = y * _silu(z_ref[...])
    outs = []
    for g in range(SSM_GROUPS):
        yg = y[:, g * GW:(g + 1) * GW]
        outs.append(yg * lax.rsqrt(jnp.mean(yg * yg, axis=-1, keepdims=True) + RMS_EPS))
    y_ref[...] = (jnp.concatenate(outs, axis=1) * ng_ref[...]).astype(y_ref.dtype)

    @pl.when(c == pl.num_programs(0) - 1)
    def _():
        for g in range(SSM_GROUPS):
            stg = st_ref[g]
            for i in range(4):
                hfin_ref[4 * g + i] = stg[:, i * P:(i + 1) * P].T


def _pad_lanes(v):
    return jnp.pad(v.astype(F32), (0, LANES - v.shape[0])).reshape(1, LANES)


BF16_ROWS = 16


def _cast_rows_per_step(rows, steps):
    return min(r for r in range(BF16_ROWS, rows + 1, BF16_ROWS) if rows % r == 0 and r * steps >= rows)


def ssd_prompt(proj, dt_raw, conv_w, conv_b, dt_bias, a_log, d_skip, norm_g, cast_weights=()):
    s = proj.shape[0]
    L = SSD_CHUNK
    steps = s // L
    col = lambda off: off // 1024
    const = lambda shape: pl.BlockSpec(shape, lambda c: (0,) * len(shape))
    cast_specs = []
    for w in cast_weights:
        rows = _cast_rows_per_step(w.shape[0], steps)
        last = w.shape[0] // rows - 1
        cast_specs.append(pl.BlockSpec((rows, w.shape[1]), lambda c, last=last: (jnp.minimum(c, last), 0)))
    return pl.pallas_call(
        functools.partial(_ssd_prompt_kernel, len(cast_weights)), grid=(steps,),
        in_specs=[
            pl.BlockSpec((L, D_SSM), lambda c: (c, col(COL_Z))),
            pl.BlockSpec((L, D_SSM), lambda c: (c, col(COL_XBC))),
            pl.BlockSpec((L, XBC_DIM - D_SSM), lambda c: (c, col(COL_XBC + D_SSM))),
            pl.BlockSpec((L, LANES), lambda c: (c, 0)),
            const((CONV_W, XBC_DIM)), const((1, XBC_DIM)), const((1, LANES)), const((1, LANES)),
            const((1, D_SSM)), const((1, D_SSM)),
        ] + cast_specs,
        out_specs=[
            pl.BlockSpec((L, D_SSM), lambda c: (c, 0)),
            const((SSM_HEADS, HEAD_DIM, SSM_STATE)),
            const((CONV_W - 1, XBC_DIM)),
        ] + cast_specs,
        out_shape=[
            jax.ShapeDtypeStruct((s, D_SSM), BF16),
            jax.ShapeDtypeStruct((SSM_HEADS, HEAD_DIM, SSM_STATE), F32),
            jax.ShapeDtypeStruct((CONV_W - 1, XBC_DIM), F32),
        ] + [jax.ShapeDtypeStruct(w.shape, BF16) for w in cast_weights],
        scratch_shapes=[
            pltpu.VMEM((L + SUBLANES, XBC_DIM), F32),
            pltpu.VMEM((SSM_GROUPS, SSM_STATE, 4 * HEAD_DIM), F32),
        ],
        compiler_params=_params(("arbitrary",)), name="ssd_prompt",
    )(proj, proj, proj, dt_raw, conv_w, conv_b.reshape(1, XBC_DIM), _pad_lanes(dt_bias), _pad_lanes(a_log),
      jnp.repeat(d_skip, HEAD_DIM).reshape(1, D_SSM), norm_g.reshape(1, D_SSM), *cast_weights)


DIL_BLK = 128
DIL_GROUP = 8


def _dil_kernel(q_ref, k_ref, v_ref, o_ref, acc_ref, l_ref, ma_ref, mb_ref, bias_ref):
    s_len = q_ref.shape[0]
    B = DIL_BLK
    scale = HEAD_DIM ** -0.5
    lane = lax.broadcasted_iota(jnp.int32, (B, LANES), 1)
    head_a = lane < HEAD_DIM
    qi = lax.broadcasted_iota(jnp.int32, (2 * B, 2 * B), 0) & (B - 1)
    ki = lax.broadcasted_iota(jnp.int32, (2 * B, 2 * B), 1)
    band = (ki >= qi) & (ki <= qi + B)
    bias_ref[0] = jnp.where(band & (ki >= B), 0.0, NEG)
    bias_ref[1] = jnp.where(band, 0.0, NEG)
    ones = jnp.ones((2 * B, LANES), BF16)

    def comb(x):
        return jnp.where(head_a, x[:B], x[B:])

    def rows(ref, start, d):
        if d == 1:
            return ref[pl.ds(start, B), :]
        return ref[pl.ds(start, B, stride=d), :]

    def store_rows(ref, start, d, val):
        if d == 1:
            ref[pl.ds(start, B), :] = val
        else:
            ref[pl.ds(start, B, stride=d), :] = val

    order = sorted(DILATED_BRANCHES, key=lambda wd: -wd[1])
    assert order[-1][1] == 1
    for bi, (w, d) in enumerate(order):
        assert w // d == B
        first_branch = bi == 0
        last_branch = bi == len(order) - 1
        per_res = s_len // (B * d)
        G = min(DIL_GROUP, per_res)
        assert per_res % G == 0

        def body(t, carry, d=d, first_branch=first_branch, last_branch=last_branch, per_res=per_res, G=G):
            r = t // (per_res // G)
            n0 = (t % (per_res // G)) * G
            starts = [(n0 + j) * (B * d) + r for j in range(G)]
            prev0 = jnp.maximum(n0 - 1, 0) * (B * d) + r
            if d == 1:
                starts = [pl.multiple_of(st, B) for st in starts]
                prev0 = pl.multiple_of(prev0, B)
            kb = [rows(k_ref, st, d).astype(BF16) for st in [prev0] + starts]
            vb = [rows(v_ref, st, d).astype(BF16) for st in [prev0] + starts]
            for j in range(G):
                start = starts[j]
                q = rows(q_ref, start, d) * scale
                qs = jnp.concatenate([jnp.where(head_a, q, 0.0), jnp.where(head_a, 0.0, q)], axis=0).astype(BF16)
                kk = jnp.concatenate([kb[j], kb[j + 1]], axis=0)
                s = lax.dot_general(qs, kk, (((1,), (1,)), ((), ())), preferred_element_type=F32)
                s = s + (bias_ref[jnp.minimum(n0, 1)] if j == 0 else bias_ref[1])
                m_row = jnp.broadcast_to(jnp.max(s, axis=-1, keepdims=True), (2 * B, LANES))
                if first_branch:
                    m_new = m_row
                else:
                    m_old = jnp.concatenate([rows(ma_ref, start, d), rows(mb_ref, start, d)], axis=0)
                    m_new = jnp.maximum(m_old, m_row)
                p = jnp.exp(s - jnp.concatenate([m_new, m_new], axis=1)).astype(BF16)
                pv = _dot(p, jnp.concatenate([jnp.concatenate([vb[j], vb[j + 1]], axis=0), ones], axis=1))
                l_new = comb(pv[:, LANES:])
                pv = comb(pv[:, :LANES])
                if first_branch:
                    acc, l_b = pv, l_new
                else:
                    alpha = comb(jnp.exp(m_old - m_new))
                    acc = rows(acc_ref, start, d) * alpha + pv
                    l_b = rows(l_ref, start, d) * alpha + l_new
                if last_branch:
                    o_ref[pl.ds(start, B), :] = (acc / l_b).astype(o_ref.dtype)
                else:
                    store_rows(acc_ref, start, d, acc)
                    store_rows(l_ref, start, d, l_b)
                    store_rows(ma_ref, start, d, m_new[:B])
                    store_rows(mb_ref, start, d, m_new[B:])
            return carry

        lax.fori_loop(0, s_len // (B * G), body, 0)


def dilated_prompt(proj):
    s = proj.shape[0]
    n_pairs = D_ATTN // LANES
    spec = lambda off: pl.BlockSpec((s, LANES), lambda h: (0, off // LANES + h))
    return pl.pallas_call(
        _dil_kernel, grid=(n_pairs,),
        in_specs=[spec(COL_Q), spec(COL_K), spec(COL_V)],
        out_specs=pl.BlockSpec((s, LANES), lambda h: (0, h)),
        out_shape=jax.ShapeDtypeStruct((s, D_ATTN), BF16),
        scratch_shapes=[pltpu.VMEM((s, LANES), F32)] * 4 + [pltpu.VMEM((2, 2 * DIL_BLK, 2 * DIL_BLK), F32)],
        compiler_params=_params(("parallel",)), name="dilated_prompt",
    )(proj, proj, proj)


SSD_SAMPLE_BB = 8


def _head_expand(x, seg_t):
    hi, mid, lo = _split3(x)
    return _dot(hi, seg_t) + _dot(mid, seg_t) + _dot(lo, seg_t)


def _ssd_sample_kernel(z_ref, xs_ref, bc_ref, dt_ref, sc_ref, st_ref, cw_ref, cb_ref, dtb_ref, alog_ref,
                       dskip_ref, ng_ref, segt_ref,
                       y_ref, ho_ref, convo_ref,
                       xs_s, t2_s, dec_s, b_s, c_s, col_s, yoff_s):
    step = pl.program_id(0)
    nb = xs_ref.shape[0]
    BB = SSD_SAMPLE_BB
    GW = 4 * HEAD_DIM
    HP = SSM_HEADS * HEAD_DIM

    @pl.when(step == 0)
    def _():
        xnew = jnp.concatenate([xs_ref[...], bc_ref[...]], axis=1)
        rows = [sc_ref[:, j * XBC_DIM:(j + 1) * XBC_DIM] for j in range(CONV_W - 1)] + [xnew]
        taps = [rows[j] * cw_ref[j:j + 1, :] for j in range(CONV_W)]
        act = _silu(cb_ref[...] + (((taps[0] + taps[1]) + taps[2]) + taps[3]))
        for j in range(CONV_W - 1):
            convo_ref[:, j * XBC_DIM:(j + 1) * XBC_DIM] = rows[j + 1]
        xs = act[:, 0:D_SSM]
        bm = act[:, D_SSM:D_SSM + SSM_GROUPS * SSM_STATE]
        cm = act[:, D_SSM + SSM_GROUPS * SSM_STATE:]
        dt = _softplus(dt_ref[...] + dtb_ref[...])
        dec = jnp.exp(dt * (-jnp.exp(alog_ref[...])))
        seg_t = segt_ref[...]
        u = _head_expand(dt, seg_t) * xs
        dec_e = _head_expand(dec, seg_t)
        xs_s[...] = xs
        dec_s[...] = dec_e
        b_s[...] = bm
        c_s[...] = cm
        for g in range(SSM_GROUPS):
            sl = slice(g * SSM_STATE, (g + 1) * SSM_STATE)
            bc = jnp.sum(bm[:, sl] * cm[:, sl], axis=-1, keepdims=True)
            t2_s[:, g * GW:(g + 1) * GW] = u[:, g * GW:(g + 1) * GW] * bc
        for j in range(HP // LANES):
            for src, base in ((u, 0), (dec_e, HP)):
                t = src[:, j * LANES:(j + 1) * LANES].T
                for k, part in enumerate(_split3(t)):
                    col_s[base + j * LANES:base + (j + 1) * LANES, k * nb:(k + 1) * nb] = part

    def per_pair(ip, carry):
        b0 = step * BB + 2 * ip
        ridx = lax.broadcasted_iota(jnp.int32, (3 * nb, 2 * SSM_STATE), 0)
        lane = lax.broadcasted_iota(jnp.int32, (3 * nb, 2 * SSM_STATE), 1)
        tgt = b0 + (lane >= SSM_STATE).astype(jnp.int32)
        sel = ((ridx == tgt) | (ridx == tgt + nb) | (ridx == tgt + 2 * nb)).astype(BF16)
        cols2 = _dot(col_s[...], sel)
        c8 = c_s[pl.ds(pl.multiple_of(step * BB, BB), BB), :]
        for j in range(2):
            i = 2 * ip + j
            b = b0 + j
            cols = cols2[:, j * SSM_STATE:(j + 1) * SSM_STATE]
            h0 = st_ref[i].reshape(HP, SSM_STATE)
            brow = b_s[pl.ds(b, 1), :]
            pick = lax.broadcasted_iota(jnp.int32, (BB, GW), 0) == i
            for g in range(SSM_GROUPS):
                rs = slice(g * GW, (g + 1) * GW)
                ls = slice(g * SSM_STATE, (g + 1) * SSM_STATE)
                h0g = h0[rs]
                hn = cols[HP + g * GW:HP + (g + 1) * GW] * h0g + cols[rs] * brow[:, ls]
                ho_ref[i, 4 * g:4 * g + 4] = hn.reshape(4, HEAD_DIM, SSM_STATE)
                yo = lax.dot_general(c8[:, ls].astype(BF16), h0g.astype(BF16), (((1,), (1,)), ((), ())),
                                     preferred_element_type=F32)
                yoff_s[pl.ds(b, 1), rs] = jnp.sum(jnp.where(pick, yo, 0.0), axis=0, keepdims=True)
        return carry

    lax.fori_loop(0, BB // 2, per_pair, 0)

    @pl.when(step == pl.num_programs(0) - 1)
    def _():
        xs = xs_s[...]
        y = dec_s[...] * yoff_s[...] + t2_s[...] + dskip_ref[...] * xs
        y = y * _silu(z_ref[...])
        for g in range(SSM_GROUPS):
            yg = y[:, g * GW:(g + 1) * GW]
            yg = yg * lax.rsqrt(jnp.mean(yg * yg, axis=-1, keepdims=True) + RMS_EPS)
            y_ref[:, g * GW:(g + 1) * GW] = (yg * ng_ref[:, g * GW:(g + 1) * GW]).astype(y_ref.dtype)


def _seg_ones(n_heads, width):
    c = jnp.arange(n_heads * width)[:, None] // width
    seg = (c == jnp.arange(LANES)[None, :]).astype(BF16)
    return seg, seg.T


def ssd_sample(proj, dt_raw, state_conv, state_ssm, conv_w, conv_b, dt_bias, a_log, d_skip, norm_g):
    nb = proj.shape[0]
    BB = SSD_SAMPLE_BB
    HP = SSM_HEADS * HEAD_DIM
    col = lambda off: off // 1024
    const = lambda shape: pl.BlockSpec(shape, lambda s: (0,) * len(shape))
    _, seg_t = _seg_ones(SSM_HEADS, HEAD_DIM)
    st_spec = pl.BlockSpec((BB, SSM_HEADS, HEAD_DIM, SSM_STATE), lambda s: (s, 0, 0, 0))
    y, h_new, conv_new = pl.pallas_call(
        _ssd_sample_kernel, grid=(nb // BB,),
        in_specs=[
            pl.BlockSpec((nb, D_SSM), lambda s: (0, col(COL_Z))),
            pl.BlockSpec((nb, D_SSM), lambda s: (0, col(COL_XBC))),
            pl.BlockSpec((nb, XBC_DIM - D_SSM), lambda s: (0, col(COL_XBC + D_SSM))),
            const((nb, LANES)), const((nb, (CONV_W - 1) * XBC_DIM)), st_spec,
            const((CONV_W, XBC_DIM)), const((1, XBC_DIM)), const((1, LANES)), const((1, LANES)),
            const((1, D_SSM)), const((1, D_SSM)), const((LANES, HP)),
        ],
        out_specs=[const((nb, D_SSM)), st_spec, const((nb, (CONV_W - 1) * XBC_DIM))],
        out_shape=[
            jax.ShapeDtypeStruct((nb, D_SSM), BF16),
            jax.ShapeDtypeStruct(state_ssm.shape, F32),
            jax.ShapeDtypeStruct((nb, (CONV_W - 1) * XBC_DIM), F32),
        ],
        scratch_shapes=[
            pltpu.VMEM((nb, HP), F32), pltpu.VMEM((nb, HP), F32), pltpu.VMEM((nb, HP), F32),
            pltpu.VMEM((nb, SSM_GROUPS * SSM_STATE), F32), pltpu.VMEM((nb, SSM_GROUPS * SSM_STATE), F32),
            pltpu.VMEM((2 * HP, 3 * nb), BF16), pltpu.VMEM((nb, HP), F32),
        ],
        compiler_params=_params(("arbitrary",)), name="ssd_sample",
    )(proj, proj, proj, dt_raw, state_conv.reshape(nb, (CONV_W - 1) * XBC_DIM), state_ssm, conv_w,
      conv_b.reshape(1, XBC_DIM), _pad_lanes(dt_bias), _pad_lanes(a_log),
      jnp.repeat(d_skip, HEAD_DIM).reshape(1, D_SSM), norm_g.reshape(1, D_SSM), seg_t)
    return y, h_new, conv_new.reshape(nb, CONV_W - 1, XBC_DIM)


WIN_UNIT_HEADS = 4
WIN_UNITS_PER_STEP = 3


def _win_unit_attention(n_branch, q4, kn4, vn4, cnt, kt, vt):
    rep = lambda t: jnp.stack([jnp.broadcast_to(t[h:h + 1, :], (SUBLANES, t.shape[1])) for h in range(t.shape[0])])
    q = rep(q4) * (HEAD_DIM ** -0.5)
    s = jnp.einsum("hqd,hdw->hqw", q.astype(BF16), kt.astype(BF16), preferred_element_type=F32)
    s = jnp.where(cnt > 0.0, s, NEG)
    s_new = jnp.sum(q * rep(kn4), axis=-1, keepdims=True)
    m = jnp.maximum(jnp.max(s, axis=-1, keepdims=True), s_new)
    p = cnt * jnp.exp(s - m)
    p_new = n_branch * jnp.exp(s_new - m)
    l = jnp.sum(p, axis=-1, keepdims=True) + p_new
    o = jnp.einsum("hqw,hdw->hqd", p.astype(BF16), vt.astype(BF16), preferred_element_type=F32)
    return (o + p_new * rep(vn4)) / l


def _ffn_stream_kernel(n_branch, x_ref, g2_ref, wg_ref, wu_ref, wd_ref, gf_ref,
                       q_ref, kn_ref, vn_ref, cnt_ref, kt_hbm, vt_hbm,
                       o_ref, oa_ref, hn_ref, kbuf, vbuf, sem):
    step = pl.program_id(0) * pl.num_programs(1) + pl.program_id(1)
    uh = WIN_UNIT_HEADS
    per_seq = kt_hbm.shape[1] // uh
    n_units = kt_hbm.shape[0] * per_seq

    def copies(u, slot):
        b = u // per_seq
        h0 = (u % per_seq) * uh
        return (pltpu.make_async_copy(kt_hbm.at[b, pl.ds(h0, uh)], kbuf.at[slot], sem.at[0, slot]),
                pltpu.make_async_copy(vt_hbm.at[b, pl.ds(h0, uh)], vbuf.at[slot], sem.at[1, slot]))

    @pl.when(step == 0)
    def _():
        for slot in range(WIN_UNITS_PER_STEP):
            for c in copies(slot, slot):
                c.start()

    cnt = cnt_ref[...][None]
    for slot in range(WIN_UNITS_PER_STEP):
        u = step * WIN_UNITS_PER_STEP + slot

        @pl.when(u < n_units)
        def _(u=u, slot=slot):
            for c in copies(u, slot):
                c.wait()
            b = u // per_seq
            h0 = (u % per_seq) * uh
            rows = pl.ds(h0, uh)
            o = _win_unit_attention(n_branch, q_ref[b, rows, :], kn_ref[b, rows, :], vn_ref[b, rows, :], cnt,
                                    kbuf[slot], vbuf[slot])
            for h in range(uh):
                oa_ref[b, pl.ds(h0 + h, 1), :] = o[h, 0:1, :]

            @pl.when(u + WIN_UNITS_PER_STEP < n_units)
            def _():
                for c in copies(u + WIN_UNITS_PER_STEP, slot):
                    c.start()

    _ffn_step(x_ref, g2_ref, wg_ref, wu_ref, wd_ref, gf_ref, o_ref, hn_ref)


def ffn_final_with_window_attention(x, g2, wg, wu, wd, gf, q, kn, vn, cache_k, cache_v, *, tm, tf):
    m, d = x.shape
    dff = wg.shape[1]
    nb, w_buf = cache_k.shape[0], cache_k.shape[1]
    grid = (m // tm, dff // tf)
    assert nb * (ATTN_HEADS // WIN_UNIT_HEADS) <= grid[0] * grid[1] * WIN_UNITS_PER_STEP
    pos = jnp.arange(w_buf)
    cnt = jnp.zeros((w_buf,), F32)
    for w, dil in DILATED_BRANCHES:
        cnt = cnt + ((pos >= w_buf - w) & ((w_buf - pos) % dil == 0)).astype(F32)
    const = lambda shape: pl.BlockSpec(shape, lambda i, f: (0,) * len(shape))
    small = const((nb, ATTN_HEADS, HEAD_DIM))
    unit = (WIN_UNITS_PER_STEP, WIN_UNIT_HEADS, HEAD_DIM, w_buf)
    y, att = pl.pallas_call(
        functools.partial(_ffn_stream_kernel, float(len(DILATED_BRANCHES))), grid=grid,
        in_specs=[
            pl.BlockSpec((tm, d), lambda i, f: (i, 0)),
            const((1, d)),
            pl.BlockSpec((d, tf), lambda i, f: (0, f)),
            pl.BlockSpec((d, tf), lambda i, f: (0, f)),
            pl.BlockSpec((tf, d), lambda i, f: (f, 0)),
            const((1, d)),
            small, small, small, const((1, w_buf)),
            pl.BlockSpec(memory_space=pl.ANY), pl.BlockSpec(memory_space=pl.ANY),
        ],
        out_specs=[pl.BlockSpec((tm, d), lambda i, f: (i, 0)), small],
        out_shape=[jax.ShapeDtypeStruct((m, d), F32), jax.ShapeDtypeStruct((nb, ATTN_HEADS, HEAD_DIM), F32)],
        scratch_shapes=[pltpu.VMEM((tm, d), BF16), pltpu.VMEM(unit, F32), pltpu.VMEM(unit, F32),
                        pltpu.SemaphoreType.DMA((2, WIN_UNITS_PER_STEP))],
        compiler_params=_params(("arbitrary", "arbitrary")), name="ffn_final_window_stream",
    )(x, g2.reshape(1, d), wg, wu, wd, gf.reshape(1, d), q, kn, vn, cnt.reshape(1, w_buf),
      jnp.transpose(cache_k, (0, 2, 3, 1)), jnp.transpose(cache_v, (0, 2, 3, 1)))
    return y, att.reshape(nb, D_ATTN)


MEM_SAMPLE_BB = 4


def _mem_sample_kernel(q_ref, k_ref, v_ref, o_ref):
    shape = (q_ref.shape[1], k_ref.shape[1])
    row = lax.broadcasted_iota(jnp.int32, shape, 0)
    col = lax.broadcasted_iota(jnp.int32, shape, 1)
    own = (col & (MEM_HEADS - 1)) == (row & (MEM_HEADS - 1))
    for i in range(q_ref.shape[0]):
        q = (q_ref[i] * (MEM_HEAD_DIM ** -0.5)).astype(BF16)
        s = lax.dot_general(q, k_ref[i].astype(BF16), (((1,), (1,)), ((), ())), preferred_element_type=F32)
        s = jnp.where(own, s, NEG)
        m = jnp.max(s, axis=-1, keepdims=True)
        p = jnp.where(own, jnp.exp(s - m), 0.0)
        l = jnp.sum(p, axis=-1, keepdims=True)
        o_ref[i] = _dot((p / l).astype(BF16), v_ref[i].astype(BF16))


def mem_attention_sample(q, cache_mk, cache_mv):
    nb = q.shape[0]
    assert MEM_HEADS & (MEM_HEADS - 1) == 0 and SUBLANES % MEM_HEADS == 0
    q8 = jnp.tile(q.reshape(nb, MEM_HEADS, MEM_HEAD_DIM), (1, SUBLANES // MEM_HEADS, 1))
    rows = MEM_LEN * MEM_HEADS
    bb = MEM_SAMPLE_BB
    small = pl.BlockSpec((bb, SUBLANES, MEM_HEAD_DIM), lambda b: (b, 0, 0))
    big = pl.BlockSpec((bb, rows, MEM_HEAD_DIM), lambda b: (b, 0, 0))
    out = pl.pallas_call(
        _mem_sample_kernel, grid=(nb // bb,), in_specs=[small, big, big], out_specs=small,
        out_shape=jax.ShapeDtypeStruct((nb, SUBLANES, MEM_HEAD_DIM), F32),
        compiler_params=_params(("parallel",)), name="mem_attention_sample",
    )(q8, cache_mk.reshape(nb, rows, MEM_HEAD_DIM), cache_mv.reshape(nb, rows, MEM_HEAD_DIM))
    return out[:, :MEM_HEADS].reshape(nb, D_MEM)


def _layer_tail(x, y_ssm, y_att, mem_attend, lw, gf, *, tm, tf):
    half = D_SSM
    x1 = matmul_residual([(y_ssm, lw["w_out"], 0), (y_att, lw["w_out"], 1)], x, tm=tm, tn=1024)
    assert y_ssm.shape[1] == half and y_att.shape[1] == half
    hq = rms_matmul(x1, lw["ln_mem_g"], lw["w_mq"], tm=tm, tn=D_MEM)
    x2 = matmul_residual([(mem_attend(hq), lw["w_mo"], 0)], x1, tm=tm, tn=1024)
    return ffn_final(x2, lw["ln2_g"], lw["w_gate"], lw["w_up"], lw["w_down"], gf, tm=min(tm, 512), tf=tf)


def kernel(x_prompt, x_sample, cache_win_k, cache_win_v, state_conv, state_ssm, cache_mem_k, cache_mem_v, mem_prompt, ln1_g, w_in, conv_w, conv_b, dt_bias, a_log, d_skip, ssm_norm_g, w_out, ln_mem_g, mem_norm_g, w_mq, w_mk, w_mv, w_mo, ln2_g, w_gate, w_up, w_down, ln_f_g):
    bp, s, d = x_prompt.shape
    nb, ts, _ = x_sample.shape
    depth = w_in.shape[0]
    assert bp == 1 and ts == 1 and depth == 1
    w_keep = min(DILATED_BRANCHES[-1][0], s)
    l = 0
    dt_lo = D_SSM + XBC_DIM
    dt_hi = dt_lo + SSM_HEADS
    w_in_t = jnp.transpose(w_in[l])
    lw = dict(ln_mem_g=ln_mem_g[l], w_mq=w_mq[l].astype(BF16), w_mo=w_mo[l].astype(BF16), ln2_g=ln2_g[l])
    w_mkv = jnp.concatenate([w_mk[l], w_mv[l]], axis=1).astype(BF16)
    ssm_w = (conv_w[l], conv_b[l], dt_bias[l], a_log[l], d_skip[l], ssm_norm_g[l])

    xs = x_sample.reshape(nb, d)
    proj_s, dt_s = rms_matmul_dt(xs, ln1_g[l], w_in_t, dt_lo, dt_hi, tm=nb, tn=1024)
    heads = lambda t: t.reshape(t.shape[0], ATTN_HEADS, HEAD_DIM)
    q_s, k_s, v_s = (heads(proj_s[:, c:c + D_ATTN]) for c in (COL_Q, COL_K, COL_V))

    xp = x_prompt.reshape(s, d)
    mkv = rms_matmul(mem_prompt.reshape(MEM_LEN, d), mem_norm_g[l], w_mkv, tm=MEM_LEN, tn=2 * D_MEM)
    proj, dt_raw = rms_matmul_dt(xp, ln1_g[l], w_in_t, dt_lo, dt_hi, tm=1024, tn=1024)
    y_ssm, h_p, conv_p, lw["w_out"], lw["w_gate"], lw["w_up"], lw["w_down"] = ssd_prompt(
        proj, dt_raw, *ssm_w, cast_weights=(w_out[l], w_gate[l], w_up[l], w_down[l]))
    y_att = dilated_prompt(proj)
    x2 = mix_mem_fused(xp, y_ssm, y_att, lw["w_out"], lw["ln_mem_g"], lw["w_mq"], mkv, lw["w_mo"], tm=512)
    y_prompt, y_att_s = ffn_final_with_window_attention(
        x2, lw["ln2_g"], lw["w_gate"], lw["w_up"], lw["w_down"], ln_f_g,
        q_s, k_s, v_s, cache_win_k[l], cache_win_v[l], tm=512, tf=512)

    y_ssm_s, h_s, conv_s = ssd_sample(proj_s, dt_s, state_conv[l], state_ssm[l], *ssm_w)
    y_sample = _layer_tail(xs, y_ssm_s, y_att_s,
                           lambda hq: mem_attention_sample(hq, cache_mem_k[l], cache_mem_v[l]), lw, ln_f_g,
                           tm=nb, tf=512)

    return (
        y_prompt.reshape(bp, s, d),
        y_sample.reshape(nb, ts, d),
        heads(proj[s - w_keep:, COL_K:COL_V])[None, None],
        heads(proj[s - w_keep:, COL_V:])[None, None],
        conv_p[None, None],
        h_p[None, None],
        mkv[:, :D_MEM].reshape(1, 1, MEM_LEN, MEM_HEADS, MEM_HEAD_DIM),
        mkv[:, D_MEM:].reshape(1, 1, MEM_LEN, MEM_HEADS, MEM_HEAD_DIM),
        k_s[None, :, None],
        v_s[None, :, None],
        conv_s[None],
        h_s[None],
    )
```

```python
import functools

import jax
import jax.numpy as jnp
from jax import lax
from jax.experimental import pallas as pl
from jax.experimental.pallas import tpu as pltpu

F32 = jnp.float32
BF16 = jnp.bfloat16

D_MODEL = 2048
D_SSM = 1024
D_ATTN = 1024
HEAD_DIM = 64
SSM_HEADS = 16
SSM_GROUPS = 4
SSM_STATE = 128
CONV_W = 4
SSD_CHUNK = 128
XBC_DIM = D_SSM + 2 * SSM_GROUPS * SSM_STATE
ATTN_HEADS = 16
DILATED_BRANCHES = ((128, 1), (512, 4), (2048, 16))
MEM_LEN = 256
MEM_HEADS = 4
MEM_HEAD_DIM = 128
D_MEM = MEM_HEADS * MEM_HEAD_DIM
RMS_EPS = 1e-5
NEG = -1e30

LANES = 128
SUBLANES = 8
VMEM_BYTES_V7X = 64 * 1024 * 1024
VMEM_LIMIT = 56 * 1024 * 1024

COL_Z = 0
COL_XBC = D_SSM
COL_Q = D_SSM + XBC_DIM
COL_K = COL_Q + D_ATTN
COL_V = COL_K + D_ATTN
D_PROJ = COL_V + D_ATTN


def _params(sem):
    return pltpu.CompilerParams(dimension_semantics=sem, vmem_limit_bytes=VMEM_LIMIT)


def _rms(x, g):
    ms = jnp.mean(x * x, axis=-1, keepdims=True)
    return x * lax.rsqrt(ms + RMS_EPS) * g


def _silu(x):
    return x * (0.5 * jnp.tanh(0.5 * x) + 0.5)


def _softplus(x):
    return jnp.maximum(x, 0.0) + jnp.log1p(jnp.exp(-jnp.abs(x)))


def _split3(x):
    hi = x.astype(BF16)
    r1 = x - hi.astype(F32)
    mid = r1.astype(BF16)
    lo = (r1 - mid.astype(F32)).astype(BF16)
    return hi, mid, lo


def _dot(a, b):
    return jnp.dot(a, b, preferred_element_type=F32)


def _rms_matmul_kernel(x_ref, g_ref, w_ref, o_ref, hn_ref):
    @pl.when(pl.program_id(1) == 0)
    def _():
        hn_ref[...] = _rms(x_ref[...], g_ref[...]).astype(BF16)

    o_ref[...] = _dot(hn_ref[...], w_ref[...])


def _dot_nt(a, b):
    return lax.dot_general(a, b, (((1,), (1,)), ((), ())), preferred_element_type=F32)


def _rms_matmul_dt_kernel(n_dt, x_ref, g_ref, wt_ref, wdt_ref, o_ref, dt_ref, hn_ref):
    @pl.when(pl.program_id(1) == 0)
    def _():
        hn = _rms(x_ref[...], g_ref[...])
        hi = hn.astype(BF16)
        lo = (hn - hi.astype(F32)).astype(BF16)
        hn_ref[...] = hi
        wd = wdt_ref[...]
        wd_hi = wd.astype(BF16)
        wd2 = jnp.concatenate([wd_hi, (wd - wd_hi.astype(F32)).astype(BF16)], axis=0)
        both = _dot_nt(hi, wd2) + _dot_nt(lo, wd2)
        dt = both[:, :LANES] + both[:, LANES:]
        dt_ref[...] = jnp.where(lax.broadcasted_iota(jnp.int32, dt.shape, 1) < n_dt, dt, 0.0)

    o_ref[...] = _dot_nt(hn_ref[...], wt_ref[...].astype(BF16))


def rms_matmul(x, g, w, *, tm, tn):
    m, k = x.shape
    n = w.shape[1]
    return pl.pallas_call(
        _rms_matmul_kernel, grid=(m // tm, n // tn),
        in_specs=[
            pl.BlockSpec((tm, k), lambda i, j: (i, 0)),
            pl.BlockSpec((1, k), lambda i, j: (0, 0)),
            pl.BlockSpec((k, tn), lambda i, j: (0, j)),
        ],
        out_specs=pl.BlockSpec((tm, tn), lambda i, j: (i, j)),
        out_shape=jax.ShapeDtypeStruct((m, n), F32), scratch_shapes=[pltpu.VMEM((tm, k), BF16)],
        compiler_params=_params(("parallel", "arbitrary")), name="rms_matmul",
    )(x, g.reshape(1, k), w)


def rms_matmul_dt(x, g, w_t, dt_lo, dt_hi, *, tm, tn):
    m, k = x.shape
    n_dt = dt_hi - dt_lo
    n = w_t.shape[0] - n_dt
    n_a = dt_lo // tn
    assert dt_lo % tn == 0 and n % tn == 0 and n_dt % SUBLANES == 0 and n_dt <= LANES
    assert dt_lo + LANES <= w_t.shape[0]
    return pl.pallas_call(
        functools.partial(_rms_matmul_dt_kernel, n_dt), grid=(m // tm, n // tn),
        in_specs=[
            pl.BlockSpec((tm, k), lambda i, j: (i, 0)),
            pl.BlockSpec((1, k), lambda i, j: (0, 0)),
            pl.BlockSpec((pl.Element(tn), pl.Element(k)),
                         lambda i, j: (pl.multiple_of(j * tn + jnp.where(j >= n_a, n_dt, 0), SUBLANES), 0)),
            pl.BlockSpec((pl.Element(LANES), pl.Element(k)), lambda i, j: (dt_lo, 0)),
        ],
        out_specs=[pl.BlockSpec((tm, tn), lambda i, j: (i, j)), pl.BlockSpec((tm, LANES), lambda i, j: (i, 0))],
        out_shape=[jax.ShapeDtypeStruct((m, n), F32), jax.ShapeDtypeStruct((m, LANES), F32)],
        scratch_shapes=[pltpu.VMEM((tm, k), BF16)],
        compiler_params=_params(("parallel", "arbitrary")), name="rms_matmul_dt",
    )(x, g.reshape(1, k), w_t, w_t)


def _mm_res_kernel(n_pairs, *refs):
    res_ref = refs[2 * n_pairs]
    o_ref = refs[2 * n_pairs + 1]
    acc = res_ref[...]
    for i in range(n_pairs):
        acc = acc + _dot(refs[2 * i][...].astype(BF16), refs[2 * i + 1][...])
    o_ref[...] = acc


def matmul_residual(pairs, res, *, tm, tn):
    m, n = res.shape
    grid = (m // tm, n // tn)
    in_specs, args = [], []
    for a, w, kblk in pairs:
        k = a.shape[1]
        in_specs.append(pl.BlockSpec((tm, k), lambda i, j: (i, 0)))
        in_specs.append(pl.BlockSpec((k, tn), lambda i, j, kblk=kblk: (kblk, j)))
        args += [a, w]
    in_specs.append(pl.BlockSpec((tm, tn), lambda i, j: (i, j)))
    args.append(res)
    return pl.pallas_call(
        functools.partial(_mm_res_kernel, len(pairs)), grid=grid, in_specs=in_specs,
        out_specs=pl.BlockSpec((tm, tn), lambda i, j: (i, j)),
        out_shape=jax.ShapeDtypeStruct((m, n), F32),
        compiler_params=_params(("parallel", "parallel")), name="matmul_residual",
    )(*args)


def _ffn_step(x_ref, g2_ref, wg_ref, wu_ref, wd_ref, gf_ref, o_ref, hn_ref):
    f = pl.program_id(1)

    @pl.when(f == 0)
    def _():
        hn_ref[...] = _rms(x_ref[...], g2_ref[...]).astype(BF16)
        o_ref[...] = jnp.zeros_like(o_ref)

    hn = hn_ref[...]
    act = _silu(_dot(hn, wg_ref[...])) * _dot(hn, wu_ref[...])
    o_ref[...] += _dot(act.astype(BF16), wd_ref[...])

    @pl.when(f == pl.num_programs(1) - 1)
    def _():
        o_ref[...] = _rms(x_ref[...] + o_ref[...], gf_ref[...])


def _ffn_kernel(x_ref, g2_ref, wg_ref, wu_ref, wd_ref, gf_ref, o_ref, hn_ref):
    _ffn_step(x_ref, g2_ref, wg_ref, wu_ref, wd_ref, gf_ref, o_ref, hn_ref)


def ffn_final(x, g2, wg, wu, wd, gf, *, tm, tf):
    m, d = x.shape
    dff = wg.shape[1]
    grid = (m // tm, dff // tf)
    return pl.pallas_call(
        _ffn_kernel, grid=grid,
        in_specs=[
            pl.BlockSpec((tm, d), lambda i, f: (i, 0)),
            pl.BlockSpec((1, d), lambda i, f: (0, 0)),
            pl.BlockSpec((d, tf), lambda i, f: (0, f)),
            pl.BlockSpec((d, tf), lambda i, f: (0, f)),
            pl.BlockSpec((tf, d), lambda i, f: (f, 0)),
            pl.BlockSpec((1, d), lambda i, f: (0, 0)),
        ],
        out_specs=pl.BlockSpec((tm, d), lambda i, f: (i, 0)),
        out_shape=jax.ShapeDtypeStruct((m, d), F32),
        scratch_shapes=[pltpu.VMEM((tm, d), BF16)],
        compiler_params=_params(("parallel", "arbitrary")), name="ffn_final",
    )(x, g2.reshape(1, d), wg, wu, wd, gf.reshape(1, d))


def _mix_mem_kernel(x_ref, ys_ref, ya_ref, wo_ref, g_ref, wq_ref, mkv_ref, wm_ref, o_ref):
    half = ys_ref.shape[1]
    x1 = x_ref[...] + _dot(ys_ref[...], wo_ref[0:half, :]) + _dot(ya_ref[...], wo_ref[half:, :])
    hq = _dot(_rms(x1, g_ref[...]).astype(BF16), wq_ref[...])
    scale = MEM_HEAD_DIM ** -0.5
    outs = []
    for h in range(MEM_HEADS):
        sl = slice(h * MEM_HEAD_DIM, (h + 1) * MEM_HEAD_DIM)
        k = mkv_ref[:, sl].astype(BF16)
        v = mkv_ref[:, D_MEM + h * MEM_HEAD_DIM:D_MEM + (h + 1) * MEM_HEAD_DIM].astype(BF16)
        s = lax.dot_general(hq[:, sl].astype(BF16), k, (((1,), (1,)), ((), ())), preferred_element_type=F32) * scale
        p = jnp.exp(s - jnp.max(s, axis=-1, keepdims=True))
        p = p / jnp.sum(p, axis=-1, keepdims=True)
        outs.append(_dot(p.astype(BF16), v).astype(BF16))
    o_ref[...] = x1 + _dot(jnp.concatenate(outs, axis=1), wm_ref[...])


def mix_mem_fused(x, y_ssm, y_att, w_out, ln_mem_g, w_mq, mkv, w_mo, *, tm):
    m, d = x.shape
    half = y_ssm.shape[1]
    row = lambda width: pl.BlockSpec((tm, width), lambda i: (i, 0))
    whole = lambda a: pl.BlockSpec(a.shape, lambda i: (0, 0), pipeline_mode=pl.Buffered(1))
    g = ln_mem_g.reshape(1, d)
    return pl.pallas_call(
        _mix_mem_kernel, grid=(m // tm,),
        in_specs=[row(d), row(half), row(half), whole(w_out), whole(g), whole(w_mq), whole(mkv), whole(w_mo)],
        out_specs=row(d),
        out_shape=jax.ShapeDtypeStruct((m, d), F32),
        compiler_params=_params(("parallel",)), name="mix_mem_fused",
    )(x, y_ssm, y_att, w_out, g, w_mq, mkv, w_mo)


def _head_cols(x, g, width):
    rows = x.shape[0]
    parts = [jnp.broadcast_to(x[:, 4 * g + i:4 * g + i + 1], (rows, width)) for i in range(4)]
    return jnp.concatenate(parts, axis=1)


def _ssd_prompt_kernel(n_cast, n_chunks, z_ref, xs_ref, bc_ref, dt_ref, cw_ref, cb_ref, dtb_ref, alog_ref,
                       dskip_ref, ng_ref, *rest):
    cast_in = rest[:n_cast]
    y_ref, hfin_ref, convo_ref = rest[n_cast:n_cast + 3]
    cast_out = rest[n_cast + 3:2 * n_cast + 3]
    cbuf, st_ref = rest[2 * n_cast + 3:]
    c = pl.program_id(0)
    L = SSD_CHUNK
    P = HEAD_DIM
    GW = 4 * P

    @pl.when(c == 0)
    def _():
        cbuf[0:SUBLANES, :] = jnp.zeros((SUBLANES, XBC_DIM), F32)
        st_ref[...] = jnp.zeros_like(st_ref)

    ri = lax.broadcasted_iota(jnp.int32, (L, L), 0)
    ci = lax.broadcasted_iota(jnp.int32, (L, L), 1)
    causal = ri >= ci
    lane = lax.broadcasted_iota(jnp.int32, (L, GW), 1)
    head_lanes = [(lane >= i * P) & (lane < (i + 1) * P) for i in range(4)]

    for k in range(n_chunks):
        rs = slice(k * L, (k + 1) * L)
        cbuf[SUBLANES:SUBLANES + L, 0:D_SSM] = xs_ref[rs, :]
        cbuf[SUBLANES:SUBLANES + L, D_SSM:XBC_DIM] = bc_ref[rs, :]
        up = cbuf[...]
        taps = [(up if j == CONV_W - 1 else pltpu.roll(up, CONV_W - 1 - j, 0))[SUBLANES:SUBLANES + L]
                * cw_ref[j:j + 1, :] for j in range(CONV_W)]
        conv = cb_ref[...] + (((taps[0] + taps[1]) + taps[2]) + taps[3])

        if k == n_chunks - 1:
            @pl.when(c == pl.num_programs(0) - 1)
            def _():
                convo_ref[...] = cbuf[pl.ds(L + SUBLANES - (CONV_W - 1), CONV_W - 1), :]

        cbuf[0:SUBLANES, :] = cbuf[L:L + SUBLANES, :]

        act = _silu(conv)
        xs = act[:, 0:D_SSM]
        dt = _softplus(dt_ref[rs, :] + dtb_ref[...])
        a = dt * (-jnp.exp(alog_ref[...]))
        acum = jnp.dot(causal.astype(F32), a, precision=lax.Precision.HIGHEST, preferred_element_type=F32)
        acum_t = acum.T
        dt_t = dt.T
        last = acum[L - 1:L, :]
        ea = jnp.exp(acum)
        wend = jnp.exp(last - acum) * dt
        cdec = jnp.exp(last)

        if k == 0:
            for src, dst in zip(cast_in, cast_out):
                dst[...] = src[...].astype(dst.dtype)

        ys = []
        for g in range(SSM_GROUPS):
            bg = act[:, D_SSM + g * SSM_STATE:D_SSM + (g + 1) * SSM_STATE]
            cg = act[:, D_SSM + SSM_GROUPS * SSM_STATE + g * SSM_STATE:
                     D_SSM + SSM_GROUPS * SSM_STATE + (g + 1) * SSM_STATE]
            bgb = bg.astype(BF16)
            cgb = cg.astype(BF16)
            cb = lax.dot_general(cgb, bgb, (((1,), (1,)), ((), ())), preferred_element_type=F32)
            xg = xs[:, g * GW:(g + 1) * GW]
            yd = jnp.zeros((L, GW), F32)
            for i in range(4):
                h = 4 * g + i
                seg = acum[:, h:h + 1] - acum_t[h:h + 1, :]
                decay = jnp.exp(jnp.where(causal, seg, NEG))
                w = (cb * decay * dt_t[h:h + 1, :]).astype(BF16)
                xm = jnp.where(head_lanes[i], xg, 0.0).astype(BF16)
                yd = yd + _dot(w, xm)
            st = st_ref[g]
            yoff = _dot(cgb, st.astype(BF16)) * _head_cols(ea, g, P)
            ys.append(yd + yoff)
            xw = (xg * _head_cols(wend, g, P)).astype(BF16)
            new = _dot(bg.T.astype(BF16), xw)
            cd = jnp.concatenate(
                [jnp.broadcast_to(cdec[:, 4 * g + i:4 * g + i + 1], (1, P)) for i in range(4)], axis=1)
            st_ref[g] = st * cd + new

        y = jnp.concatenate(ys, axis=1) + dskip_ref[...] * xs
        y = y * _silu(z_ref[rs, :])
        outs = []
        for g in range(SSM_GROUPS):
            yg = y[:, g * GW:(g + 1) * GW]
            outs.append(yg * lax.rsqrt(jnp.mean(yg * yg, axis=-1, keepdims=True) + RMS_EPS))
        y_ref[rs, :] = (jnp.concatenate(outs, axis=1) * ng_ref[...]).astype(y_ref.dtype)

    @pl.when(c == pl.num_programs(0) - 1)
    def _():
        for g in range(SSM_GROUPS):
            stg = st_ref[g]
            for i in range(4):
                hfin_ref[4 * g + i] = stg[:, i * P:(i + 1) * P].T


def _pad_lanes(v):
    return jnp.pad(v.astype(F32), (0, LANES - v.shape[0])).reshape(1, LANES)


BF16_ROWS = 16
SSD_CHUNKS_PER_STEP = 4


def _cast_rows_per_step(rows, steps):
    return min(r for r in range(BF16_ROWS, rows + 1, BF16_ROWS) if rows % r == 0 and r * steps >= rows)


def ssd_prompt(proj, dt_raw, conv_w, conv_b, dt_bias, a_log, d_skip, norm_g, cast_weights=()):
    s = proj.shape[0]
    n_chunks = SSD_CHUNKS_PER_STEP if s % (SSD_CHUNKS_PER_STEP * SSD_CHUNK) == 0 else 1
    L = n_chunks * SSD_CHUNK
    steps = s // L
    col = lambda off: off // 1024
    const = lambda shape: pl.BlockSpec(shape, lambda c: (0,) * len(shape))
    cast_specs = []
    for w in cast_weights:
        rows = _cast_rows_per_step(w.shape[0], steps)
        last = w.shape[0] // rows - 1
        cast_specs.append(pl.BlockSpec((rows, w.shape[1]), lambda c, last=last: (jnp.minimum(c, last), 0)))
    return pl.pallas_call(
        functools.partial(_ssd_prompt_kernel, len(cast_weights), n_chunks), grid=(steps,),
        in_specs=[
            pl.BlockSpec((L, D_SSM), lambda c: (c, col(COL_Z))),
            pl.BlockSpec((L, D_SSM), lambda c: (c, col(COL_XBC))),
            pl.BlockSpec((L, XBC_DIM - D_SSM), lambda c: (c, col(COL_XBC + D_SSM))),
            pl.BlockSpec((L, LANES), lambda c: (c, 0)),
            const((CONV_W, XBC_DIM)), const((1, XBC_DIM)), const((1, LANES)), const((1, LANES)),
            const((1, D_SSM)), const((1, D_SSM)),
        ] + cast_specs,
        out_specs=[
            pl.BlockSpec((L, D_SSM), lambda c: (c, 0)),
            const((SSM_HEADS, HEAD_DIM, SSM_STATE)),
            const((CONV_W - 1, XBC_DIM)),
        ] + cast_specs,
        out_shape=[
            jax.ShapeDtypeStruct((s, D_SSM), BF16),
            jax.ShapeDtypeStruct((SSM_HEADS, HEAD_DIM, SSM_STATE), F32),
            jax.ShapeDtypeStruct((CONV_W - 1, XBC_DIM), F32),
        ] + [jax.ShapeDtypeStruct(w.shape, BF16) for w in cast_weights],
        scratch_shapes=[
            pltpu.VMEM((SSD_CHUNK + SUBLANES, XBC_DIM), F32),
            pltpu.VMEM((SSM_GROUPS, SSM_STATE, 4 * HEAD_DIM), F32),
        ],
        compiler_params=_params(("arbitrary",)), name="ssd_prompt",
    )(proj, proj, proj, dt_raw, conv_w, conv_b.reshape(1, XBC_DIM), _pad_lanes(dt_bias), _pad_lanes(a_log),
      jnp.repeat(d_skip, HEAD_DIM).reshape(1, D_SSM), norm_g.reshape(1, D_SSM), *cast_weights)


DIL_BLK = 128
DIL_GROUP = 8


def _dil_kernel(q_ref, k_ref, v_ref, o_ref, acc_ref, l_ref, ma_ref, mb_ref, bias_ref):
    s_len = q_ref.shape[0]
    B = DIL_BLK
    scale = HEAD_DIM ** -0.5
    lane = lax.broadcasted_iota(jnp.int32, (B, LANES), 1)
    head_a = lane < HEAD_DIM
    qi = lax.broadcasted_iota(jnp.int32, (2 * B, 2 * B), 0) & (B - 1)
    ki = lax.broadcasted_iota(jnp.int32, (2 * B, 2 * B), 1)
    band = (ki >= qi) & (ki <= qi + B)
    bias_ref[0] = jnp.where(band & (ki >= B), 0.0, NEG)
    bias_ref[1] = jnp.where(band, 0.0, NEG)
    ones = jnp.ones((2 * B, LANES), BF16)

    def comb(x):
        return jnp.where(head_a, x[:B], x[B:])

    def rows(ref, start, d):
        if d == 1:
            return ref[pl.ds(start, B), :]
        return ref[pl.ds(start, B, stride=d), :]

    def store_rows(ref, start, d, val):
        if d == 1:
            ref[pl.ds(start, B), :] = val
        else:
            ref[pl.ds(start, B, stride=d), :] = val

    order = sorted(DILATED_BRANCHES, key=lambda wd: -wd[1])
    assert order[-1][1] == 1
    for bi, (w, d) in enumerate(order):
        assert w // d == B
        first_branch = bi == 0
        last_branch = bi == len(order) - 1
        per_res = s_len // (B * d)
        G = min(DIL_GROUP, per_res)
        assert per_res % G == 0

        def body(t, carry, d=d, first_branch=first_branch, last_branch=last_branch, per_res=per_res, G=G):
            r = t // (per_res // G)
            n0 = (t % (per_res // G)) * G
            starts = [(n0 + j) * (B * d) + r for j in range(G)]
            prev0 = jnp.maximum(n0 - 1, 0) * (B * d) + r
            if d == 1:
                starts = [pl.multiple_of(st, B) for st in starts]
                prev0 = pl.multiple_of(prev0, B)
            kb = [rows(k_ref, st, d).astype(BF16) for st in [prev0] + starts]
            vb = [rows(v_ref, st, d).astype(BF16) for st in [prev0] + starts]
            for j in range(G):
                start = starts[j]
                q = rows(q_ref, start, d) * scale
                qs = jnp.concatenate([jnp.where(head_a, q, 0.0), jnp.where(head_a, 0.0, q)], axis=0).astype(BF16)
                kk = jnp.concatenate([kb[j], kb[j + 1]], axis=0)
                s = lax.dot_general(qs, kk, (((1,), (1,)), ((), ())), preferred_element_type=F32)
                s = s + (bias_ref[jnp.minimum(n0, 1)] if j == 0 else bias_ref[1])
                m_row = jnp.broadcast_to(jnp.max(s, axis=-1, keepdims=True), (2 * B, LANES))
                if first_branch:
                    m_new = m_row
                else:
                    m_old = jnp.concatenate([rows(ma_ref, start, d), rows(mb_ref, start, d)], axis=0)
                    m_new = jnp.maximum(m_old, m_row)
                p = jnp.exp(s - jnp.concatenate([m_new, m_new], axis=1)).astype(BF16)
                pv = _dot(p, jnp.concatenate([jnp.concatenate([vb[j], vb[j + 1]], axis=0), ones], axis=1))
                l_new = comb(pv[:, LANES:])
                pv = comb(pv[:, :LANES])
                if first_branch:
                    acc, l_b = pv, l_new
                else:
                    alpha = comb(jnp.exp(m_old - m_new))
                    acc = rows(acc_ref, start, d) * alpha + pv
                    l_b = rows(l_ref, start, d) * alpha + l_new
                if last_branch:
                    o_ref[pl.ds(start, B), :] = (acc / l_b).astype(o_ref.dtype)
                else:
                    store_rows(acc_ref, start, d, acc)
                    store_rows(l_ref, start, d, l_b)
                    store_rows(ma_ref, start, d, m_new[:B])
                    store_rows(mb_ref, start, d, m_new[B:])
            return carry

        lax.fori_loop(0, s_len // (B * G), body, 0)


def dilated_prompt(proj):
    s = proj.shape[0]
    n_pairs = D_ATTN // LANES
    spec = lambda off: pl.BlockSpec((s, LANES), lambda h: (0, off // LANES + h))
    return pl.pallas_call(
        _dil_kernel, grid=(n_pairs,),
        in_specs=[spec(COL_Q), spec(COL_K), spec(COL_V)],
        out_specs=pl.BlockSpec((s, LANES), lambda h: (0, h)),
        out_shape=jax.ShapeDtypeStruct((s, D_ATTN), BF16),
        scratch_shapes=[pltpu.VMEM((s, LANES), F32)] * 4 + [pltpu.VMEM((2, 2 * DIL_BLK, 2 * DIL_BLK), F32)],
        compiler_params=_params(("parallel",)), name="dilated_prompt",
    )(proj, proj, proj)


SSD_SAMPLE_BB = 8


def _head_expand(x, seg_t):
    hi, mid, lo = _split3(x)
    return _dot(hi, seg_t) + _dot(mid, seg_t) + _dot(lo, seg_t)


def _ssd_sample_kernel(z_ref, xs_ref, bc_ref, dt_ref, sc_ref, st_ref, cw_ref, cb_ref, dtb_ref, alog_ref,
                       dskip_ref, ng_ref, segt_ref,
                       y_ref, ho_ref, convo_ref,
                       xs_s, t2_s, dec_s, b_s, c_s, col_s, yoff_s):
    step = pl.program_id(0)
    nb = xs_ref.shape[0]
    BB = SSD_SAMPLE_BB
    GW = 4 * HEAD_DIM
    HP = SSM_HEADS * HEAD_DIM

    @pl.when(step == 0)
    def _():
        xnew = jnp.concatenate([xs_ref[...], bc_ref[...]], axis=1)
        rows = [sc_ref[:, j * XBC_DIM:(j + 1) * XBC_DIM] for j in range(CONV_W - 1)] + [xnew]
        taps = [rows[j] * cw_ref[j:j + 1, :] for j in range(CONV_W)]
        act = _silu(cb_ref[...] + (((taps[0] + taps[1]) + taps[2]) + taps[3]))
        for j in range(CONV_W - 1):
            convo_ref[:, j * XBC_DIM:(j + 1) * XBC_DIM] = rows[j + 1]
        xs = act[:, 0:D_SSM]
        bm = act[:, D_SSM:D_SSM + SSM_GROUPS * SSM_STATE]
        cm = act[:, D_SSM + SSM_GROUPS * SSM_STATE:]
        dt = _softplus(dt_ref[...] + dtb_ref[...])
        dec = jnp.exp(dt * (-jnp.exp(alog_ref[...])))
        seg_t = segt_ref[...]
        u = _head_expand(dt, seg_t) * xs
        dec_e = _head_expand(dec, seg_t)
        xs_s[...] = xs
        dec_s[...] = dec_e
        b_s[...] = bm
        c_s[...] = cm
        for g in range(SSM_GROUPS):
            sl = slice(g * SSM_STATE, (g + 1) * SSM_STATE)
            bc = jnp.sum(bm[:, sl] * cm[:, sl], axis=-1, keepdims=True)
            t2_s[:, g * GW:(g + 1) * GW] = u[:, g * GW:(g + 1) * GW] * bc
        for j in range(HP // LANES):
            for src, base in ((u, 0), (dec_e, HP)):
                t = src[:, j * LANES:(j + 1) * LANES].T
                for k, part in enumerate(_split3(t)):
                    col_s[base + j * LANES:base + (j + 1) * LANES, k * nb:(k + 1) * nb] = part

    def per_pair(ip, carry):
        b0 = step * BB + 2 * ip
        ridx = lax.broadcasted_iota(jnp.int32, (3 * nb, 2 * SSM_STATE), 0)
        lane = lax.broadcasted_iota(jnp.int32, (3 * nb, 2 * SSM_STATE), 1)
        tgt = b0 + (lane >= SSM_STATE).astype(jnp.int32)
        sel = ((ridx == tgt) | (ridx == tgt + nb) | (ridx == tgt + 2 * nb)).astype(BF16)
        cols2 = _dot(col_s[...], sel)
        c8 = c_s[pl.ds(pl.multiple_of(step * BB, BB), BB), :]
        for j in range(2):
            i = 2 * ip + j
            b = b0 + j
            cols = cols2[:, j * SSM_STATE:(j + 1) * SSM_STATE]
            h0 = st_ref[i].reshape(HP, SSM_STATE)
            brow = b_s[pl.ds(b, 1), :]
            pick = lax.broadcasted_iota(jnp.int32, (BB, GW), 0) == i
            for g in range(SSM_GROUPS):
                rs = slice(g * GW, (g + 1) * GW)
                ls = slice(g * SSM_STATE, (g + 1) * SSM_STATE)
                h0g = h0[rs]
                hn = cols[HP + g * GW:HP + (g + 1) * GW] * h0g + cols[rs] * brow[:, ls]
                ho_ref[i, 4 * g:4 * g + 4] = hn.reshape(4, HEAD_DIM, SSM_STATE)
                yo = lax.dot_general(c8[:, ls].astype(BF16), h0g.astype(BF16), (((1,), (1,)), ((), ())),
                                     preferred_element_type=F32)
                yoff_s[pl.ds(b, 1), rs] = jnp.sum(jnp.where(pick, yo, 0.0), axis=0, keepdims=True)
        return carry

    lax.fori_loop(0, BB // 2, per_pair, 0)

    @pl.when(step == pl.num_programs(0) - 1)
    def _():
        xs = xs_s[...]
        y = dec_s[...] * yoff_s[...] + t2_s[...] + dskip_ref[...] * xs
        y = y * _silu(z_ref[...])
        for g in range(SSM_GROUPS):
            yg = y[:, g * GW:(g + 1) * GW]
            yg = yg * lax.rsqrt(jnp.mean(yg * yg, axis=-1, keepdims=True) + RMS_EPS)
            y_ref[:, g * GW:(g + 1) * GW] = (yg * ng_ref[:, g * GW:(g + 1) * GW]).astype(y_ref.dtype)


def _seg_ones(n_heads, width):
    c = jnp.arange(n_heads * width)[:, None] // width
    seg = (c == jnp.arange(LANES)[None, :]).astype(BF16)
    return seg, seg.T


def ssd_sample(proj, dt_raw, state_conv, state_ssm, conv_w, conv_b, dt_bias, a_log, d_skip, norm_g):
    nb = proj.shape[0]
    BB = SSD_SAMPLE_BB
    HP = SSM_HEADS * HEAD_DIM
    col = lambda off: off // 1024
    const = lambda shape: pl.BlockSpec(shape, lambda s: (0,) * len(shape))
    _, seg_t = _seg_ones(SSM_HEADS, HEAD_DIM)
    st_spec = pl.BlockSpec((BB, SSM_HEADS, HEAD_DIM, SSM_STATE), lambda s: (s, 0, 0, 0))
    y, h_new, conv_new = pl.pallas_call(
        _ssd_sample_kernel, grid=(nb // BB,),
        in_specs=[
            pl.BlockSpec((nb, D_SSM), lambda s: (0, col(COL_Z))),
            pl.BlockSpec((nb, D_SSM), lambda s: (0, col(COL_XBC))),
            pl.BlockSpec((nb, XBC_DIM - D_SSM), lambda s: (0, col(COL_XBC + D_SSM))),
            const((nb, LANES)), const((nb, (CONV_W - 1) * XBC_DIM)), st_spec,
            const((CONV_W, XBC_DIM)), const((1, XBC_DIM)), const((1, LANES)), const((1, LANES)),
            const((1, D_SSM)), const((1, D_SSM)), const((LANES, HP)),
        ],
        out_specs=[const((nb, D_SSM)), st_spec, const((nb, (CONV_W - 1) * XBC_DIM))],
        out_shape=[
            jax.ShapeDtypeStruct((nb, D_SSM), BF16),
            jax.ShapeDtypeStruct(state_ssm.shape, F32),
            jax.ShapeDtypeStruct((nb, (CONV_W - 1) * XBC_DIM), F32),
        ],
        scratch_shapes=[
            pltpu.VMEM((nb, HP), F32), pltpu.VMEM((nb, HP), F32), pltpu.VMEM((nb, HP), F32),
            pltpu.VMEM((nb, SSM_GROUPS * SSM_STATE), F32), pltpu.VMEM((nb, SSM_GROUPS * SSM_STATE), F32),
            pltpu.VMEM((2 * HP, 3 * nb), BF16), pltpu.VMEM((nb, HP), F32),
        ],
        compiler_params=_params(("arbitrary",)), name="ssd_sample",
    )(proj, proj, proj, dt_raw, state_conv.reshape(nb, (CONV_W - 1) * XBC_DIM), state_ssm, conv_w,
      conv_b.reshape(1, XBC_DIM), _pad_lanes(dt_bias), _pad_lanes(a_log),
      jnp.repeat(d_skip, HEAD_DIM).reshape(1, D_SSM), norm_g.reshape(1, D_SSM), seg_t)
    return y, h_new, conv_new.reshape(nb, CONV_W - 1, XBC_DIM)


WIN_UNIT_HEADS = 4
WIN_UNITS_PER_STEP = 3


def _win_unit_attention(n_branch, q4, kn4, vn4, cnt, kt, vt):
    rep = lambda t: jnp.stack([jnp.broadcast_to(t[h:h + 1, :], (SUBLANES, t.shape[1])) for h in range(t.shape[0])])
    q = rep(q4) * (HEAD_DIM ** -0.5)
    s = jnp.einsum("hqd,hdw->hqw", q.astype(BF16), kt.astype(BF16), preferred_element_type=F32)
    s = jnp.where(cnt > 0.0, s, NEG)
    s_new = jnp.sum(q * rep(kn4), axis=-1, keepdims=True)
    m = jnp.maximum(jnp.max(s, axis=-1, keepdims=True), s_new)
    p = cnt * jnp.exp(s - m)
    p_new = n_branch * jnp.exp(s_new - m)
    l = jnp.sum(p, axis=-1, keepdims=True) + p_new
    o = jnp.einsum("hqw,hdw->hqd", p.astype(BF16), vt.astype(BF16), preferred_element_type=F32)
    return (o + p_new * rep(vn4)) / l


def _ffn_stream_kernel(n_branch, x_ref, g2_ref, wg_ref, wu_ref, wd_ref, gf_ref,
                       q_ref, kn_ref, vn_ref, cnt_ref, kt_hbm, vt_hbm,
                       o_ref, oa_ref, hn_ref, kbuf, vbuf, sem):
    step = pl.program_id(0) * pl.num_programs(1) + pl.program_id(1)
    uh = WIN_UNIT_HEADS
    per_seq = kt_hbm.shape[1] // uh
    n_units = kt_hbm.shape[0] * per_seq

    def copies(u, slot):
        b = u // per_seq
        h0 = (u % per_seq) * uh
        return (pltpu.make_async_copy(kt_hbm.at[b, pl.ds(h0, uh)], kbuf.at[slot], sem.at[0, slot]),
                pltpu.make_async_copy(vt_hbm.at[b, pl.ds(h0, uh)], vbuf.at[slot], sem.at[1, slot]))

    @pl.when(step == 0)
    def _():
        for slot in range(WIN_UNITS_PER_STEP):
            for c in copies(slot, slot):
                c.start()

    cnt = cnt_ref[...][None]
    for slot in range(WIN_UNITS_PER_STEP):
        u = step * WIN_UNITS_PER_STEP + slot

        @pl.when(u < n_units)
        def _(u=u, slot=slot):
            for c in copies(u, slot):
                c.wait()
            b = u // per_seq
            h0 = (u % per_seq) * uh
            rows = pl.ds(h0, uh)
            o = _win_unit_attention(n_branch, q_ref[b, rows, :], kn_ref[b, rows, :], vn_ref[b, rows, :], cnt,
                                    kbuf[slot], vbuf[slot])
            for h in range(uh):
                oa_ref[b, pl.ds(h0 + h, 1), :] = o[h, 0:1, :]

            @pl.when(u + WIN_UNITS_PER_STEP < n_units)
            def _():
                for c in copies(u + WIN_UNITS_PER_STEP, slot):
                    c.start()

    _ffn_step(x_ref, g2_ref, wg_ref, wu_ref, wd_ref, gf_ref, o_ref, hn_ref)


def ffn_final_with_window_attention(x, g2, wg, wu, wd, gf, q, kn, vn, cache_k, cache_v, *, tm, tf):
    m, d = x.shape
    dff = wg.shape[1]
    nb, w_buf = cache_k.shape[0], cache_k.shape[1]
    grid = (m // tm, dff // tf)
    assert nb * (ATTN_HEADS // WIN_UNIT_HEADS) <= grid[0] * grid[1] * WIN_UNITS_PER_STEP
    pos = jnp.arange(w_buf)
    cnt = jnp.zeros((w_buf,), F32)
    for w, dil in DILATED_BRANCHES:
        cnt = cnt + ((pos >= w_buf - w) & ((w_buf - pos) % dil == 0)).astype(F32)
    const = lambda shape: pl.BlockSpec(shape, lambda i, f: (0,) * len(shape))
    small = const((nb, ATTN_HEADS, HEAD_DIM))
    unit = (WIN_UNITS_PER_STEP, WIN_UNIT_HEADS, HEAD_DIM, w_buf)
    y, att = pl.pallas_call(
        functools.partial(_ffn_stream_kernel, float(len(DILATED_BRANCHES))), grid=grid,
        in_specs=[
            pl.BlockSpec((tm, d), lambda i, f: (i, 0)),
            const((1, d)),
            pl.BlockSpec((d, tf), lambda i, f: (0, f)),
            pl.BlockSpec((d, tf), lambda i, f: (0, f)),
            pl.BlockSpec((tf, d), lambda i, f: (f, 0)),
            const((1, d)),
            small, small, small, const((1, w_buf)),
            pl.BlockSpec(memory_space=pl.ANY), pl.BlockSpec(memory_space=pl.ANY),
        ],
        out_specs=[pl.BlockSpec((tm, d), lambda i, f: (i, 0)), small],
        out_shape=[jax.ShapeDtypeStruct((m, d), F32), jax.ShapeDtypeStruct((nb, ATTN_HEADS, HEAD_DIM), F32)],
        scratch_shapes=[pltpu.VMEM((tm, d), BF16), pltpu.VMEM(unit, F32), pltpu.VMEM(unit, F32),
                        pltpu.SemaphoreType.DMA((2, WIN_UNITS_PER_STEP))],
        compiler_params=_params(("arbitrary", "arbitrary")), name="ffn_final_window_stream",
    )(x, g2.reshape(1, d), wg, wu, wd, gf.reshape(1, d), q, kn, vn, cnt.reshape(1, w_buf),
      jnp.transpose(cache_k, (0, 2, 3, 1)), jnp.transpose(cache_v, (0, 2, 3, 1)))
    return y, att.reshape(nb, D_ATTN)


MEM_SAMPLE_BB = 4


def _mem_sample_kernel(q_ref, k_ref, v_ref, o_ref):
    shape = (q_ref.shape[1], k_ref.shape[1])
    row = lax.broadcasted_iota(jnp.int32, shape, 0)
    col = lax.broadcasted_iota(jnp.int32, shape, 1)
    own = (col & (MEM_HEADS - 1)) == (row & (MEM_HEADS - 1))
    for i in range(q_ref.shape[0]):
        q = (q_ref[i] * (MEM_HEAD_DIM ** -0.5)).astype(BF16)
        s = lax.dot_general(q, k_ref[i].astype(BF16), (((1,), (1,)), ((), ())), preferred_element_type=F32)
        s = jnp.where(own, s, NEG)
        m = jnp.max(s, axis=-1, keepdims=True)
        p = jnp.where(own, jnp.exp(s - m), 0.0)
        l = jnp.sum(p, axis=-1, keepdims=True)
        o_ref[i] = _dot((p / l).astype(BF16), v_ref[i].astype(BF16))


def mem_attention_sample(q, cache_mk, cache_mv):
    nb = q.shape[0]
    assert MEM_HEADS & (MEM_HEADS - 1) == 0 and SUBLANES % MEM_HEADS == 0
    q8 = jnp.tile(q.reshape(nb, MEM_HEADS, MEM_HEAD_DIM), (1, SUBLANES // MEM_HEADS, 1))
    rows = MEM_LEN * MEM_HEADS
    bb = MEM_SAMPLE_BB
    small = pl.BlockSpec((bb, SUBLANES, MEM_HEAD_DIM), lambda b: (b, 0, 0))
    big = pl.BlockSpec((bb, rows, MEM_HEAD_DIM), lambda b: (b, 0, 0))
    out = pl.pallas_call(
        _mem_sample_kernel, grid=(nb // bb,), in_specs=[small, big, big], out_specs=small,
        out_shape=jax.ShapeDtypeStruct((nb, SUBLANES, MEM_HEAD_DIM), F32),
        compiler_params=_params(("parallel",)), name="mem_attention_sample",
    )(q8, cache_mk.reshape(nb, rows, MEM_HEAD_DIM), cache_mv.reshape(nb, rows, MEM_HEAD_DIM))
    return out[:, :MEM_HEADS].reshape(nb, D_MEM)


def _layer_tail(x, y_ssm, y_att, mem_attend, lw, gf, *, tm, tf):
    half = D_SSM
    x1 = matmul_residual([(y_ssm, lw["w_out"], 0), (y_att, lw["w_out"], 1)], x, tm=tm, tn=1024)
    assert y_ssm.shape[1] == half and y_att.shape[1] == half
    hq = rms_matmul(x1, lw["ln_mem_g"], lw["w_mq"], tm=tm, tn=D_MEM)
    x2 = matmul_residual([(mem_attend(hq), lw["w_mo"], 0)], x1, tm=tm, tn=1024)
    return ffn_final(x2, lw["ln2_g"], lw["w_gate"], lw["w_up"], lw["w_down"], gf, tm=min(tm, 512), tf=tf)


def kernel(x_prompt, x_sample, cache_win_k, cache_win_v, state_conv, state_ssm, cache_mem_k, cache_mem_v, mem_prompt, ln1_g, w_in, conv_w, conv_b, dt_bias, a_log, d_skip, ssm_norm_g, w_out, ln_mem_g, mem_norm_g, w_mq, w_mk, w_mv, w_mo, ln2_g, w_gate, w_up, w_down, ln_f_g):
    bp, s, d = x_prompt.shape
    nb, ts, _ = x_sample.shape
    depth = w_in.shape[0]
    assert bp == 1 and ts == 1 and depth == 1
    w_keep = min(DILATED_BRANCHES[-1][0], s)
    l = 0
    dt_lo = D_SSM + XBC_DIM
    dt_hi = dt_lo + SSM_HEADS
    w_in_t = jnp.transpose(w_in[l])
    lw = dict(ln_mem_g=ln_mem_g[l], w_mq=w_mq[l].astype(BF16), w_mo=w_mo[l].astype(BF16), ln2_g=ln2_g[l])
    w_mkv = jnp.concatenate([w_mk[l], w_mv[l]], axis=1).astype(BF16)
    ssm_w = (conv_w[l], conv_b[l], dt_bias[l], a_log[l], d_skip[l], ssm_norm_g[l])

    xs = x_sample.reshape(nb, d)
    proj_s, dt_s = rms_matmul_dt(xs, ln1_g[l], w_in_t, dt_lo, dt_hi, tm=nb, tn=1024)
    heads = lambda t: t.reshape(t.shape[0], ATTN_HEADS, HEAD_DIM)
    q_s, k_s, v_s = (heads(proj_s[:, c:c + D_ATTN]) for c in (COL_Q, COL_K, COL_V))

    xp = x_prompt.reshape(s, d)
    mkv = rms_matmul(mem_prompt.reshape(MEM_LEN, d), mem_norm_g[l], w_mkv, tm=MEM_LEN, tn=2 * D_MEM)
    proj, dt_raw = rms_matmul_dt(xp, ln1_g[l], w_in_t, dt_lo, dt_hi, tm=1024, tn=1024)
    y_ssm, h_p, conv_p, lw["w_out"], lw["w_gate"], lw["w_up"], lw["w_down"] = ssd_prompt(
        proj, dt_raw, *ssm_w, cast_weights=(w_out[l], w_gate[l], w_up[l], w_down[l]))
    y_att = dilated_prompt(proj)
    x2 = mix_mem_fused(xp, y_ssm, y_att, lw["w_out"], lw["ln_mem_g"], lw["w_mq"], mkv, lw["w_mo"], tm=512)
    y_prompt, y_att_s = ffn_final_with_window_attention(
        x2, lw["ln2_g"], lw["w_gate"], lw["w_up"], lw["w_down"], ln_f_g,
        q_s, k_s, v_s, cache_win_k[l], cache_win_v[l], tm=512, tf=512)

    y_ssm_s, h_s, conv_s = ssd_sample(proj_s, dt_s, state_conv[l], state_ssm[l], *ssm_w)
    y_sample = _layer_tail(xs, y_ssm_s, y_att_s,
                           lambda hq: mem_attention_sample(hq, cache_mem_k[l], cache_mem_v[l]), lw, ln_f_g,
                           tm=nb, tf=512)

    return (
        y_prompt.reshape(bp, s, d),
        y_sample.reshape(nb, ts, d),
        heads(proj[s - w_keep:, COL_K:COL_V])[None, None],
        heads(proj[s - w_keep:, COL_V:])[None, None],
        conv_p[None, None],
        h_p[None, None],
        mkv[:, :D_MEM].reshape(1, 1, MEM_LEN, MEM_HEADS, MEM_HEAD_DIM),
        mkv[:, D_MEM:].reshape(1, 1, MEM_LEN, MEM_HEADS, MEM_HEAD_DIM),
        k_s[None, :, None],
        v_s[None, :, None],
        conv_s[None],
        h_s[None],
    )
```

```python
import functools

import jax
import jax.numpy as jnp
from jax import lax
from jax.experimental import pallas as pl
from jax.experimental.pallas import tpu as pltpu

F32 = jnp.float32
BF16 = jnp.bfloat16

D_MODEL = 2048
D_SSM = 1024
D_ATTN = 1024
HEAD_DIM = 64
SSM_HEADS = 16
SSM_GROUPS = 4
SSM_STATE = 128
CONV_W = 4
SSD_CHUNK = 128
XBC_DIM = D_SSM + 2 * SSM_GROUPS * SSM_STATE
ATTN_HEADS = 16
DILATED_BRANCHES = ((128, 1), (512, 4), (2048, 16))
MEM_LEN = 256
MEM_HEADS = 4
MEM_HEAD_DIM = 128
D_MEM = MEM_HEADS * MEM_HEAD_DIM
RMS_EPS = 1e-5
NEG = -1e30

LANES = 128
SUBLANES = 8
VMEM_BYTES_V7X = 64 * 1024 * 1024
VMEM_LIMIT = 56 * 1024 * 1024

COL_Z = 0
COL_XBC = D_SSM
COL_Q = D_SSM + XBC_DIM
COL_K = COL_Q + D_ATTN
COL_V = COL_K + D_ATTN
D_PROJ = COL_V + D_ATTN


def _params(sem):
    return pltpu.CompilerParams(dimension_semantics=sem, vmem_limit_bytes=VMEM_LIMIT)


def _rms(x, g):
    ms = jnp.mean(x * x, axis=-1, keepdims=True)
    return x * lax.rsqrt(ms + RMS_EPS) * g


def _silu(x):
    return x * (0.5 * jnp.tanh(0.5 * x) + 0.5)


def _softplus(x):
    return jnp.maximum(x, 0.0) + jnp.log1p(jnp.exp(-jnp.abs(x)))


def _split3(x):
    hi = x.astype(BF16)
    r1 = x - hi.astype(F32)
    mid = r1.astype(BF16)
    lo = (r1 - mid.astype(F32)).astype(BF16)
    return hi, mid, lo


def _dot(a, b):
    return jnp.dot(a, b, preferred_element_type=F32)


def _rms_matmul_kernel(x_ref, g_ref, w_ref, o_ref, hn_ref):
    @pl.when(pl.program_id(1) == 0)
    def _():
        hn_ref[...] = _rms(x_ref[...], g_ref[...]).astype(BF16)

    o_ref[...] = _dot(hn_ref[...], w_ref[...])


def _dot_nt(a, b):
    return lax.dot_general(a, b, (((1,), (1,)), ((), ())), preferred_element_type=F32)


def _rms_matmul_dt_kernel(n_dt, x_ref, g_ref, wt_ref, wdt_ref, o_ref, dt_ref, hn_ref):
    @pl.when(pl.program_id(1) == 0)
    def _():
        hn = _rms(x_ref[...], g_ref[...])
        hi = hn.astype(BF16)
        lo = (hn - hi.astype(F32)).astype(BF16)
        hn_ref[...] = hi
        wd = wdt_ref[...]
        wd_hi = wd.astype(BF16)
        wd2 = jnp.concatenate([wd_hi, (wd - wd_hi.astype(F32)).astype(BF16)], axis=0)
        both = _dot_nt(hi, wd2) + _dot_nt(lo, wd2)
        dt = both[:, :LANES] + both[:, LANES:]
        dt_ref[...] = jnp.where(lax.broadcasted_iota(jnp.int32, dt.shape, 1) < n_dt, dt, 0.0)

    o_ref[...] = _dot_nt(hn_ref[...], wt_ref[...].astype(BF16))


def rms_matmul(x, g, w, *, tm, tn):
    m, k = x.shape
    n = w.shape[1]
    return pl.pallas_call(
        _rms_matmul_kernel, grid=(m // tm, n // tn),
        in_specs=[
            pl.BlockSpec((tm, k), lambda i, j: (i, 0)),
            pl.BlockSpec((1, k), lambda i, j: (0, 0)),
            pl.BlockSpec((k, tn), lambda i, j: (0, j)),
        ],
        out_specs=pl.BlockSpec((tm, tn), lambda i, j: (i, j)),
        out_shape=jax.ShapeDtypeStruct((m, n), F32), scratch_shapes=[pltpu.VMEM((tm, k), BF16)],
        compiler_params=_params(("parallel", "arbitrary")), name="rms_matmul",
    )(x, g.reshape(1, k), w)


def rms_matmul_dt(x, g, w_t, dt_lo, dt_hi, *, tm, tn):
    m, k = x.shape
    n_dt = dt_hi - dt_lo
    n = w_t.shape[0] - n_dt
    n_a = dt_lo // tn
    assert dt_lo % tn == 0 and n % tn == 0 and n_dt % SUBLANES == 0 and n_dt <= LANES
    assert dt_lo + LANES <= w_t.shape[0]
    return pl.pallas_call(
        functools.partial(_rms_matmul_dt_kernel, n_dt), grid=(m // tm, n // tn),
        in_specs=[
            pl.BlockSpec((tm, k), lambda i, j: (i, 0)),
            pl.BlockSpec((1, k), lambda i, j: (0, 0)),
            pl.BlockSpec((pl.Element(tn), pl.Element(k)),
                         lambda i, j: (pl.multiple_of(j * tn + jnp.where(j >= n_a, n_dt, 0), SUBLANES), 0)),
            pl.BlockSpec((pl.Element(LANES), pl.Element(k)), lambda i, j: (dt_lo, 0)),
        ],
        out_specs=[pl.BlockSpec((tm, tn), lambda i, j: (i, j)), pl.BlockSpec((tm, LANES), lambda i, j: (i, 0))],
        out_shape=[jax.ShapeDtypeStruct((m, n), F32), jax.ShapeDtypeStruct((m, LANES), F32)],
        scratch_shapes=[pltpu.VMEM((tm, k), BF16)],
        compiler_params=_params(("parallel", "arbitrary")), name="rms_matmul_dt",
    )(x, g.reshape(1, k), w_t, w_t)


def _mm_res_kernel(n_pairs, *refs):
    res_ref = refs[2 * n_pairs]
    o_ref = refs[2 * n_pairs + 1]
    acc = res_ref[...]
    for i in range(n_pairs):
        acc = acc + _dot(refs[2 * i][...].astype(BF16), refs[2 * i + 1][...])
    o_ref[...] = acc


def matmul_residual(pairs, res, *, tm, tn):
    m, n = res.shape
    grid = (m // tm, n // tn)
    in_specs, args = [], []
    for a, w, kblk in pairs:
        k = a.shape[1]
        in_specs.append(pl.BlockSpec((tm, k), lambda i, j: (i, 0)))
        in_specs.append(pl.BlockSpec((k, tn), lambda i, j, kblk=kblk: (kblk, j)))
        args += [a, w]
    in_specs.append(pl.BlockSpec((tm, tn), lambda i, j: (i, j)))
    args.append(res)
    return pl.pallas_call(
        functools.partial(_mm_res_kernel, len(pairs)), grid=grid, in_specs=in_specs,
        out_specs=pl.BlockSpec((tm, tn), lambda i, j: (i, j)),
        out_shape=jax.ShapeDtypeStruct((m, n), F32),
        compiler_params=_params(("parallel", "parallel")), name="matmul_residual",
    )(*args)


def _ffn_step(x_ref, g2_ref, wg_ref, wu_ref, wd_ref, gf_ref, o_ref, hn_ref):
    f = pl.program_id(1)

    @pl.when(f == 0)
    def _():
        hn_ref[...] = _rms(x_ref[...], g2_ref[...]).astype(BF16)
        o_ref[...] = jnp.zeros_like(o_ref)

    hn = hn_ref[...]
    act = _silu(_dot(hn, wg_ref[...])) * _dot(hn, wu_ref[...])
    o_ref[...] += _dot(act.astype(BF16), wd_ref[...])

    @pl.when(f == pl.num_programs(1) - 1)
    def _():
        o_ref[...] = _rms(x_ref[...] + o_ref[...], gf_ref[...])


def _ffn_kernel(x_ref, g2_ref, wg_ref, wu_ref, wd_ref, gf_ref, o_ref, hn_ref):
    _ffn_step(x_ref, g2_ref, wg_ref, wu_ref, wd_ref, gf_ref, o_ref, hn_ref)


def ffn_final(x, g2, wg, wu, wd, gf, *, tm, tf):
    m, d = x.shape
    dff = wg.shape[1]
    grid = (m // tm, dff // tf)
    return pl.pallas_call(
        _ffn_kernel, grid=grid,
        in_specs=[
            pl.BlockSpec((tm, d), lambda i, f: (i, 0)),
            pl.BlockSpec((1, d), lambda i, f: (0, 0)),
            pl.BlockSpec((d, tf), lambda i, f: (0, f)),
            pl.BlockSpec((d, tf), lambda i, f: (0, f)),
            pl.BlockSpec((tf, d), lambda i, f: (f, 0)),
            pl.BlockSpec((1, d), lambda i, f: (0, 0)),
        ],
        out_specs=pl.BlockSpec((tm, d), lambda i, f: (i, 0)),
        out_shape=jax.ShapeDtypeStruct((m, d), F32),
        scratch_shapes=[pltpu.VMEM((tm, d), BF16)],
        compiler_params=_params(("parallel", "arbitrary")), name="ffn_final",
    )(x, g2.reshape(1, d), wg, wu, wd, gf.reshape(1, d))


def _mix_mem_kernel(x_ref, ys_ref, ya_ref, wo_ref, g_ref, wq_ref, mkv_ref, wm_ref, o_ref):
    half = ys_ref.shape[1]
    x1 = x_ref[...] + _dot(ys_ref[...], wo_ref[0:half, :]) + _dot(ya_ref[...], wo_ref[half:, :])
    hq = _dot(_rms(x1, g_ref[...]).astype(BF16), wq_ref[...])
    scale = MEM_HEAD_DIM ** -0.5
    outs = []
    for h in range(MEM_HEADS):
        sl = slice(h * MEM_HEAD_DIM, (h + 1) * MEM_HEAD_DIM)
        k = mkv_ref[:, sl].astype(BF16)
        v = mkv_ref[:, D_MEM + h * MEM_HEAD_DIM:D_MEM + (h + 1) * MEM_HEAD_DIM].astype(BF16)
        s = lax.dot_general(hq[:, sl].astype(BF16), k, (((1,), (1,)), ((), ())), preferred_element_type=F32) * scale
        p = jnp.exp(s - jnp.max(s, axis=-1, keepdims=True))
        p = p / jnp.sum(p, axis=-1, keepdims=True)
        outs.append(_dot(p.astype(BF16), v).astype(BF16))
    o_ref[...] = x1 + _dot(jnp.concatenate(outs, axis=1), wm_ref[...])


def mix_mem_fused(x, y_ssm, y_att, w_out, ln_mem_g, w_mq, mkv, w_mo, *, tm):
    m, d = x.shape
    half = y_ssm.shape[1]
    row = lambda width: pl.BlockSpec((tm, width), lambda i: (i, 0))
    whole = lambda a: pl.BlockSpec(a.shape, lambda i: (0, 0), pipeline_mode=pl.Buffered(1))
    g = ln_mem_g.reshape(1, d)
    return pl.pallas_call(
        _mix_mem_kernel, grid=(m // tm,),
        in_specs=[row(d), row(half), row(half), whole(w_out), whole(g), whole(w_mq), whole(mkv), whole(w_mo)],
        out_specs=row(d),
        out_shape=jax.ShapeDtypeStruct((m, d), F32),
        compiler_params=_params(("parallel",)), name="mix_mem_fused",
    )(x, y_ssm, y_att, w_out, g, w_mq, mkv, w_mo)


def _head_cols(x, g, width):
    rows = x.shape[0]
    parts = [jnp.broadcast_to(x[:, 4 * g + i:4 * g + i + 1], (rows, width)) for i in range(4)]
    return jnp.concatenate(parts, axis=1)


def _ssd_prompt_kernel(n_cast, n_chunks, z_ref, xs_ref, bc_ref, dt_ref, cw_ref, cb_ref, dtb_ref, alog_ref,
                       dskip_ref, ng_ref, *rest):
    cast_in = rest[:n_cast]
    y_ref, hfin_ref, convo_ref = rest[n_cast:n_cast + 3]
    cast_out = rest[n_cast + 3:2 * n_cast + 3]
    cbuf, st_ref = rest[2 * n_cast + 3:]
    c = pl.program_id(0)
    L = SSD_CHUNK
    P = HEAD_DIM
    GW = 4 * P

    @pl.when(c == 0)
    def _():
        cbuf[0:SUBLANES, :] = jnp.zeros((SUBLANES, XBC_DIM), F32)
        st_ref[...] = jnp.zeros_like(st_ref)

    ri = lax.broadcasted_iota(jnp.int32, (L, L), 0)
    ci = lax.broadcasted_iota(jnp.int32, (L, L), 1)
    causal = ri >= ci
    lane = lax.broadcasted_iota(jnp.int32, (L, GW), 1)
    head_lanes = [(lane >= i * P) & (lane < (i + 1) * P) for i in range(4)]

    for k in range(n_chunks):
        rs = slice(k * L, (k + 1) * L)
        cbuf[SUBLANES:SUBLANES + L, 0:D_SSM] = xs_ref[rs, :]
        cbuf[SUBLANES:SUBLANES + L, D_SSM:XBC_DIM] = bc_ref[rs, :]
        up = cbuf[...]
        taps = [(up if j == CONV_W - 1 else pltpu.roll(up, CONV_W - 1 - j, 0))[SUBLANES:SUBLANES + L]
                * cw_ref[j:j + 1, :] for j in range(CONV_W)]
        conv = cb_ref[...] + (((taps[0] + taps[1]) + taps[2]) + taps[3])

        if k == n_chunks - 1:
            @pl.when(c == pl.num_programs(0) - 1)
            def _():
                convo_ref[...] = cbuf[pl.ds(L + SUBLANES - (CONV_W - 1), CONV_W - 1), :]

        cbuf[0:SUBLANES, :] = cbuf[L:L + SUBLANES, :]

        act = _silu(conv)
        xs = act[:, 0:D_SSM]
        dt = _softplus(dt_ref[rs, :] + dtb_ref[...])
        a = dt * (-jnp.exp(alog_ref[...]))
        acum = jnp.dot(causal.astype(F32), a, precision=lax.Precision.HIGHEST, preferred_element_type=F32)
        acum_t = acum.T
        dt_t = dt.T
        last = acum[L - 1:L, :]
        ea = jnp.exp(acum)
        wend = jnp.exp(last - acum) * dt
        cdec = jnp.exp(last)

        if k == 0:
            for src, dst in zip(cast_in, cast_out):
                dst[...] = src[...].astype(dst.dtype)

        ys = []
        for g in range(SSM_GROUPS):
            bg = act[:, D_SSM + g * SSM_STATE:D_SSM + (g + 1) * SSM_STATE]
            cg = act[:, D_SSM + SSM_GROUPS * SSM_STATE + g * SSM_STATE:
                     D_SSM + SSM_GROUPS * SSM_STATE + (g + 1) * SSM_STATE]
            bgb = bg.astype(BF16)
            cgb = cg.astype(BF16)
            cb = lax.dot_general(cgb, bgb, (((1,), (1,)), ((), ())), preferred_element_type=F32)
            xg = xs[:, g * GW:(g + 1) * GW]
            yd = jnp.zeros((L, GW), F32)
            for i in range(4):
                h = 4 * g + i
                seg = acum[:, h:h + 1] - acum_t[h:h + 1, :]
                decay = jnp.exp(jnp.where(causal, seg, NEG))
                w = (cb * decay * dt_t[h:h + 1, :]).astype(BF16)
                xm = jnp.where(head_lanes[i], xg, 0.0).astype(BF16)
                yd = yd + _dot(w, xm)
            st = st_ref[g]
            yoff = _dot(cgb, st.astype(BF16)) * _head_cols(ea, g, P)
            ys.append(yd + yoff)
            xw = (xg * _head_cols(wend, g, P)).astype(BF16)
            new = _dot(bg.T.astype(BF16), xw)
            cd = jnp.concatenate(
                [jnp.broadcast_to(cdec[:, 4 * g + i:4 * g + i + 1], (1, P)) for i in range(4)], axis=1)
            st_ref[g] = st * cd + new

        y = jnp.concatenate(ys, axis=1) + dskip_ref[...] * xs
        y = y * _silu(z_ref[rs, :])
        outs = []
        for g in range(SSM_GROUPS):
            yg = y[:, g * GW:(g + 1) * GW]
            outs.append(yg * lax.rsqrt(jnp.mean(yg * yg, axis=-1, keepdims=True) + RMS_EPS))
        y_ref[rs, :] = (jnp.concatenate(outs, axis=1) * ng_ref[...]).astype(y_ref.dtype)

    @pl.when(c == pl.num_programs(0) - 1)
    def _():
        for g in range(SSM_GROUPS):
            stg = st_ref[g]
            for i in range(4):
                hfin_ref[4 * g + i] = stg[:, i * P:(i + 1) * P].T


def _pad_lanes(v):
    return jnp.pad(v.astype(F32), (0, LANES - v.shape[0])).reshape(1, LANES)


BF16_ROWS = 16
SSD_CHUNKS_PER_STEP = 4


def _cast_rows_per_step(rows, steps):
    return min(r for r in range(BF16_ROWS, rows + 1, BF16_ROWS) if rows % r == 0 and r * steps >= rows)


def ssd_prompt(proj, dt_raw, conv_w, conv_b, dt_bias, a_log, d_skip, norm_g, cast_weights=()):
    s = proj.shape[0]
    n_chunks = SSD_CHUNKS_PER_STEP if s % (SSD_CHUNKS_PER_STEP * SSD_CHUNK) == 0 else 1
    L = n_chunks * SSD_CHUNK
    steps = s // L
    col = lambda off: off // 1024
    const = lambda shape: pl.BlockSpec(shape, lambda c: (0,) * len(shape))
    cast_specs = []
    for w in cast_weights:
        rows = _cast_rows_per_step(w.shape[0], steps)
        last = w.shape[0] // rows - 1
        cast_specs.append(pl.BlockSpec((rows, w.shape[1]), lambda c, last=last: (jnp.minimum(c, last), 0)))
    return pl.pallas_call(
        functools.partial(_ssd_prompt_kernel, len(cast_weights), n_chunks), grid=(steps,),
        in_specs=[
            pl.BlockSpec((L, D_SSM), lambda c: (c, col(COL_Z))),
            pl.BlockSpec((L, D_SSM), lambda c: (c, col(COL_XBC))),
            pl.BlockSpec((L, XBC_DIM - D_SSM), lambda c: (c, col(COL_XBC + D_SSM))),
            pl.BlockSpec((L, LANES), lambda c: (c, 0)),
            const((CONV_W, XBC_DIM)), const((1, XBC_DIM)), const((1, LANES)), const((1, LANES)),
            const((1, D_SSM)), const((1, D_SSM)),
        ] + cast_specs,
        out_specs=[
            pl.BlockSpec((L, D_SSM), lambda c: (c, 0)),
            const((SSM_HEADS, HEAD_DIM, SSM_STATE)),
            const((CONV_W - 1, XBC_DIM)),
        ] + cast_specs,
        out_shape=[
            jax.ShapeDtypeStruct((s, D_SSM), BF16),
            jax.ShapeDtypeStruct((SSM_HEADS, HEAD_DIM, SSM_STATE), F32),
            jax.ShapeDtypeStruct((CONV_W - 1, XBC_DIM), F32),
        ] + [jax.ShapeDtypeStruct(w.shape, BF16) for w in cast_weights],
        scratch_shapes=[
            pltpu.VMEM((SSD_CHUNK + SUBLANES, XBC_DIM), F32),
            pltpu.VMEM((SSM_GROUPS, SSM_STATE, 4 * HEAD_DIM), F32),
        ],
        compiler_params=_params(("arbitrary",)), name="ssd_prompt",
    )(proj, proj, proj, dt_raw, conv_w, conv_b.reshape(1, XBC_DIM), _pad_lanes(dt_bias), _pad_lanes(a_log),
      jnp.repeat(d_skip, HEAD_DIM).reshape(1, D_SSM), norm_g.reshape(1, D_SSM), *cast_weights)


DIL_BLK = 128
DIL_GROUP = 8


def _dil_kernel(q_ref, k_ref, v_ref, o_ref, acc_ref, l_ref, ma_ref, mb_ref, bias_ref):
    s_len = q_ref.shape[0]
    B = DIL_BLK
    scale = HEAD_DIM ** -0.5
    lane = lax.broadcasted_iota(jnp.int32, (B, LANES), 1)
    head_a = lane < HEAD_DIM
    qi = lax.broadcasted_iota(jnp.int32, (2 * B, 2 * B), 0) & (B - 1)
    ki = lax.broadcasted_iota(jnp.int32, (2 * B, 2 * B), 1)
    band = (ki >= qi) & (ki <= qi + B)
    bias_ref[0] = jnp.where(band & (ki >= B), 0.0, NEG)
    bias_ref[1] = jnp.where(band, 0.0, NEG)
    ones = jnp.ones((2 * B, LANES), BF16)

    def comb(x):
        return jnp.where(head_a, x[:B], x[B:])

    def rows(ref, start, d):
        if d == 1:
            return ref[pl.ds(start, B), :]
        return ref[pl.ds(start, B, stride=d), :]

    def store_rows(ref, start, d, val):
        if d == 1:
            ref[pl.ds(start, B), :] = val
        else:
            ref[pl.ds(start, B, stride=d), :] = val

    order = sorted(DILATED_BRANCHES, key=lambda wd: -wd[1])
    assert order[-1][1] == 1
    for bi, (w, d) in enumerate(order):
        assert w // d == B
        first_branch = bi == 0
        last_branch = bi == len(order) - 1
        per_res = s_len // (B * d)
        G = min(DIL_GROUP, per_res)
        assert per_res % G == 0

        RG = max(1, DIL_GROUP // per_res)
        assert d % RG == 0

        def body(t, carry, d=d, first_branch=first_branch, last_branch=last_branch, per_res=per_res, G=G, RG=RG):
            r0 = (t // (per_res // G)) * RG
            n0 = (t % (per_res // G)) * G
            for rr in range(RG):
                r = r0 + rr
                starts = [(n0 + j) * (B * d) + r for j in range(G)]
                prev0 = jnp.maximum(n0 - 1, 0) * (B * d) + r
                if d == 1:
                    starts = [pl.multiple_of(st, B) for st in starts]
                    prev0 = pl.multiple_of(prev0, B)
                kb = [rows(k_ref, st, d).astype(BF16) for st in [prev0] + starts]
                vb = [rows(v_ref, st, d).astype(BF16) for st in [prev0] + starts]
                for j in range(G):
                    start = starts[j]
                    q = rows(q_ref, start, d) * scale
                    qs = jnp.concatenate([jnp.where(head_a, q, 0.0), jnp.where(head_a, 0.0, q)],
                                         axis=0).astype(BF16)
                    kk = jnp.concatenate([kb[j], kb[j + 1]], axis=0)
                    s = lax.dot_general(qs, kk, (((1,), (1,)), ((), ())), preferred_element_type=F32)
                    s = s + (bias_ref[jnp.minimum(n0, 1)] if j == 0 else bias_ref[1])
                    m_row = jnp.broadcast_to(jnp.max(s, axis=-1, keepdims=True), (2 * B, LANES))
                    if first_branch:
                        m_new = m_row
                    else:
                        m_old = jnp.concatenate([rows(ma_ref, start, d), rows(mb_ref, start, d)], axis=0)
                        m_new = jnp.maximum(m_old, m_row)
                    p = jnp.exp(s - jnp.concatenate([m_new, m_new], axis=1)).astype(BF16)
                    pv = _dot(p, jnp.concatenate([jnp.concatenate([vb[j], vb[j + 1]], axis=0), ones], axis=1))
                    l_new = comb(pv[:, LANES:])
                    pv = comb(pv[:, :LANES])
                    if first_branch:
                        acc, l_b = pv, l_new
                    else:
                        alpha = comb(jnp.exp(m_old - m_new))
                        acc = rows(acc_ref, start, d) * alpha + pv
                        l_b = rows(l_ref, start, d) * alpha + l_new
                    if last_branch:
                        o_ref[pl.ds(start, B), :] = (acc / l_b).astype(o_ref.dtype)
                    else:
                        store_rows(acc_ref, start, d, acc)
                        store_rows(l_ref, start, d, l_b)
                        store_rows(ma_ref, start, d, m_new[:B])
                        store_rows(mb_ref, start, d, m_new[B:])
            return carry

        lax.fori_loop(0, s_len // (B * G * RG), body, 0)


def dilated_prompt(proj):
    s = proj.shape[0]
    n_pairs = D_ATTN // LANES
    spec = lambda off: pl.BlockSpec((s, LANES), lambda h: (0, off // LANES + h))
    return pl.pallas_call(
        _dil_kernel, grid=(n_pairs,),
        in_specs=[spec(COL_Q), spec(COL_K), spec(COL_V)],
        out_specs=pl.BlockSpec((s, LANES), lambda h: (0, h)),
        out_shape=jax.ShapeDtypeStruct((s, D_ATTN), BF16),
        scratch_shapes=[pltpu.VMEM((s, LANES), F32)] * 4 + [pltpu.VMEM((2, 2 * DIL_BLK, 2 * DIL_BLK), F32)],
        compiler_params=_params(("parallel",)), name="dilated_prompt",
    )(proj, proj, proj)


SSD_SAMPLE_BB = 8


def _head_expand(x, seg_t):
    hi, mid, lo = _split3(x)
    return _dot(hi, seg_t) + _dot(mid, seg_t) + _dot(lo, seg_t)


def _ssd_sample_kernel(z_ref, xs_ref, bc_ref, dt_ref, sc_ref, st_ref, cw_ref, cb_ref, dtb_ref, alog_ref,
                       dskip_ref, ng_ref, segt_ref,
                       y_ref, ho_ref, convo_ref,
                       xs_s, t2_s, dec_s, b_s, c_s, col_s, yoff_s):
    step = pl.program_id(0)
    nb = xs_ref.shape[0]
    BB = SSD_SAMPLE_BB
    GW = 4 * HEAD_DIM
    HP = SSM_HEADS * HEAD_DIM

    @pl.when(step == 0)
    def _():
        xnew = jnp.concatenate([xs_ref[...], bc_ref[...]], axis=1)
        rows = [sc_ref[:, j * XBC_DIM:(j + 1) * XBC_DIM] for j in range(CONV_W - 1)] + [xnew]
        taps = [rows[j] * cw_ref[j:j + 1, :] for j in range(CONV_W)]
        act = _silu(cb_ref[...] + (((taps[0] + taps[1]) + taps[2]) + taps[3]))
        for j in range(CONV_W - 1):
            convo_ref[:, j * XBC_DIM:(j + 1) * XBC_DIM] = rows[j + 1]
        xs = act[:, 0:D_SSM]
        bm = act[:, D_SSM:D_SSM + SSM_GROUPS * SSM_STATE]
        cm = act[:, D_SSM + SSM_GROUPS * SSM_STATE:]
        dt = _softplus(dt_ref[...] + dtb_ref[...])
        dec = jnp.exp(dt * (-jnp.exp(alog_ref[...])))
        seg_t = segt_ref[...]
        u = _head_expand(dt, seg_t) * xs
        dec_e = _head_expand(dec, seg_t)
        xs_s[...] = xs
        dec_s[...] = dec_e
        b_s[...] = bm
        c_s[...] = cm
        for g in range(SSM_GROUPS):
            sl = slice(g * SSM_STATE, (g + 1) * SSM_STATE)
            bc = jnp.sum(bm[:, sl] * cm[:, sl], axis=-1, keepdims=True)
            t2_s[:, g * GW:(g + 1) * GW] = u[:, g * GW:(g + 1) * GW] * bc
        for j in range(HP // LANES):
            for src, base in ((u, 0), (dec_e, HP)):
                t = src[:, j * LANES:(j + 1) * LANES].T
                for k, part in enumerate(_split3(t)):
                    col_s[base + j * LANES:base + (j + 1) * LANES, k * nb:(k + 1) * nb] = part

    def per_pair(ip, carry):
        b0 = step * BB + 2 * ip
        ridx = lax.broadcasted_iota(jnp.int32, (3 * nb, 2 * SSM_STATE), 0)
        lane = lax.broadcasted_iota(jnp.int32, (3 * nb, 2 * SSM_STATE), 1)
        tgt = b0 + (lane >= SSM_STATE).astype(jnp.int32)
        sel = ((ridx == tgt) | (ridx == tgt + nb) | (ridx == tgt + 2 * nb)).astype(BF16)
        cols2 = _dot(col_s[...], sel)
        c8 = c_s[pl.ds(pl.multiple_of(step * BB, BB), BB), :]
        for j in range(2):
            i = 2 * ip + j
            b = b0 + j
            cols = cols2[:, j * SSM_STATE:(j + 1) * SSM_STATE]
            h0 = st_ref[i].reshape(HP, SSM_STATE)
            brow = b_s[pl.ds(b, 1), :]
            pick = lax.broadcasted_iota(jnp.int32, (BB, GW), 0) == i
            for g in range(SSM_GROUPS):
                rs = slice(g * GW, (g + 1) * GW)
                ls = slice(g * SSM_STATE, (g + 1) * SSM_STATE)
                h0g = h0[rs]
                hn = cols[HP + g * GW:HP + (g + 1) * GW] * h0g + cols[rs] * brow[:, ls]
                ho_ref[i, 4 * g:4 * g + 4] = hn.reshape(4, HEAD_DIM, SSM_STATE)
                yo = lax.dot_general(c8[:, ls].astype(BF16), h0g.astype(BF16), (((1,), (1,)), ((), ())),
                                     preferred_element_type=F32)
                yoff_s[pl.ds(b, 1), rs] = jnp.sum(jnp.where(pick, yo, 0.0), axis=0, keepdims=True)
        return carry

    lax.fori_loop(0, BB // 2, per_pair, 0, unroll=2)

    @pl.when(step == pl.num_programs(0) - 1)
    def _():
        xs = xs_s[...]
        y = dec_s[...] * yoff_s[...] + t2_s[...] + dskip_ref[...] * xs
        y = y * _silu(z_ref[...])
        for g in range(SSM_GROUPS):
            yg = y[:, g * GW:(g + 1) * GW]
            yg = yg * lax.rsqrt(jnp.mean(yg * yg, axis=-1, keepdims=True) + RMS_EPS)
            y_ref[:, g * GW:(g + 1) * GW] = (yg * ng_ref[:, g * GW:(g + 1) * GW]).astype(y_ref.dtype)


def _seg_ones(n_heads, width):
    c = jnp.arange(n_heads * width)[:, None] // width
    seg = (c == jnp.arange(LANES)[None, :]).astype(BF16)
    return seg, seg.T


def ssd_sample(proj, dt_raw, state_conv, state_ssm, conv_w, conv_b, dt_bias, a_log, d_skip, norm_g):
    nb = proj.shape[0]
    BB = SSD_SAMPLE_BB
    HP = SSM_HEADS * HEAD_DIM
    col = lambda off: off // 1024
    const = lambda shape: pl.BlockSpec(shape, lambda s: (0,) * len(shape))
    _, seg_t = _seg_ones(SSM_HEADS, HEAD_DIM)
    st_spec = pl.BlockSpec((BB, SSM_HEADS, HEAD_DIM, SSM_STATE), lambda s: (s, 0, 0, 0))
    y, h_new, conv_new = pl.pallas_call(
        _ssd_sample_kernel, grid=(nb // BB,),
        in_specs=[
            pl.BlockSpec((nb, D_SSM), lambda s: (0, col(COL_Z))),
            pl.BlockSpec((nb, D_SSM), lambda s: (0, col(COL_XBC))),
            pl.BlockSpec((nb, XBC_DIM - D_SSM), lambda s: (0, col(COL_XBC + D_SSM))),
            const((nb, LANES)), const((nb, (CONV_W - 1) * XBC_DIM)), st_spec,
            const((CONV_W, XBC_DIM)), const((1, XBC_DIM)), const((1, LANES)), const((1, LANES)),
            const((1, D_SSM)), const((1, D_SSM)), const((LANES, HP)),
        ],
        out_specs=[const((nb, D_SSM)), st_spec, const((nb, (CONV_W - 1) * XBC_DIM))],
        out_shape=[
            jax.ShapeDtypeStruct((nb, D_SSM), BF16),
            jax.ShapeDtypeStruct(state_ssm.shape, F32),
            jax.ShapeDtypeStruct((nb, (CONV_W - 1) * XBC_DIM), F32),
        ],
        scratch_shapes=[
            pltpu.VMEM((nb, HP), F32), pltpu.VMEM((nb, HP), F32), pltpu.VMEM((nb, HP), F32),
            pltpu.VMEM((nb, SSM_GROUPS * SSM_STATE), F32), pltpu.VMEM((nb, SSM_GROUPS * SSM_STATE), F32),
            pltpu.VMEM((2 * HP, 3 * nb), BF16), pltpu.VMEM((nb, HP), F32),
        ],
        compiler_params=_params(("arbitrary",)), name="ssd_sample",
    )(proj, proj, proj, dt_raw, state_conv.reshape(nb, (CONV_W - 1) * XBC_DIM), state_ssm, conv_w,
      conv_b.reshape(1, XBC_DIM), _pad_lanes(dt_bias), _pad_lanes(a_log),
      jnp.repeat(d_skip, HEAD_DIM).reshape(1, D_SSM), norm_g.reshape(1, D_SSM), seg_t)
    return y, h_new, conv_new.reshape(nb, CONV_W - 1, XBC_DIM)


WIN_UNIT_HEADS = 4
WIN_UNITS_PER_STEP = 3


def _win_unit_attention(n_branch, q4, kn4, vn4, cnt, kt, vt):
    rep = lambda t: jnp.stack([jnp.broadcast_to(t[h:h + 1, :], (SUBLANES, t.shape[1])) for h in range(t.shape[0])])
    q = rep(q4) * (HEAD_DIM ** -0.5)
    s = jnp.einsum("hqd,hdw->hqw", q.astype(BF16), kt.astype(BF16), preferred_element_type=F32)
    s = jnp.where(cnt > 0.0, s, NEG)
    s_new = jnp.sum(q * rep(kn4), axis=-1, keepdims=True)
    m = jnp.maximum(jnp.max(s, axis=-1, keepdims=True), s_new)
    p = cnt * jnp.exp(s - m)
    p_new = n_branch * jnp.exp(s_new - m)
    l = jnp.sum(p, axis=-1, keepdims=True) + p_new
    o = jnp.einsum("hqw,hdw->hqd", p.astype(BF16), vt.astype(BF16), preferred_element_type=F32)
    return (o + p_new * rep(vn4)) / l


def _ffn_stream_kernel(n_branch, x_ref, g2_ref, wg_ref, wu_ref, wd_ref, gf_ref,
                       q_ref, kn_ref, vn_ref, cnt_ref, kt_hbm, vt_hbm,
                       o_ref, oa_ref, hn_ref, kbuf, vbuf, sem):
    step = pl.program_id(0) * pl.num_programs(1) + pl.program_id(1)
    uh = WIN_UNIT_HEADS
    per_seq = kt_hbm.shape[1] // uh
    n_units = kt_hbm.shape[0] * per_seq

    def copies(u, slot):
        b = u // per_seq
        h0 = (u % per_seq) * uh
        return (pltpu.make_async_copy(kt_hbm.at[b, pl.ds(h0, uh)], kbuf.at[slot], sem.at[0, slot]),
                pltpu.make_async_copy(vt_hbm.at[b, pl.ds(h0, uh)], vbuf.at[slot], sem.at[1, slot]))

    @pl.when(step == 0)
    def _():
        for slot in range(WIN_UNITS_PER_STEP):
            for c in copies(slot, slot):
                c.start()

    cnt = cnt_ref[...][None]
    for slot in range(WIN_UNITS_PER_STEP):
        u = step * WIN_UNITS_PER_STEP + slot

        @pl.when(u < n_units)
        def _(u=u, slot=slot):
            for c in copies(u, slot):
                c.wait()
            b = u // per_seq
            h0 = (u % per_seq) * uh
            rows = pl.ds(h0, uh)
            o = _win_unit_attention(n_branch, q_ref[b, rows, :], kn_ref[b, rows, :], vn_ref[b, rows, :], cnt,
                                    kbuf[slot], vbuf[slot])
            for h in range(uh):
                oa_ref[b, pl.ds(h0 + h, 1), :] = o[h, 0:1, :]

            @pl.when(u + WIN_UNITS_PER_STEP < n_units)
            def _():
                for c in copies(u + WIN_UNITS_PER_STEP, slot):
                    c.start()

    _ffn_step(x_ref, g2_ref, wg_ref, wu_ref, wd_ref, gf_ref, o_ref, hn_ref)


def ffn_final_with_window_attention(x, g2, wg, wu, wd, gf, q, kn, vn, cache_k, cache_v, *, tm, tf):
    m, d = x.shape
    dff = wg.shape[1]
    nb, w_buf = cache_k.shape[0], cache_k.shape[1]
    grid = (m // tm, dff // tf)
    assert nb * (ATTN_HEADS // WIN_UNIT_HEADS) <= grid[0] * grid[1] * WIN_UNITS_PER_STEP
    pos = jnp.arange(w_buf)
    cnt = jnp.zeros((w_buf,), F32)
    for w, dil in DILATED_BRANCHES:
        cnt = cnt + ((pos >= w_buf - w) & ((w_buf - pos) % dil == 0)).astype(F32)
    const = lambda shape: pl.BlockSpec(shape, lambda i, f: (0,) * len(shape))
    small = const((nb, ATTN_HEADS, HEAD_DIM))
    unit = (WIN_UNITS_PER_STEP, WIN_UNIT_HEADS, HEAD_DIM, w_buf)
    y, att = pl.pallas_call(
        functools.partial(_ffn_stream_kernel, float(len(DILATED_BRANCHES))), grid=grid,
        in_specs=[
            pl.BlockSpec((tm, d), lambda i, f: (i, 0)),
            const((1, d)),
            pl.BlockSpec((d, tf), lambda i, f: (0, f)),
            pl.BlockSpec((d, tf), lambda i, f: (0, f)),
            pl.BlockSpec((tf, d), lambda i, f: (f, 0)),
            const((1, d)),
            small, small, small, const((1, w_buf)),
            pl.BlockSpec(memory_space=pl.ANY), pl.BlockSpec(memory_space=pl.ANY),
        ],
        out_specs=[pl.BlockSpec((tm, d), lambda i, f: (i, 0)), small],
        out_shape=[jax.ShapeDtypeStruct((m, d), F32), jax.ShapeDtypeStruct((nb, ATTN_HEADS, HEAD_DIM), F32)],
        scratch_shapes=[pltpu.VMEM((tm, d), BF16), pltpu.VMEM(unit, F32), pltpu.VMEM(unit, F32),
                        pltpu.SemaphoreType.DMA((2, WIN_UNITS_PER_STEP))],
        compiler_params=_params(("arbitrary", "arbitrary")), name="ffn_final_window_stream",
    )(x, g2.reshape(1, d), wg, wu, wd, gf.reshape(1, d), q, kn, vn, cnt.reshape(1, w_buf),
      jnp.transpose(cache_k, (0, 2, 3, 1)), jnp.transpose(cache_v, (0, 2, 3, 1)))
    return y, att.reshape(nb, D_ATTN)


MEM_SAMPLE_BB = 4


def _mem_sample_kernel(q_ref, k_ref, v_ref, o_ref):
    shape = (q_ref.shape[1], k_ref.shape[1])
    row = lax.broadcasted_iota(jnp.int32, shape, 0)
    col = lax.broadcasted_iota(jnp.int32, shape, 1)
    own = (col & (MEM_HEADS - 1)) == (row & (MEM_HEADS - 1))
    for i in range(q_ref.shape[0]):
        q = (q_ref[i] * (MEM_HEAD_DIM ** -0.5)).astype(BF16)
        s = lax.dot_general(q, k_ref[i].astype(BF16), (((1,), (1,)), ((), ())), preferred_element_type=F32)
        s = jnp.where(own, s, NEG)
        m = jnp.max(s, axis=-1, keepdims=True)
        p = jnp.where(own, jnp.exp(s - m), 0.0)
        l = jnp.sum(p, axis=-1, keepdims=True)
        o_ref[i] = _dot((p / l).astype(BF16), v_ref[i].astype(BF16))


def mem_attention_sample(q, cache_mk, cache_mv):
    nb = q.shape[0]
    assert MEM_HEADS & (MEM_HEADS - 1) == 0 and SUBLANES % MEM_HEADS == 0
    q8 = jnp.tile(q.reshape(nb, MEM_HEADS, MEM_HEAD_DIM), (1, SUBLANES // MEM_HEADS, 1))
    rows = MEM_LEN * MEM_HEADS
    bb = MEM_SAMPLE_BB
    small = pl.BlockSpec((bb, SUBLANES, MEM_HEAD_DIM), lambda b: (b, 0, 0))
    big = pl.BlockSpec((bb, rows, MEM_HEAD_DIM), lambda b: (b, 0, 0))
    out = pl.pallas_call(
        _mem_sample_kernel, grid=(nb // bb,), in_specs=[small, big, big], out_specs=small,
        out_shape=jax.ShapeDtypeStruct((nb, SUBLANES, MEM_HEAD_DIM), F32),
        compiler_params=_params(("parallel",)), name="mem_attention_sample",
    )(q8, cache_mk.reshape(nb, rows, MEM_HEAD_DIM), cache_mv.reshape(nb, rows, MEM_HEAD_DIM))
    return out[:, :MEM_HEADS].reshape(nb, D_MEM)


def _layer_tail(x, y_ssm, y_att, mem_attend, lw, gf, *, tm, tf):
    half = D_SSM
    x1 = matmul_residual([(y_ssm, lw["w_out"], 0), (y_att, lw["w_out"], 1)], x, tm=tm, tn=1024)
    assert y_ssm.shape[1] == half and y_att.shape[1] == half
    hq = rms_matmul(x1, lw["ln_mem_g"], lw["w_mq"], tm=tm, tn=D_MEM)
    x2 = matmul_residual([(mem_attend(hq), lw["w_mo"], 0)], x1, tm=tm, tn=1024)
    return ffn_final(x2, lw["ln2_g"], lw["w_gate"], lw["w_up"], lw["w_down"], gf, tm=min(tm, 512), tf=tf)


def kernel(x_prompt, x_sample, cache_win_k, cache_win_v, state_conv, state_ssm, cache_mem_k, cache_mem_v, mem_prompt, ln1_g, w_in, conv_w, conv_b, dt_bias, a_log, d_skip, ssm_norm_g, w_out, ln_mem_g, mem_norm_g, w_mq, w_mk, w_mv, w_mo, ln2_g, w_gate, w_up, w_down, ln_f_g):
    bp, s, d = x_prompt.shape
    nb, ts, _ = x_sample.shape
    depth = w_in.shape[0]
    assert bp == 1 and ts == 1 and depth == 1
    w_keep = min(DILATED_BRANCHES[-1][0], s)
    l = 0
    dt_lo = D_SSM + XBC_DIM
    dt_hi = dt_lo + SSM_HEADS
    w_in_t = jnp.transpose(w_in[l])
    lw = dict(ln_mem_g=ln_mem_g[l], w_mq=w_mq[l].astype(BF16), w_mo=w_mo[l].astype(BF16), ln2_g=ln2_g[l])
    w_mkv = jnp.concatenate([w_mk[l], w_mv[l]], axis=1).astype(BF16)
    ssm_w = (conv_w[l], conv_b[l], dt_bias[l], a_log[l], d_skip[l], ssm_norm_g[l])

    xs = x_sample.reshape(nb, d)
    proj_s, dt_s = rms_matmul_dt(xs, ln1_g[l], w_in_t, dt_lo, dt_hi, tm=nb, tn=1024)
    heads = lambda t: t.reshape(t.shape[0], ATTN_HEADS, HEAD_DIM)
    q_s, k_s, v_s = (heads(proj_s[:, c:c + D_ATTN]) for c in (COL_Q, COL_K, COL_V))

    xp = x_prompt.reshape(s, d)
    mkv = rms_matmul(mem_prompt.reshape(MEM_LEN, d), mem_norm_g[l], w_mkv, tm=MEM_LEN, tn=2 * D_MEM)
    proj, dt_raw = rms_matmul_dt(xp, ln1_g[l], w_in_t, dt_lo, dt_hi, tm=1024, tn=1024)
    y_ssm, h_p, conv_p, lw["w_out"], lw["w_gate"], lw["w_up"], lw["w_down"] = ssd_prompt(
        proj, dt_raw, *ssm_w, cast_weights=(w_out[l], w_gate[l], w_up[l], w_down[l]))
    y_att = dilated_prompt(proj)
    x2 = mix_mem_fused(xp, y_ssm, y_att, lw["w_out"], lw["ln_mem_g"], lw["w_mq"], mkv, lw["w_mo"], tm=512)
    y_prompt, y_att_s = ffn_final_with_window_attention(
        x2, lw["ln2_g"], lw["w_gate"], lw["w_up"], lw["w_down"], ln_f_g,
        q_s, k_s, v_s, cache_win_k[l], cache_win_v[l], tm=512, tf=512)

    y_ssm_s, h_s, conv_s = ssd_sample(proj_s, dt_s, state_conv[l], state_ssm[l], *ssm_w)
    y_sample = _layer_tail(xs, y_ssm_s, y_att_s,
                           lambda hq: mem_attention_sample(hq, cache_mem_k[l], cache_mem_v[l]), lw, ln_f_g,
                           tm=nb, tf=512)

    return (
        y_prompt.reshape(bp, s, d),
        y_sample.reshape(nb, ts, d),
        heads(proj[s - w_keep:, COL_K:COL_V])[None, None],
        heads(proj[s - w_keep:, COL_V:])[None, None],
        conv_p[None, None],
        h_p[None, None],
        mkv[:, :D_MEM].reshape(1, 1, MEM_LEN, MEM_HEADS, MEM_HEAD_DIM),
        mkv[:, D_MEM:].reshape(1, 1, MEM_LEN, MEM_HEADS, MEM_HEAD_DIM),
        k_s[None, :, None],
        v_s[None, :, None],
        conv_s[None],
        h_s[None],
    )
```

```python
import functools

import jax
import jax.numpy as jnp
from jax import lax
from jax.experimental import pallas as pl
from jax.experimental.pallas import tpu as pltpu

F32 = jnp.float32
BF16 = jnp.bfloat16

D_MODEL = 2048
D_SSM = 1024
D_ATTN = 1024
HEAD_DIM = 64
SSM_HEADS = 16
SSM_GROUPS = 4
SSM_STATE = 128
CONV_W = 4
SSD_CHUNK = 128
XBC_DIM = D_SSM + 2 * SSM_GROUPS * SSM_STATE
ATTN_HEADS = 16
DILATED_BRANCHES = ((128, 1), (512, 4), (2048, 16))
MEM_LEN = 256
MEM_HEADS = 4
MEM_HEAD_DIM = 128
D_MEM = MEM_HEADS * MEM_HEAD_DIM
RMS_EPS = 1e-5
NEG = -1e30

LANES = 128
SUBLANES = 8
VMEM_BYTES_V7X = 64 * 1024 * 1024
VMEM_LIMIT = 56 * 1024 * 1024

COL_Z = 0
COL_XBC = D_SSM
COL_Q = D_SSM + XBC_DIM
COL_K = COL_Q + D_ATTN
COL_V = COL_K + D_ATTN
D_PROJ = COL_V + D_ATTN


def _params(sem):
    return pltpu.CompilerParams(dimension_semantics=sem, vmem_limit_bytes=VMEM_LIMIT)


def _rms(x, g):
    ms = jnp.mean(x * x, axis=-1, keepdims=True)
    return x * lax.rsqrt(ms + RMS_EPS) * g


def _silu(x):
    return x * (0.5 * jnp.tanh(0.5 * x) + 0.5)


def _softplus(x):
    return jnp.maximum(x, 0.0) + jnp.log1p(jnp.exp(-jnp.abs(x)))


def _split3(x):
    hi = x.astype(BF16)
    r1 = x - hi.astype(F32)
    mid = r1.astype(BF16)
    lo = (r1 - mid.astype(F32)).astype(BF16)
    return hi, mid, lo


def _dot(a, b):
    return jnp.dot(a, b, preferred_element_type=F32)


def _rms_matmul_kernel(x_ref, g_ref, w_ref, o_ref, hn_ref):
    @pl.when(pl.program_id(1) == 0)
    def _():
        hn_ref[...] = _rms(x_ref[...], g_ref[...]).astype(BF16)

    o_ref[...] = _dot(hn_ref[...], w_ref[...])


def _dot_nt(a, b):
    return lax.dot_general(a, b, (((1,), (1,)), ((), ())), preferred_element_type=F32)


def _rms_matmul_dt_kernel(n_dt, x_ref, g_ref, wt_ref, wdt_ref, o_ref, dt_ref, hn_ref):
    @pl.when(pl.program_id(1) == 0)
    def _():
        hn = _rms(x_ref[...], g_ref[...])
        hi = hn.astype(BF16)
        lo = (hn - hi.astype(F32)).astype(BF16)
        hn_ref[...] = hi
        wd = wdt_ref[...]
        wd_hi = wd.astype(BF16)
        wd2 = jnp.concatenate([wd_hi, (wd - wd_hi.astype(F32)).astype(BF16)], axis=0)
        both = _dot_nt(hi, wd2) + _dot_nt(lo, wd2)
        dt = both[:, :LANES] + both[:, LANES:]
        dt_ref[...] = jnp.where(lax.broadcasted_iota(jnp.int32, dt.shape, 1) < n_dt, dt, 0.0)

    o_ref[...] = _dot_nt(hn_ref[...], wt_ref[...].astype(BF16))


def rms_matmul(x, g, w, *, tm, tn):
    m, k = x.shape
    n = w.shape[1]
    return pl.pallas_call(
        _rms_matmul_kernel, grid=(m // tm, n // tn),
        in_specs=[
            pl.BlockSpec((tm, k), lambda i, j: (i, 0)),
            pl.BlockSpec((1, k), lambda i, j: (0, 0)),
            pl.BlockSpec((k, tn), lambda i, j: (0, j)),
        ],
        out_specs=pl.BlockSpec((tm, tn), lambda i, j: (i, j)),
        out_shape=jax.ShapeDtypeStruct((m, n), F32), scratch_shapes=[pltpu.VMEM((tm, k), BF16)],
        compiler_params=_params(("parallel", "arbitrary")), name="rms_matmul",
    )(x, g.reshape(1, k), w)


def rms_matmul_dt(x, g, w_t, dt_lo, dt_hi, *, tm, tn):
    m, k = x.shape
    n_dt = dt_hi - dt_lo
    n = w_t.shape[0] - n_dt
    n_a = dt_lo // tn
    assert dt_lo % tn == 0 and n % tn == 0 and n_dt % SUBLANES == 0 and n_dt <= LANES
    assert dt_lo + LANES <= w_t.shape[0]
    return pl.pallas_call(
        functools.partial(_rms_matmul_dt_kernel, n_dt), grid=(m // tm, n // tn),
        in_specs=[
            pl.BlockSpec((tm, k), lambda i, j: (i, 0)),
            pl.BlockSpec((1, k), lambda i, j: (0, 0)),
            pl.BlockSpec((pl.Element(tn), pl.Element(k)),
                         lambda i, j: (pl.multiple_of(j * tn + jnp.where(j >= n_a, n_dt, 0), SUBLANES), 0)),
            pl.BlockSpec((pl.Element(LANES), pl.Element(k)), lambda i, j: (dt_lo, 0)),
        ],
        out_specs=[pl.BlockSpec((tm, tn), lambda i, j: (i, j)), pl.BlockSpec((tm, LANES), lambda i, j: (i, 0))],
        out_shape=[jax.ShapeDtypeStruct((m, n), F32), jax.ShapeDtypeStruct((m, LANES), F32)],
        scratch_shapes=[pltpu.VMEM((tm, k), BF16)],
        compiler_params=_params(("parallel", "arbitrary")), name="rms_matmul_dt",
    )(x, g.reshape(1, k), w_t, w_t)


def _mm_res_kernel(n_pairs, *refs):
    res_ref = refs[2 * n_pairs]
    o_ref = refs[2 * n_pairs + 1]
    acc = res_ref[...]
    for i in range(n_pairs):
        acc = acc + _dot(refs[2 * i][...].astype(BF16), refs[2 * i + 1][...])
    o_ref[...] = acc


def matmul_residual(pairs, res, *, tm, tn):
    m, n = res.shape
    grid = (m // tm, n // tn)
    in_specs, args = [], []
    for a, w, kblk in pairs:
        k = a.shape[1]
        in_specs.append(pl.BlockSpec((tm, k), lambda i, j: (i, 0)))
        in_specs.append(pl.BlockSpec((k, tn), lambda i, j, kblk=kblk: (kblk, j)))
        args += [a, w]
    in_specs.append(pl.BlockSpec((tm, tn), lambda i, j: (i, j)))
    args.append(res)
    return pl.pallas_call(
        functools.partial(_mm_res_kernel, len(pairs)), grid=grid, in_specs=in_specs,
        out_specs=pl.BlockSpec((tm, tn), lambda i, j: (i, j)),
        out_shape=jax.ShapeDtypeStruct((m, n), F32),
        compiler_params=_params(("parallel", "parallel")), name="matmul_residual",
    )(*args)


def _ffn_step(x_ref, g2_ref, wg_ref, wu_ref, wd_ref, gf_ref, o_ref, hn_ref):
    f = pl.program_id(1)

    @pl.when(f == 0)
    def _():
        hn_ref[...] = _rms(x_ref[...], g2_ref[...]).astype(BF16)
        o_ref[...] = jnp.zeros_like(o_ref)

    hn = hn_ref[...]
    act = _silu(_dot(hn, wg_ref[...])) * _dot(hn, wu_ref[...])
    o_ref[...] += _dot(act.astype(BF16), wd_ref[...])

    @pl.when(f == pl.num_programs(1) - 1)
    def _():
        o_ref[...] = _rms(x_ref[...] + o_ref[...], gf_ref[...])


def _ffn_kernel(x_ref, g2_ref, wg_ref, wu_ref, wd_ref, gf_ref, o_ref, hn_ref):
    _ffn_step(x_ref, g2_ref, wg_ref, wu_ref, wd_ref, gf_ref, o_ref, hn_ref)


def ffn_final(x, g2, wg, wu, wd, gf, *, tm, tf):
    m, d = x.shape
    dff = wg.shape[1]
    grid = (m // tm, dff // tf)
    return pl.pallas_call(
        _ffn_kernel, grid=grid,
        in_specs=[
            pl.BlockSpec((tm, d), lambda i, f: (i, 0)),
            pl.BlockSpec((1, d), lambda i, f: (0, 0)),
            pl.BlockSpec((d, tf), lambda i, f: (0, f)),
            pl.BlockSpec((d, tf), lambda i, f: (0, f)),
            pl.BlockSpec((tf, d), lambda i, f: (f, 0)),
            pl.BlockSpec((1, d), lambda i, f: (0, 0)),
        ],
        out_specs=pl.BlockSpec((tm, d), lambda i, f: (i, 0)),
        out_shape=jax.ShapeDtypeStruct((m, d), F32),
        scratch_shapes=[pltpu.VMEM((tm, d), BF16)],
        compiler_params=_params(("parallel", "arbitrary")), name="ffn_final",
    )(x, g2.reshape(1, d), wg, wu, wd, gf.reshape(1, d))


def _mix_mem_kernel(x_ref, ys_ref, ya_ref, wo_ref, g_ref, wq_ref, mkv_ref, wm_ref, o_ref):
    half = ys_ref.shape[1]
    x1 = x_ref[...] + _dot(ys_ref[...], wo_ref[0:half, :]) + _dot(ya_ref[...], wo_ref[half:, :])
    hq = _dot(_rms(x1, g_ref[...]).astype(BF16), wq_ref[...])
    scale = MEM_HEAD_DIM ** -0.5
    outs = []
    for h in range(MEM_HEADS):
        sl = slice(h * MEM_HEAD_DIM, (h + 1) * MEM_HEAD_DIM)
        k = mkv_ref[:, sl].astype(BF16)
        v = mkv_ref[:, D_MEM + h * MEM_HEAD_DIM:D_MEM + (h + 1) * MEM_HEAD_DIM].astype(BF16)
        s = lax.dot_general(hq[:, sl].astype(BF16), k, (((1,), (1,)), ((), ())), preferred_element_type=F32) * scale
        p = jnp.exp(s - jnp.max(s, axis=-1, keepdims=True))
        p = p / jnp.sum(p, axis=-1, keepdims=True)
        outs.append(_dot(p.astype(BF16), v).astype(BF16))
    o_ref[...] = x1 + _dot(jnp.concatenate(outs, axis=1), wm_ref[...])


def mix_mem_fused(x, y_ssm, y_att, w_out, ln_mem_g, w_mq, mkv, w_mo, *, tm):
    m, d = x.shape
    half = y_ssm.shape[1]
    row = lambda width: pl.BlockSpec((tm, width), lambda i: (i, 0))
    whole = lambda a: pl.BlockSpec(a.shape, lambda i: (0, 0), pipeline_mode=pl.Buffered(1))
    g = ln_mem_g.reshape(1, d)
    return pl.pallas_call(
        _mix_mem_kernel, grid=(m // tm,),
        in_specs=[row(d), row(half), row(half), whole(w_out), whole(g), whole(w_mq), whole(mkv), whole(w_mo)],
        out_specs=row(d),
        out_shape=jax.ShapeDtypeStruct((m, d), F32),
        compiler_params=_params(("parallel",)), name="mix_mem_fused",
    )(x, y_ssm, y_att, w_out, g, w_mq, mkv, w_mo)


def _head_cols(x, g, width):
    rows = x.shape[0]
    parts = [jnp.broadcast_to(x[:, 4 * g + i:4 * g + i + 1], (rows, width)) for i in range(4)]
    return jnp.concatenate(parts, axis=1)


def _ssd_prompt_kernel(n_cast, n_chunks, z_ref, xs_ref, bc_ref, dt_ref, cw_ref, cb_ref, dtb_ref, alog_ref,
                       dskip_ref, ng_ref, *rest):
    cast_in = rest[:n_cast]
    y_ref, hfin_ref, convo_ref = rest[n_cast:n_cast + 3]
    cast_out = rest[n_cast + 3:2 * n_cast + 3]
    cbuf, st_ref = rest[2 * n_cast + 3:]
    c = pl.program_id(0)
    L = SSD_CHUNK
    P = HEAD_DIM
    GW = 4 * P

    @pl.when(c == 0)
    def _():
        cbuf[0:SUBLANES, :] = jnp.zeros((SUBLANES, XBC_DIM), F32)
        st_ref[...] = jnp.zeros_like(st_ref)

    ri = lax.broadcasted_iota(jnp.int32, (L, L), 0)
    ci = lax.broadcasted_iota(jnp.int32, (L, L), 1)
    causal = ri >= ci
    lane = lax.broadcasted_iota(jnp.int32, (L, GW), 1)
    head_lanes = [(lane >= i * P) & (lane < (i + 1) * P) for i in range(4)]

    for k in range(n_chunks):
        rs = slice(k * L, (k + 1) * L)
        cbuf[SUBLANES:SUBLANES + L, 0:D_SSM] = xs_ref[rs, :]
        cbuf[SUBLANES:SUBLANES + L, D_SSM:XBC_DIM] = bc_ref[rs, :]
        up = cbuf[...]
        taps = [(up if j == CONV_W - 1 else pltpu.roll(up, CONV_W - 1 - j, 0))[SUBLANES:SUBLANES + L]
                * cw_ref[j:j + 1, :] for j in range(CONV_W)]
        conv = cb_ref[...] + (((taps[0] + taps[1]) + taps[2]) + taps[3])

        if k == n_chunks - 1:
            @pl.when(c == pl.num_programs(0) - 1)
            def _():
                convo_ref[...] = cbuf[pl.ds(L + SUBLANES - (CONV_W - 1), CONV_W - 1), :]

        cbuf[0:SUBLANES, :] = cbuf[L:L + SUBLANES, :]

        act = _silu(conv)
        xs = act[:, 0:D_SSM]
        dt = _softplus(dt_ref[rs, :] + dtb_ref[...])
        a = dt * (-jnp.exp(alog_ref[...]))
        acum = jnp.dot(causal.astype(F32), a, precision=lax.Precision.HIGHEST, preferred_element_type=F32)
        acum_t = acum.T
        dt_t = dt.T
        last = acum[L - 1:L, :]
        ea = jnp.exp(acum)
        wend = jnp.exp(last - acum) * dt
        cdec = jnp.exp(last)

        if k == 0:
            for src, dst in zip(cast_in, cast_out):
                dst[...] = src[...].astype(dst.dtype)

        ys = []
        for g in range(SSM_GROUPS):
            bg = act[:, D_SSM + g * SSM_STATE:D_SSM + (g + 1) * SSM_STATE]
            cg = act[:, D_SSM + SSM_GROUPS * SSM_STATE + g * SSM_STATE:
                     D_SSM + SSM_GROUPS * SSM_STATE + (g + 1) * SSM_STATE]
            bgb = bg.astype(BF16)
            cgb = cg.astype(BF16)
            cb = lax.dot_general(cgb, bgb, (((1,), (1,)), ((), ())), preferred_element_type=F32)
            xg = xs[:, g * GW:(g + 1) * GW]
            yd = jnp.zeros((L, GW), F32)
            for i in range(4):
                h = 4 * g + i
                seg = acum[:, h:h + 1] - acum_t[h:h + 1, :]
                decay = jnp.exp(jnp.where(causal, seg, NEG))
                w = (cb * decay * dt_t[h:h + 1, :]).astype(BF16)
                xm = jnp.where(head_lanes[i], xg, 0.0).astype(BF16)
                yd = yd + _dot(w, xm)
            st = st_ref[g]
            yoff = _dot(cgb, st.astype(BF16)) * _head_cols(ea, g, P)
            ys.append(yd + yoff)
            xw = (xg * _head_cols(wend, g, P)).astype(BF16)
            new = _dot(bg.T.astype(BF16), xw)
            cd = jnp.concatenate(
                [jnp.broadcast_to(cdec[:, 4 * g + i:4 * g + i + 1], (1, P)) for i in range(4)], axis=1)
            st_ref[g] = st * cd + new

        y = jnp.concatenate(ys, axis=1) + dskip_ref[...] * xs
        y = y * _silu(z_ref[rs, :])
        outs = []
        for g in range(SSM_GROUPS):
            yg = y[:, g * GW:(g + 1) * GW]
            outs.append(yg * lax.rsqrt(jnp.mean(yg * yg, axis=-1, keepdims=True) + RMS_EPS))
        y_ref[rs, :] = (jnp.concatenate(outs, axis=1) * ng_ref[...]).astype(y_ref.dtype)

    @pl.when(c == pl.num_programs(0) - 1)
    def _():
        for g in range(SSM_GROUPS):
            stg = st_ref[g]
            for i in range(4):
                hfin_ref[4 * g + i] = stg[:, i * P:(i + 1) * P].T


def _pad_lanes(v):
    return jnp.pad(v.astype(F32), (0, LANES - v.shape[0])).reshape(1, LANES)


BF16_ROWS = 16
SSD_CHUNKS_PER_STEP = 4


def _cast_rows_per_step(rows, steps):
    return min(r for r in range(BF16_ROWS, rows + 1, BF16_ROWS) if rows % r == 0 and r * steps >= rows)


def ssd_prompt(proj, dt_raw, conv_w, conv_b, dt_bias, a_log, d_skip, norm_g, cast_weights=()):
    s = proj.shape[0]
    n_chunks = SSD_CHUNKS_PER_STEP if s % (SSD_CHUNKS_PER_STEP * SSD_CHUNK) == 0 else 1
    L = n_chunks * SSD_CHUNK
    steps = s // L
    col = lambda off: off // 1024
    const = lambda shape: pl.BlockSpec(shape, lambda c: (0,) * len(shape))
    cast_specs = []
    for w in cast_weights:
        rows = _cast_rows_per_step(w.shape[0], steps)
        last = w.shape[0] // rows - 1
        cast_specs.append(pl.BlockSpec((rows, w.shape[1]), lambda c, last=last: (jnp.minimum(c, last), 0)))
    return pl.pallas_call(
        functools.partial(_ssd_prompt_kernel, len(cast_weights), n_chunks), grid=(steps,),
        in_specs=[
            pl.BlockSpec((L, D_SSM), lambda c: (c, col(COL_Z))),
            pl.BlockSpec((L, D_SSM), lambda c: (c, col(COL_XBC))),
            pl.BlockSpec((L, XBC_DIM - D_SSM), lambda c: (c, col(COL_XBC + D_SSM))),
            pl.BlockSpec((L, LANES), lambda c: (c, 0)),
            const((CONV_W, XBC_DIM)), const((1, XBC_DIM)), const((1, LANES)), const((1, LANES)),
            const((1, D_SSM)), const((1, D_SSM)),
        ] + cast_specs,
        out_specs=[
            pl.BlockSpec((L, D_SSM), lambda c: (c, 0)),
            const((SSM_HEADS, HEAD_DIM, SSM_STATE)),
            const((CONV_W - 1, XBC_DIM)),
        ] + cast_specs,
        out_shape=[
            jax.ShapeDtypeStruct((s, D_SSM), BF16),
            jax.ShapeDtypeStruct((SSM_HEADS, HEAD_DIM, SSM_STATE), F32),
            jax.ShapeDtypeStruct((CONV_W - 1, XBC_DIM), F32),
        ] + [jax.ShapeDtypeStruct(w.shape, BF16) for w in cast_weights],
        scratch_shapes=[
            pltpu.VMEM((SSD_CHUNK + SUBLANES, XBC_DIM), F32),
            pltpu.VMEM((SSM_GROUPS, SSM_STATE, 4 * HEAD_DIM), F32),
        ],
        compiler_params=_params(("arbitrary",)), name="ssd_prompt",
    )(proj, proj, proj, dt_raw, conv_w, conv_b.reshape(1, XBC_DIM), _pad_lanes(dt_bias), _pad_lanes(a_log),
      jnp.repeat(d_skip, HEAD_DIM).reshape(1, D_SSM), norm_g.reshape(1, D_SSM), *cast_weights)


DIL_BLK = 128
DIL_GROUP = 16


def _dil_kernel(q_ref, k_ref, v_ref, o_ref, acc_ref, l_ref, ma_ref, mb_ref, bias_ref):
    s_len = q_ref.shape[0]
    B = DIL_BLK
    scale = HEAD_DIM ** -0.5
    lane = lax.broadcasted_iota(jnp.int32, (B, LANES), 1)
    head_a = lane < HEAD_DIM
    qi = lax.broadcasted_iota(jnp.int32, (2 * B, 2 * B), 0) & (B - 1)
    ki = lax.broadcasted_iota(jnp.int32, (2 * B, 2 * B), 1)
    band = (ki >= qi) & (ki <= qi + B)
    bias_ref[0] = jnp.where(band & (ki >= B), 0.0, NEG)
    bias_ref[1] = jnp.where(band, 0.0, NEG)
    ones = jnp.ones((2 * B, LANES), BF16)

    def comb(x):
        return jnp.where(head_a, x[:B], x[B:])

    def rows(ref, start, d):
        if d == 1:
            return ref[pl.ds(start, B), :]
        return ref[pl.ds(start, B, stride=d), :]

    def store_rows(ref, start, d, val):
        if d == 1:
            ref[pl.ds(start, B), :] = val
        else:
            ref[pl.ds(start, B, stride=d), :] = val

    order = sorted(DILATED_BRANCHES, key=lambda wd: -wd[1])
    assert order[-1][1] == 1
    for bi, (w, d) in enumerate(order):
        assert w // d == B
        first_branch = bi == 0
        last_branch = bi == len(order) - 1
        per_res = s_len // (B * d)
        G = min(DIL_GROUP, per_res)
        assert per_res % G == 0

        RG = max(1, DIL_GROUP // per_res)
        assert d % RG == 0

        def body(t, carry, d=d, first_branch=first_branch, last_branch=last_branch, per_res=per_res, G=G, RG=RG):
            r0 = (t // (per_res // G)) * RG
            n0 = (t % (per_res // G)) * G
            for rr in range(RG):
                r = r0 + rr
                starts = [(n0 + j) * (B * d) + r for j in range(G)]
                prev0 = jnp.maximum(n0 - 1, 0) * (B * d) + r
                if d == 1:
                    starts = [pl.multiple_of(st, B) for st in starts]
                    prev0 = pl.multiple_of(prev0, B)
                kb = [rows(k_ref, st, d).astype(BF16) for st in [prev0] + starts]
                vb = [rows(v_ref, st, d).astype(BF16) for st in [prev0] + starts]
                for j in range(G):
                    start = starts[j]
                    q = rows(q_ref, start, d) * scale
                    qs = jnp.concatenate([jnp.where(head_a, q, 0.0), jnp.where(head_a, 0.0, q)],
                                         axis=0).astype(BF16)
                    kk = jnp.concatenate([kb[j], kb[j + 1]], axis=0)
                    s = lax.dot_general(qs, kk, (((1,), (1,)), ((), ())), preferred_element_type=F32)
                    s = s + (bias_ref[jnp.minimum(n0, 1)] if j == 0 else bias_ref[1])
                    m_row = jnp.broadcast_to(jnp.max(s, axis=-1, keepdims=True), (2 * B, LANES))
                    if first_branch:
                        m_new = m_row
                    else:
                        m_old = jnp.concatenate([rows(ma_ref, start, d), rows(mb_ref, start, d)], axis=0)
                        m_new = jnp.maximum(m_old, m_row)
                    p = jnp.exp(s - jnp.concatenate([m_new, m_new], axis=1)).astype(BF16)
                    pv = _dot(p, jnp.concatenate([jnp.concatenate([vb[j], vb[j + 1]], axis=0), ones], axis=1))
                    l_new = comb(pv[:, LANES:])
                    pv = comb(pv[:, :LANES])
                    if first_branch:
                        acc, l_b = pv, l_new
                    else:
                        alpha = comb(jnp.exp(m_old - m_new))
                        acc = rows(acc_ref, start, d) * alpha + pv
                        l_b = rows(l_ref, start, d) * alpha + l_new
                    if last_branch:
                        o_ref[pl.ds(start, B), :] = (acc / l_b).astype(o_ref.dtype)
                    else:
                        store_rows(acc_ref, start, d, acc)
                        store_rows(l_ref, start, d, l_b)
                        store_rows(ma_ref, start, d, m_new[:B])
                        store_rows(mb_ref, start, d, m_new[B:])
            return carry

        lax.fori_loop(0, s_len // (B * G * RG), body, 0)


def dilated_prompt(proj):
    s = proj.shape[0]
    n_pairs = D_ATTN // LANES
    spec = lambda off: pl.BlockSpec((s, LANES), lambda h: (0, off // LANES + h))
    return pl.pallas_call(
        _dil_kernel, grid=(n_pairs,),
        in_specs=[spec(COL_Q), spec(COL_K), spec(COL_V)],
        out_specs=pl.BlockSpec((s, LANES), lambda h: (0, h)),
        out_shape=jax.ShapeDtypeStruct((s, D_ATTN), BF16),
        scratch_shapes=[pltpu.VMEM((s, LANES), F32)] * 4 + [pltpu.VMEM((2, 2 * DIL_BLK, 2 * DIL_BLK), F32)],
        compiler_params=_params(("parallel",)), name="dilated_prompt",
    )(proj, proj, proj)


SSD_SAMPLE_BB = 8


def _head_expand(x, seg_t):
    hi, mid, lo = _split3(x)
    return _dot(hi, seg_t) + _dot(mid, seg_t) + _dot(lo, seg_t)


def _ssd_sample_kernel(z_ref, xs_ref, bc_ref, dt_ref, sc_ref, st_ref, cw_ref, cb_ref, dtb_ref, alog_ref,
                       dskip_ref, ng_ref, segt_ref,
                       y_ref, ho_ref, convo_ref,
                       xs_s, t2_s, dec_s, b_s, c_s, col_s, yoff_s):
    step = pl.program_id(0)
    nb = xs_ref.shape[0]
    BB = SSD_SAMPLE_BB
    GW = 4 * HEAD_DIM
    HP = SSM_HEADS * HEAD_DIM

    @pl.when(step == 0)
    def _():
        xnew = jnp.concatenate([xs_ref[...], bc_ref[...]], axis=1)
        rows = [sc_ref[:, j * XBC_DIM:(j + 1) * XBC_DIM] for j in range(CONV_W - 1)] + [xnew]
        taps = [rows[j] * cw_ref[j:j + 1, :] for j in range(CONV_W)]
        act = _silu(cb_ref[...] + (((taps[0] + taps[1]) + taps[2]) + taps[3]))
        for j in range(CONV_W - 1):
            convo_ref[:, j * XBC_DIM:(j + 1) * XBC_DIM] = rows[j + 1]
        xs = act[:, 0:D_SSM]
        bm = act[:, D_SSM:D_SSM + SSM_GROUPS * SSM_STATE]
        cm = act[:, D_SSM + SSM_GROUPS * SSM_STATE:]
        dt = _softplus(dt_ref[...] + dtb_ref[...])
        dec = jnp.exp(dt * (-jnp.exp(alog_ref[...])))
        seg_t = segt_ref[...]
        u = _head_expand(dt, seg_t) * xs
        dec_e = _head_expand(dec, seg_t)
        xs_s[...] = xs
        dec_s[...] = dec_e
        b_s[...] = bm
        c_s[...] = cm
        for g in range(SSM_GROUPS):
            sl = slice(g * SSM_STATE, (g + 1) * SSM_STATE)
            bc = jnp.sum(bm[:, sl] * cm[:, sl], axis=-1, keepdims=True)
            t2_s[:, g * GW:(g + 1) * GW] = u[:, g * GW:(g + 1) * GW] * bc
        for j in range(HP // LANES):
            for src, base in ((u, 0), (dec_e, HP)):
                t = src[:, j * LANES:(j + 1) * LANES].T
                for k, part in enumerate(_split3(t)):
                    col_s[base + j * LANES:base + (j + 1) * LANES, k * nb:(k + 1) * nb] = part

    def per_pair(ip, carry):
        b0 = step * BB + 2 * ip
        ridx = lax.broadcasted_iota(jnp.int32, (3 * nb, 2 * SSM_STATE), 0)
        lane = lax.broadcasted_iota(jnp.int32, (3 * nb, 2 * SSM_STATE), 1)
        tgt = b0 + (lane >= SSM_STATE).astype(jnp.int32)
        sel = ((ridx == tgt) | (ridx == tgt + nb) | (ridx == tgt + 2 * nb)).astype(BF16)
        cols2 = _dot(col_s[...], sel)
        c8 = c_s[pl.ds(pl.multiple_of(step * BB, BB), BB), :]
        for j in range(2):
            i = 2 * ip + j
            b = b0 + j
            cols = cols2[:, j * SSM_STATE:(j + 1) * SSM_STATE]
            h0 = st_ref[i].reshape(HP, SSM_STATE)
            brow = b_s[pl.ds(b, 1), :]
            pick = lax.broadcasted_iota(jnp.int32, (BB, GW), 0) == i
            for g in range(SSM_GROUPS):
                rs = slice(g * GW, (g + 1) * GW)
                ls = slice(g * SSM_STATE, (g + 1) * SSM_STATE)
                h0g = h0[rs]
                hn = cols[HP + g * GW:HP + (g + 1) * GW] * h0g + cols[rs] * brow[:, ls]
                ho_ref[i, 4 * g:4 * g + 4] = hn.reshape(4, HEAD_DIM, SSM_STATE)
                yo = lax.dot_general(c8[:, ls].astype(BF16), h0g.astype(BF16), (((1,), (1,)), ((), ())),
                                     preferred_element_type=F32)
                yoff_s[pl.ds(b, 1), rs] = jnp.sum(jnp.where(pick, yo, 0.0), axis=0, keepdims=True)
        return carry

    lax.fori_loop(0, BB // 2, per_pair, 0, unroll=2)

    @pl.when(step == pl.num_programs(0) - 1)
    def _():
        xs = xs_s[...]
        y = dec_s[...] * yoff_s[...] + t2_s[...] + dskip_ref[...] * xs
        y = y * _silu(z_ref[...])
        for g in range(SSM_GROUPS):
            yg = y[:, g * GW:(g + 1) * GW]
            yg = yg * lax.rsqrt(jnp.mean(yg * yg, axis=-1, keepdims=True) + RMS_EPS)
            y_ref[:, g * GW:(g + 1) * GW] = (yg * ng_ref[:, g * GW:(g + 1) * GW]).astype(y_ref.dtype)


def _seg_ones(n_heads, width):
    c = jnp.arange(n_heads * width)[:, None] // width
    seg = (c == jnp.arange(LANES)[None, :]).astype(BF16)
    return seg, seg.T


def ssd_sample(proj, dt_raw, state_conv, state_ssm, conv_w, conv_b, dt_bias, a_log, d_skip, norm_g):
    nb = proj.shape[0]
    BB = SSD_SAMPLE_BB
    HP = SSM_HEADS * HEAD_DIM
    col = lambda off: off // 1024
    const = lambda shape: pl.BlockSpec(shape, lambda s: (0,) * len(shape))
    _, seg_t = _seg_ones(SSM_HEADS, HEAD_DIM)
    st_spec = pl.BlockSpec((BB, SSM_HEADS, HEAD_DIM, SSM_STATE), lambda s: (s, 0, 0, 0))
    y, h_new, conv_new = pl.pallas_call(
        _ssd_sample_kernel, grid=(nb // BB,),
        in_specs=[
            pl.BlockSpec((nb, D_SSM), lambda s: (0, col(COL_Z))),
            pl.BlockSpec((nb, D_SSM), lambda s: (0, col(COL_XBC))),
            pl.BlockSpec((nb, XBC_DIM - D_SSM), lambda s: (0, col(COL_XBC + D_SSM))),
            const((nb, LANES)), const((nb, (CONV_W - 1) * XBC_DIM)), st_spec,
            const((CONV_W, XBC_DIM)), const((1, XBC_DIM)), const((1, LANES)), const((1, LANES)),
            const((1, D_SSM)), const((1, D_SSM)), const((LANES, HP)),
        ],
        out_specs=[const((nb, D_SSM)), st_spec, const((nb, (CONV_W - 1) * XBC_DIM))],
        out_shape=[
            jax.ShapeDtypeStruct((nb, D_SSM), BF16),
            jax.ShapeDtypeStruct(state_ssm.shape, F32),
            jax.ShapeDtypeStruct((nb, (CONV_W - 1) * XBC_DIM), F32),
        ],
        scratch_shapes=[
            pltpu.VMEM((nb, HP), F32), pltpu.VMEM((nb, HP), F32), pltpu.VMEM((nb, HP), F32),
            pltpu.VMEM((nb, SSM_GROUPS * SSM_STATE), F32), pltpu.VMEM((nb, SSM_GROUPS * SSM_STATE), F32),
            pltpu.VMEM((2 * HP, 3 * nb), BF16), pltpu.VMEM((nb, HP), F32),
        ],
        compiler_params=_params(("arbitrary",)), name="ssd_sample",
    )(proj, proj, proj, dt_raw, state_conv.reshape(nb, (CONV_W - 1) * XBC_DIM), state_ssm, conv_w,
      conv_b.reshape(1, XBC_DIM), _pad_lanes(dt_bias), _pad_lanes(a_log),
      jnp.repeat(d_skip, HEAD_DIM).reshape(1, D_SSM), norm_g.reshape(1, D_SSM), seg_t)
    return y, h_new, conv_new.reshape(nb, CONV_W - 1, XBC_DIM)


WIN_UNIT_HEADS = 4
WIN_UNITS_PER_STEP = 3


def _win_unit_attention(n_branch, q4, kn4, vn4, cnt, kt, vt):
    rep = lambda t: jnp.stack([jnp.broadcast_to(t[h:h + 1, :], (SUBLANES, t.shape[1])) for h in range(t.shape[0])])
    q = rep(q4) * (HEAD_DIM ** -0.5)
    s = jnp.einsum("hqd,hdw->hqw", q.astype(BF16), kt.astype(BF16), preferred_element_type=F32)
    s = jnp.where(cnt > 0.0, s, NEG)
    s_new = jnp.sum(q * rep(kn4), axis=-1, keepdims=True)
    m = jnp.maximum(jnp.max(s, axis=-1, keepdims=True), s_new)
    p = cnt * jnp.exp(s - m)
    p_new = n_branch * jnp.exp(s_new - m)
    l = jnp.sum(p, axis=-1, keepdims=True) + p_new
    o = jnp.einsum("hqw,hdw->hqd", p.astype(BF16), vt.astype(BF16), preferred_element_type=F32)
    return (o + p_new * rep(vn4)) / l


def _ffn_stream_kernel(n_branch, x_ref, g2_ref, wg_ref, wu_ref, wd_ref, gf_ref,
                       q_ref, kn_ref, vn_ref, cnt_ref, kt_hbm, vt_hbm,
                       o_ref, oa_ref, hn_ref, kbuf, vbuf, sem):
    step = pl.program_id(0) * pl.num_programs(1) + pl.program_id(1)
    uh = WIN_UNIT_HEADS
    per_seq = kt_hbm.shape[1] // uh
    n_units = kt_hbm.shape[0] * per_seq

    def copies(u, slot):
        b = u // per_seq
        h0 = (u % per_seq) * uh
        return (pltpu.make_async_copy(kt_hbm.at[b, pl.ds(h0, uh)], kbuf.at[slot], sem.at[0, slot]),
                pltpu.make_async_copy(vt_hbm.at[b, pl.ds(h0, uh)], vbuf.at[slot], sem.at[1, slot]))

    @pl.when(step == 0)
    def _():
        for slot in range(WIN_UNITS_PER_STEP):
            for c in copies(slot, slot):
                c.start()

    cnt = cnt_ref[...][None]
    for slot in range(WIN_UNITS_PER_STEP):
        u = step * WIN_UNITS_PER_STEP + slot

        @pl.when(u < n_units)
        def _(u=u, slot=slot):
            for c in copies(u, slot):
                c.wait()
            b = u // per_seq
            h0 = (u % per_seq) * uh
            rows = pl.ds(h0, uh)
            o = _win_unit_attention(n_branch, q_ref[b, rows, :], kn_ref[b, rows, :], vn_ref[b, rows, :], cnt,
                                    kbuf[slot], vbuf[slot])
            for h in range(uh):
                oa_ref[b, pl.ds(h0 + h, 1), :] = o[h, 0:1, :]

            @pl.when(u + WIN_UNITS_PER_STEP < n_units)
            def _():
                for c in copies(u + WIN_UNITS_PER_STEP, slot):
                    c.start()

    _ffn_step(x_ref, g2_ref, wg_ref, wu_ref, wd_ref, gf_ref, o_ref, hn_ref)


def ffn_final_with_window_attention(x, g2, wg, wu, wd, gf, q, kn, vn, cache_k, cache_v, *, tm, tf):
    m, d = x.shape
    dff = wg.shape[1]
    nb, w_buf = cache_k.shape[0], cache_k.shape[1]
    grid = (m // tm, dff // tf)
    assert nb * (ATTN_HEADS // WIN_UNIT_HEADS) <= grid[0] * grid[1] * WIN_UNITS_PER_STEP
    pos = jnp.arange(w_buf)
    cnt = jnp.zeros((w_buf,), F32)
    for w, dil in DILATED_BRANCHES:
        cnt = cnt + ((pos >= w_buf - w) & ((w_buf - pos) % dil == 0)).astype(F32)
    const = lambda shape: pl.BlockSpec(shape, lambda i, f: (0,) * len(shape))
    small = const((nb, ATTN_HEADS, HEAD_DIM))
    unit = (WIN_UNITS_PER_STEP, WIN_UNIT_HEADS, HEAD_DIM, w_buf)
    y, att = pl.pallas_call(
        functools.partial(_ffn_stream_kernel, float(len(DILATED_BRANCHES))), grid=grid,
        in_specs=[
            pl.BlockSpec((tm, d), lambda i, f: (i, 0)),
            const((1, d)),
            pl.BlockSpec((d, tf), lambda i, f: (0, f)),
            pl.BlockSpec((d, tf), lambda i, f: (0, f)),
            pl.BlockSpec((tf, d), lambda i, f: (f, 0)),
            const((1, d)),
            small, small, small, const((1, w_buf)),
            pl.BlockSpec(memory_space=pl.ANY), pl.BlockSpec(memory_space=pl.ANY),
        ],
        out_specs=[pl.BlockSpec((tm, d), lambda i, f: (i, 0)), small],
        out_shape=[jax.ShapeDtypeStruct((m, d), F32), jax.ShapeDtypeStruct((nb, ATTN_HEADS, HEAD_DIM), F32)],
        scratch_shapes=[pltpu.VMEM((tm, d), BF16), pltpu.VMEM(unit, F32), pltpu.VMEM(unit, F32),
                        pltpu.SemaphoreType.DMA((2, WIN_UNITS_PER_STEP))],
        compiler_params=_params(("arbitrary", "arbitrary")), name="ffn_final_window_stream",
    )(x, g2.reshape(1, d), wg, wu, wd, gf.reshape(1, d), q, kn, vn, cnt.reshape(1, w_buf),
      jnp.transpose(cache_k, (0, 2, 3, 1)), jnp.transpose(cache_v, (0, 2, 3, 1)))
    return y, att.reshape(nb, D_ATTN)


MEM_SAMPLE_BB = 4


def _mem_sample_kernel(q_ref, k_ref, v_ref, o_ref):
    shape = (q_ref.shape[1], k_ref.shape[1])
    row = lax.broadcasted_iota(jnp.int32, shape, 0)
    col = lax.broadcasted_iota(jnp.int32, shape, 1)
    own = (col & (MEM_HEADS - 1)) == (row & (MEM_HEADS - 1))
    for i in range(q_ref.shape[0]):
        q = (q_ref[i] * (MEM_HEAD_DIM ** -0.5)).astype(BF16)
        s = lax.dot_general(q, k_ref[i].astype(BF16), (((1,), (1,)), ((), ())), preferred_element_type=F32)
        s = jnp.where(own, s, NEG)
        m = jnp.max(s, axis=-1, keepdims=True)
        p = jnp.where(own, jnp.exp(s - m), 0.0)
        l = jnp.sum(p, axis=-1, keepdims=True)
        o_ref[i] = _dot((p / l).astype(BF16), v_ref[i].astype(BF16))


def mem_attention_sample(q, cache_mk, cache_mv):
    nb = q.shape[0]
    assert MEM_HEADS & (MEM_HEADS - 1) == 0 and SUBLANES % MEM_HEADS == 0
    q8 = jnp.tile(q.reshape(nb, MEM_HEADS, MEM_HEAD_DIM), (1, SUBLANES // MEM_HEADS, 1))
    rows = MEM_LEN * MEM_HEADS
    bb = MEM_SAMPLE_BB
    small = pl.BlockSpec((bb, SUBLANES, MEM_HEAD_DIM), lambda b: (b, 0, 0))
    big = pl.BlockSpec((bb, rows, MEM_HEAD_DIM), lambda b: (b, 0, 0))
    out = pl.pallas_call(
        _mem_sample_kernel, grid=(nb // bb,), in_specs=[small, big, big], out_specs=small,
        out_shape=jax.ShapeDtypeStruct((nb, SUBLANES, MEM_HEAD_DIM), F32),
        compiler_params=_params(("parallel",)), name="mem_attention_sample",
    )(q8, cache_mk.reshape(nb, rows, MEM_HEAD_DIM), cache_mv.reshape(nb, rows, MEM_HEAD_DIM))
    return out[:, :MEM_HEADS].reshape(nb, D_MEM)


def _layer_tail(x, y_ssm, y_att, mem_attend, lw, gf, *, tm, tf):
    half = D_SSM
    x1 = matmul_residual([(y_ssm, lw["w_out"], 0), (y_att, lw["w_out"], 1)], x, tm=tm, tn=1024)
    assert y_ssm.shape[1] == half and y_att.shape[1] == half
    hq = rms_matmul(x1, lw["ln_mem_g"], lw["w_mq"], tm=tm, tn=D_MEM)
    x2 = matmul_residual([(mem_attend(hq), lw["w_mo"], 0)], x1, tm=tm, tn=1024)
    return ffn_final(x2, lw["ln2_g"], lw["w_gate"], lw["w_up"], lw["w_down"], gf, tm=min(tm, 512), tf=tf)


def kernel(x_prompt, x_sample, cache_win_k, cache_win_v, state_conv, state_ssm, cache_mem_k, cache_mem_v, mem_prompt, ln1_g, w_in, conv_w, conv_b, dt_bias, a_log, d_skip, ssm_norm_g, w_out, ln_mem_g, mem_norm_g, w_mq, w_mk, w_mv, w_mo, ln2_g, w_gate, w_up, w_down, ln_f_g):
    bp, s, d = x_prompt.shape
    nb, ts, _ = x_sample.shape
    depth = w_in.shape[0]
    assert bp == 1 and ts == 1 and depth == 1
    w_keep = min(DILATED_BRANCHES[-1][0], s)
    l = 0
    dt_lo = D_SSM + XBC_DIM
    dt_hi = dt_lo + SSM_HEADS
    w_in_t = jnp.transpose(w_in[l])
    lw = dict(ln_mem_g=ln_mem_g[l], w_mq=w_mq[l].astype(BF16), w_mo=w_mo[l].astype(BF16), ln2_g=ln2_g[l])
    w_mkv = jnp.concatenate([w_mk[l], w_mv[l]], axis=1).astype(BF16)
    ssm_w = (conv_w[l], conv_b[l], dt_bias[l], a_log[l], d_skip[l], ssm_norm_g[l])

    xs = x_sample.reshape(nb, d)
    proj_s, dt_s = rms_matmul_dt(xs, ln1_g[l], w_in_t, dt_lo, dt_hi, tm=nb, tn=1024)
    heads = lambda t: t.reshape(t.shape[0], ATTN_HEADS, HEAD_DIM)
    q_s, k_s, v_s = (heads(proj_s[:, c:c + D_ATTN]) for c in (COL_Q, COL_K, COL_V))

    xp = x_prompt.reshape(s, d)
    mkv = rms_matmul(mem_prompt.reshape(MEM_LEN, d), mem_norm_g[l], w_mkv, tm=MEM_LEN, tn=2 * D_MEM)
    proj, dt_raw = rms_matmul_dt(xp, ln1_g[l], w_in_t, dt_lo, dt_hi, tm=1024, tn=1024)
    y_ssm, h_p, conv_p, lw["w_out"], lw["w_gate"], lw["w_up"], lw["w_down"] = ssd_prompt(
        proj, dt_raw, *ssm_w, cast_weights=(w_out[l], w_gate[l], w_up[l], w_down[l]))
    y_att = dilated_prompt(proj)
    x2 = mix_mem_fused(xp, y_ssm, y_att, lw["w_out"], lw["ln_mem_g"], lw["w_mq"], mkv, lw["w_mo"], tm=512)
    y_prompt, y_att_s = ffn_final_with_window_attention(
        x2, lw["ln2_g"], lw["w_gate"], lw["w_up"], lw["w_down"], ln_f_g,
        q_s, k_s, v_s, cache_win_k[l], cache_win_v[l], tm=512, tf=512)

    y_ssm_s, h_s, conv_s = ssd_sample(proj_s, dt_s, state_conv[l], state_ssm[l], *ssm_w)
    y_sample = _layer_tail(xs, y_ssm_s, y_att_s,
                           lambda hq: mem_attention_sample(hq, cache_mem_k[l], cache_mem_v[l]), lw, ln_f_g,
                           tm=nb, tf=512)

    return (
        y_prompt.reshape(bp, s, d),
        y_sample.reshape(nb, ts, d),
        heads(proj[s - w_keep:, COL_K:COL_V])[None, None],
        heads(proj[s - w_keep:, COL_V:])[None, None],
        conv_p[None, None],
        h_p[None, None],
        mkv[:, :D_MEM].reshape(1, 1, MEM_LEN, MEM_HEADS, MEM_HEAD_DIM),
        mkv[:, D_MEM:].reshape(1, 1, MEM_LEN, MEM_HEADS, MEM_HEAD_DIM),
        k_s[None, :, None],
        v_s[None, :, None],
        conv_s[None],
        h_s[None],
    )
```
